```python
import math
import jax, jax.numpy as jnp
from jax import lax
import numpy as np

D_MODEL = 1024
BATCH = 4
SEQ = 4096
DEPTH = 1

CHUNK = 64
EPS = 1e-6
HG_HEADS = 8
HG_DK = 128
HG_DV = D_MODEL // HG_HEADS
HG_WIDTH_K = HG_HEADS * HG_DK
HG_WIDTH_V = HG_HEADS * HG_DV
GDN_QK_HEADS = 8
GDN_V_HEADS = 16
GDN_DK = 128
GDN_DV = 128
GDN_WIDTH_K = GDN_QK_HEADS * GDN_DK
GDN_WIDTH_V = GDN_V_HEADS * GDN_DV
CONV_K = 4
D_FF = 2816
IN_SIZES = (HG_WIDTH_K, HG_WIDTH_K, HG_WIDTH_V, HG_WIDTH_V,
            GDN_WIDTH_K, GDN_WIDTH_K, GDN_WIDTH_V, GDN_V_HEADS, GDN_V_HEADS, GDN_WIDTH_V,
            D_MODEL, D_MODEL)
IN_WIDTH = sum(IN_SIZES)

kernel_name = "hgrn2_gdn_gated_macaron_block"


def rmsnorm(x, g):
    xf = x.astype(jnp.float32)
    y = xf * lax.rsqrt(jnp.mean(xf * xf, axis=-1, keepdims=True) + EPS)
    return (y * g).astype(x.dtype)


def l2norm(x):
    return x * lax.rsqrt(jnp.sum(x * x, axis=-1, keepdims=True) + EPS)


def swiglu(x, w_in, w_out):
    a, b = jnp.split(x @ w_in, 2, axis=-1)
    return (jax.nn.silu(a) * b) @ w_out


def to_chunks(t, n_heads):
    B, S = t.shape[:2]
    t = t.reshape(B, S // CHUNK, CHUNK, n_heads, -1)
    return jnp.transpose(t, (0, 3, 1, 2, 4))


def from_chunks(t):
    B, H, NC, C, d = t.shape
    return jnp.transpose(t, (0, 2, 3, 1, 4)).reshape(B, NC * C, H, d)


def causal_short_conv(x, w):
    K = w.shape[0]
    S = x.shape[1]
    xp = jnp.pad(x, ((0, 0), (K - 1, 0), (0, 0)))
    return sum(xp[:, j:j + S] * w[j] for j in range(K))


def hgrn2_chunked(q, k, v, log_f):
    B, H, NC, C, DK = q.shape
    DV = v.shape[-1]
    b_cum = jnp.cumsum(log_f, axis=3)
    causal = jnp.tril(jnp.ones((C, C), dtype=bool))[:, :, None]

    def step(S, inp):
        q_c, k_c, v_c, b_c = inp
        inter = jnp.einsum('bhtk,bhkv->bhtv', q_c * jnp.exp(b_c), S)
        diff = b_c[:, :, :, None, :] - b_c[:, :, None, :, :]
        decay = jnp.exp(jnp.where(causal, diff, -jnp.inf))
        scores = jnp.einsum('bhtk,bhsk,bhtsk->bhts', q_c, k_c, decay)
        intra = jnp.einsum('bhts,bhsv->bhtv', scores, v_c)
        b_end = b_c[:, :, -1, :]
        k_to_end = k_c * jnp.exp(b_end[:, :, None, :] - b_c)
        S = jnp.exp(b_end)[..., None] * S + jnp.einsum('bhsk,bhsv->bhkv', k_to_end, v_c)
        return S, inter + intra

    S0 = jnp.zeros((B, H, DK, DV), jnp.float32)
    xs = (jnp.moveaxis(q, 2, 0), jnp.moveaxis(k, 2, 0), jnp.moveaxis(v, 2, 0), jnp.moveaxis(b_cum, 2, 0))
    _, o = lax.scan(step, S0, xs)
    return jnp.moveaxis(o, 0, 2)


def gated_delta_chunked(q, k, v, beta, g):
    B, H, NC, C, DK = q.shape
    DV = v.shape[-1]
    gam = jnp.cumsum(g, axis=-1)
    incl = jnp.tril(jnp.ones((C, C), dtype=bool))
    strict = jnp.tril(jnp.ones((C, C), dtype=bool), -1)
    diff = gam[..., :, None] - gam[..., None, :]
    Lmat = jnp.exp(jnp.where(incl, diff, -jnp.inf))
    kb = k * beta[..., None]
    A = jnp.where(strict, jnp.einsum('bhntk,bhnsk->bhnts', kb, k) * Lmat, 0.0)
    eye = jnp.eye(C, dtype=A.dtype)
    T = lax.linalg.triangular_solve(eye + A, jnp.broadcast_to(eye, A.shape),
                                    left_side=True, lower=True, unit_diagonal=True)
    u = jnp.matmul(T, v * beta[..., None])
    w = jnp.matmul(T, kb * jnp.exp(gam)[..., None])
    qk = jnp.einsum('bhntk,bhnsk->bhnts', q, k) * Lmat

    def step(S, inp):
        q_c, k_c, u_c, w_c, qk_c, gam_c = inp
        v_new = u_c - jnp.einsum('bhtk,bhkv->bhtv', w_c, S)
        o = (jnp.einsum('bhtk,bhkv->bhtv', q_c * jnp.exp(gam_c)[..., None], S)
             + jnp.einsum('bhts,bhsv->bhtv', qk_c, v_new))
        g_end = gam_c[..., -1]
        k_to_end = k_c * jnp.exp(g_end[..., None] - gam_c)[..., None]
        S = S * jnp.exp(g_end)[..., None, None] + jnp.einsum('bhsk,bhsv->bhkv', k_to_end, v_new)
        return S, o

    S0 = jnp.zeros((B, H, DK, DV), jnp.float32)
    xs = tuple(jnp.moveaxis(t, 2, 0) for t in (q, k, u, w, qk, gam))
    _, o = lax.scan(step, S0, xs)
    return jnp.moveaxis(o, 0, 2)


def hybrid_mixer(u, w_in, lb, hgrn_out_norm, conv_w, a_log, dt_bias, gdn_out_norm,
                 w_branch_hgrn, w_branch_gdn, w_out):
    B, S, _ = u.shape
    f32 = jnp.float32
    proj = (u @ w_in).astype(f32)
    offsets = np.cumsum(IN_SIZES)[:-1].tolist()
    (hq, hf, hi, hg, gq, gk, gv, ga, gb, gz, gate_h, gate_g) = jnp.split(proj, offsets, axis=-1)

    lb = lb.astype(f32)
    log_f = jnp.logaddexp(jnp.log(lb), jnp.log1p(-lb) + jax.nn.log_sigmoid(hf))
    k_h = -jnp.expm1(log_f)
    q_h = jax.nn.silu(hq) * HG_DK ** -0.5
    o_h = hgrn2_chunked(to_chunks(q_h, HG_HEADS), to_chunks(k_h, HG_HEADS),
                        to_chunks(hi, HG_HEADS), to_chunks(log_f, HG_HEADS))
    o_h = rmsnorm(from_chunks(o_h), hgrn_out_norm) * jax.nn.silu(hg).reshape(B, S, HG_HEADS, HG_DV)
    y_h = o_h.reshape(B, S, HG_WIDTH_V) @ w_branch_hgrn

    qkv = jax.nn.silu(causal_short_conv(jnp.concatenate([gq, gk, gv], axis=-1), conv_w))
    cq, ck, cv = jnp.split(qkv, [GDN_WIDTH_K, 2 * GDN_WIDTH_K], axis=-1)
    rep = GDN_V_HEADS // GDN_QK_HEADS
    q_g = l2norm(cq.reshape(B, S, GDN_QK_HEADS, GDN_DK)) * GDN_DK ** -0.5
    k_g = l2norm(ck.reshape(B, S, GDN_QK_HEADS, GDN_DK))
    q_g = jnp.repeat(q_g, rep, axis=2).reshape(B, S, GDN_V_HEADS * GDN_DK)
    k_g = jnp.repeat(k_g, rep, axis=2).reshape(B, S, GDN_V_HEADS * GDN_DK)
    beta = jax.nn.sigmoid(gb)
    g = -jnp.exp(a_log.astype(f32)) * jax.nn.softplus(ga + dt_bias)
    o_g = gated_delta_chunked(to_chunks(q_g, GDN_V_HEADS), to_chunks(k_g, GDN_V_HEADS),
                              to_chunks(cv, GDN_V_HEADS),
                              to_chunks(beta[..., None], GDN_V_HEADS)[..., 0],
                              to_chunks(g[..., None], GDN_V_HEADS)[..., 0])
    o_g = rmsnorm(from_chunks(o_g), gdn_out_norm) * jax.nn.silu(gz).reshape(B, S, GDN_V_HEADS, GDN_DV)
    y_g = o_g.reshape(B, S, GDN_WIDTH_V) @ w_branch_gdn

    y = jax.nn.sigmoid(gate_h) * y_h + jax.nn.sigmoid(gate_g) * y_g
    return (y @ w_out).astype(u.dtype)


def setup_inputs(seed: int = 0) -> dict:
    key = jax.random.key(seed)
    ks = jax.random.split(key, 20)
    f32 = jnp.float32
    L = DEPTH

    def dense(k, shape):
        return jax.random.normal(k, shape, f32) * shape[-2] ** -0.5

    def gain(k, shape):
        return 1.0 + 0.05 * jax.random.normal(k, shape, f32)

    A = jax.random.uniform(ks[8], (L, GDN_V_HEADS), f32, 1.0, 16.0)
    dt = jnp.exp(jax.random.uniform(ks[9], (L, GDN_V_HEADS), f32, math.log(1e-3), math.log(1e-1)))
    dt_bias = dt + jnp.log(-jnp.expm1(-dt))
    return {
        "x": jax.random.normal(ks[0], (BATCH, SEQ, D_MODEL), f32),
        "ffn1_norm": gain(ks[1], (L, D_MODEL)),
        "ffn1_w_in": dense(ks[2], (L, D_MODEL, 2 * D_FF)),
        "ffn1_w_out": dense(ks[3], (L, D_FF, D_MODEL)),
        "mix_norm": gain(ks[4], (L, D_MODEL)),
        "w_in": dense(ks[5], (L, D_MODEL, IN_WIDTH)),
        "hgrn_lb_logits": 0.5 * jax.random.normal(ks[6], (L + 1, HG_WIDTH_K), f32),
        "hgrn_out_norm": gain(ks[7], (L, HG_DV)),
        "gdn_conv_w": 0.5 * jax.random.normal(ks[10], (L, CONV_K, 2 * GDN_WIDTH_K + GDN_WIDTH_V), f32),
        "gdn_a_log": jnp.log(A),
        "gdn_dt_bias": dt_bias,
        "gdn_out_norm": gain(ks[11], (L, GDN_DV)),
        "w_branch_hgrn": dense(ks[12], (L, HG_WIDTH_V, D_MODEL)),
        "w_branch_gdn": dense(ks[13], (L, GDN_WIDTH_V, D_MODEL)),
        "w_out": dense(ks[14], (L, D_MODEL, D_MODEL)),
        "ffn2_norm": gain(ks[15], (L, D_MODEL)),
        "ffn2_w_in": dense(ks[16], (L, D_MODEL, 2 * D_FF)),
        "ffn2_w_out": dense(ks[17], (L, D_FF, D_MODEL)),
        "final_norm": gain(ks[18], (D_MODEL,)),
    }


def reference(x, ffn1_norm, ffn1_w_in, ffn1_w_out, mix_norm, w_in, hgrn_lb_logits,
              hgrn_out_norm, gdn_conv_w, gdn_a_log, gdn_dt_bias, gdn_out_norm,
              w_branch_hgrn, w_branch_gdn, w_out, ffn2_norm, ffn2_w_in, ffn2_w_out,
              final_norm):
    lb_all = jnp.cumsum(jax.nn.softmax(hgrn_lb_logits.astype(jnp.float32), axis=0), axis=0)
    h = x
    for l in range(DEPTH):
        h = h + 0.5 * swiglu(rmsnorm(h, ffn1_norm[l]), ffn1_w_in[l], ffn1_w_out[l])
        h = h + hybrid_mixer(rmsnorm(h, mix_norm[l]), w_in[l], lb_all[l], hgrn_out_norm[l],
                             gdn_conv_w[l], gdn_a_log[l], gdn_dt_bias[l], gdn_out_norm[l],
                             w_branch_hgrn[l], w_branch_gdn[l], w_out[l])
        h = h + 0.5 * swiglu(rmsnorm(h, ffn2_norm[l]), ffn2_w_in[l], ffn2_w_out[l])
    return rmsnorm(h, final_norm)
```

```python
import functools

import jax
import jax.numpy as jnp
from jax import lax
from jax.experimental import pallas as pl
from jax.experimental.pallas import tpu as pltpu

F32 = jnp.float32
BF16 = jnp.bfloat16
HIGHEST = lax.Precision.HIGHEST

EPS = 1e-6
CHUNK = 64
SUB = 16
HEAD_DIM = 128
HG_HEADS = 8
GDN_QK_HEADS = 8
GDN_V_HEADS = 16
CONV_K = 4
LANES = 128
SUBLANES = 8
NEG_BIG = -1e30
VMEM_LIMIT = 56 * 1024 * 1024


def _dot(a, b):
    return jnp.dot(a, b, preferred_element_type=F32)


def _dot_nt(a, b):
    return lax.dot_general(a, b, (((1,), (1,)), ((), ())), preferred_element_type=F32)


def _dot_tn(a, b):
    return lax.dot_general(a, b, (((0,), (0,)), ((), ())), preferred_element_type=F32)


def _sigmoid(x):
    return 1.0 / (1.0 + jnp.exp(-x))


def _rms(x, w):
    ms = jnp.mean(x * x, axis=-1, keepdims=True)
    return x * lax.rsqrt(ms + EPS) * w


def _params(*sem):
    return pltpu.CompilerParams(dimension_semantics=sem, vmem_limit_bytes=VMEM_LIMIT)


def _resident(shape):
    return pl.BlockSpec(shape, lambda *_: (0,) * len(shape), pipeline_mode=pl.Buffered(1))


def _ffn_body(h_ref, nw_ref, wa_ref, wb_ref, wo_ref, onw_ref, *out_refs, ff_tile, final):
    h = h_ref[...]
    xn = _rms(h, nw_ref[...]).astype(BF16)
    d_ff = wa_ref.shape[1]
    acc = jnp.zeros(h.shape, F32)
    for j in range(0, d_ff, ff_tile):
        w = min(ff_tile, d_ff - j)
        a = _dot(xn, wa_ref[:, j:j + w])
        b = _dot(xn, wb_ref[:, j:j + w])
        g = (a * _sigmoid(a) * b).astype(BF16)
        acc = acc + _dot(g, wo_ref[j:j + w, :])
    hn = h + 0.5 * acc
    if final:
        out_refs[0][...] = _rms(hn, onw_ref[...])
    else:
        out_refs[0][...] = hn
        out_refs[1][...] = _rms(hn, onw_ref[...]).astype(BF16)


def _ffn(h, norm_w, w_in, w_out, next_norm_w, *, final, tm=512, ff_tile=512):
    n, d = h.shape
    d_ff = w_out.shape[0]
    wa = w_in[:, :d_ff].astype(BF16)
    wb = w_in[:, d_ff:].astype(BF16)
    wo = w_out.astype(BF16)
    row = pl.BlockSpec((tm, d), lambda i: (i, 0))
    if final:
        out_shape = jax.ShapeDtypeStruct((n, d), F32)
        out_specs = row
    else:
        out_shape = (jax.ShapeDtypeStruct((n, d), F32), jax.ShapeDtypeStruct((n, d), BF16))
        out_specs = (row, row)
    return pl.pallas_call(
        functools.partial(_ffn_body, ff_tile=ff_tile, final=final),
        out_shape=out_shape,
        grid=(n // tm,),
        in_specs=[row, _resident((1, d)), _resident((d, d_ff)), _resident((d, d_ff)),
                  _resident((d_ff, d)), _resident((1, d))],
        out_specs=out_specs,
        compiler_params=_params("arbitrary"),
        name="ffn_final" if final else "ffn",
    )(h, norm_w.reshape(1, d), wa, wb, wo, next_norm_w.reshape(1, d))


def _proj_call(body, u, w, aux, outs, *, tm, tn, scratch=(), name):
    n, d = u.shape
    cols = w.shape[1]
    in_specs = [pl.BlockSpec((tm, d), lambda i, j: (i, 0)),
                pl.BlockSpec((d, tn), lambda i, j: (0, j))]
    args = [u, w]
    if aux is not None:
        in_specs.append(pl.BlockSpec((aux.shape[0], tn), lambda i, j: (0, j)))
        args.append(aux)
    out_shape = tuple(jax.ShapeDtypeStruct((n, cols), dt) for dt in outs)
    out_specs = tuple(pl.BlockSpec((tm, tn), lambda i, j: (i, j)) for _ in outs)
    return pl.pallas_call(
        body, out_shape=out_shape, grid=(n // tm, cols // tn),
        in_specs=in_specs, out_specs=out_specs, scratch_shapes=list(scratch),
        compiler_params=_params("arbitrary", "arbitrary"), name=name,
    )(*args)


def _proj_hq_body(u_ref, w_ref, o_ref):
    p = _dot(u_ref[...], w_ref[...])
    o_ref[...] = (p * _sigmoid(p) * HEAD_DIM ** -0.5).astype(BF16)


def _proj_hf_body(u_ref, w_ref, logit_ref, lf_ref, k_ref, *, layer):
    p = _dot(u_ref[...], w_ref[...])
    lg = logit_ref[...]
    e = jnp.exp(lg - jnp.max(lg, axis=0, keepdims=True))
    lb = jnp.sum(e[0:layer + 1, :], axis=0, keepdims=True) / jnp.sum(e, axis=0, keepdims=True)
    s = _sigmoid(p)
    lf_ref[...] = jnp.log(lb + (1.0 - lb) * s)
    k_ref[...] = ((1.0 - lb) * (1.0 - s)).astype(BF16)


def _proj_id_body(u_ref, w_ref, o_ref):
    o_ref[...] = _dot(u_ref[...], w_ref[...]).astype(BF16)


def _proj_silu_body(u_ref, w_ref, o_ref):
    p = _dot(u_ref[...], w_ref[...])
    o_ref[...] = (p * _sigmoid(p)).astype(BF16)


def _proj_sigmoid_body(u_ref, w_ref, o_ref):
    o_ref[...] = _sigmoid(_dot(u_ref[...], w_ref[...])).astype(BF16)


def _causal_conv_silu(p, cw_ref, ext_ref, tail_ref, tiles_per_seq):
    i = pl.program_id(0)
    j = pl.program_id(1)
    tm = p.shape[0]

    @pl.when(i % tiles_per_seq == 0)
    def _():
        ext_ref[0:SUBLANES, :] = jnp.zeros((SUBLANES, p.shape[1]), F32)

    @pl.when(i % tiles_per_seq != 0)
    def _():
        ext_ref[0:SUBLANES, :] = tail_ref[j]

    ext_ref[SUBLANES:, :] = p
    tail_ref[j] = p[tm - SUBLANES:, :]
    acc = p * cw_ref[CONV_K - 1:CONV_K, :]
    for d in range(1, CONV_K):
        acc = acc + ext_ref[SUBLANES - d:SUBLANES - d + tm, :] * cw_ref[CONV_K - 1 - d:CONV_K - d, :]
    return acc * _sigmoid(acc)


def _proj_conv_qk_body(u_ref, w_ref, cw_ref, o_ref, ext_ref, tail_ref, *, tiles_per_seq, q_tiles):
    y = _causal_conv_silu(_dot(u_ref[...], w_ref[...]), cw_ref, ext_ref, tail_ref, tiles_per_seq)
    scale = jnp.where(pl.program_id(1) < q_tiles, HEAD_DIM ** -0.5, 1.0).astype(F32)
    for s in range(y.shape[1] // HEAD_DIM):
        ys = y[:, s * HEAD_DIM:(s + 1) * HEAD_DIM]
        inv = lax.rsqrt(jnp.sum(ys * ys, axis=-1, keepdims=True) + EPS) * scale
        o_ref[:, s * HEAD_DIM:(s + 1) * HEAD_DIM] = (ys * inv).astype(BF16)


def _proj_conv_v_body(u_ref, w_ref, cw_ref, o_ref, ext_ref, tail_ref, *, tiles_per_seq):
    y = _causal_conv_silu(_dot(u_ref[...], w_ref[...]), cw_ref, ext_ref, tail_ref, tiles_per_seq)
    o_ref[...] = y.astype(BF16)


def _proj_gab_body(u_ref, w_ref, aux_ref, o_ref):
    p = _dot(u_ref[...], w_ref[...])
    x = p + aux_ref[1:2, :]
    softplus = jnp.maximum(x, 0.0) + jnp.log1p(jnp.exp(-jnp.abs(x)))
    g = -jnp.exp(aux_ref[0:1, :]) * softplus
    lane = lax.broadcasted_iota(jnp.int32, p.shape, 1)
    res = jnp.where(lane % SUBLANES < 2, g, _sigmoid(p))
    o_ref[...] = res.T


def _proj_gab(u, w, aux, *, tm):
    n, d = u.shape
    return pl.pallas_call(
        _proj_gab_body, out_shape=jax.ShapeDtypeStruct((LANES, n), F32), grid=(n // tm,),
        in_specs=[pl.BlockSpec((tm, d), lambda i: (i, 0)), _resident((d, LANES)), _resident((2, LANES))],
        out_specs=pl.BlockSpec((LANES, tm), lambda i: (0, i)),
        compiler_params=_params("arbitrary"), name="proj_gab",
    )(u, w, aux)


def _hgrn_body(q_ref, lf_ref, k_ref, v_ref, gate_ref, nw_ref, tri_ref, sel_ref, o_ref, st_ref):
    @pl.when(pl.program_id(2) == 0)
    def _():
        st_ref[...] = jnp.zeros_like(st_ref)

    ts = q_ref.shape[0]
    b_all = jnp.dot(tri_ref[...], lf_ref[...], precision=HIGHEST, preferred_element_type=F32)
    row16 = lax.broadcasted_iota(jnp.int32, (SUB, HEAD_DIM), 0)
    t_idx = lax.broadcasted_iota(jnp.int32, (CHUNK, CHUNK), 0)
    s_idx = lax.broadcasted_iota(jnp.int32, (CHUNK, CHUNK), 1)
    same_sub = (t_idx // SUB) == (s_idx // SUB)
    nsub = CHUNK // SUB
    for c in range(ts // CHUNK):
        r0 = c * CHUNK
        b = b_all[r0:r0 + CHUNK]
        q = q_ref[r0:r0 + CHUNK, :].astype(F32)
        k = k_ref[r0:r0 + CHUNK, :].astype(F32)
        v = v_ref[r0:r0 + CHUNK, :]
        b_end = b[CHUNK - 1:CHUNK, :]
        st = st_ref[...]
        inter = _dot_nt((q * jnp.exp(b)).astype(BF16), st.astype(BF16))

        off_rows = [jnp.zeros((SUB, CHUNK), F32)]
        for bi in range(1, nsub):
            lo = bi * SUB
            b_ref = b[lo - 1:lo, :]
            q_i = (q[lo:lo + SUB] * jnp.exp(b[lo:lo + SUB] - b_ref)).astype(BF16)
            k_i = (k * jnp.exp(jnp.minimum(b_ref - b, 0.0))).astype(BF16)
            off_rows.append(_dot_nt(q_i, k_i))
        off = jnp.concatenate(off_rows, axis=0)

        diag_rows = []
        for bi in range(nsub):
            lo = bi * SUB
            q_i, k_i, b_i = q[lo:lo + SUB], k[lo:lo + SUB], b[lo:lo + SUB]
            cols = []
            for j in range(SUB):
                dec = jnp.exp(jnp.where(row16 >= j, b_i - b_i[j:j + 1, :], NEG_BIG))
                cols.append((q_i * k_i[j:j + 1, :] * dec).astype(BF16))
            diag_rows.append(jnp.concatenate(cols, axis=1))
        diag = _dot(jnp.concatenate(diag_rows, axis=0), sel_ref[...])

        scores = jnp.where(same_sub, diag, jnp.where(s_idx < t_idx, off, 0.0))
        o = inter + _dot(scores.astype(BF16), v)

        k_end = (k * jnp.exp(b_end - b)).astype(BF16)
        st_ref[...] = st * jnp.exp(b_end) + _dot_tn(v, k_end)

        o_ref[r0:r0 + CHUNK, :] = (_rms(o, nw_ref[...]) * gate_ref[r0:r0 + CHUNK, :].astype(F32)).astype(BF16)


def _chunk_tri(ts, upper):
    r = jnp.arange(ts)[:, None]
    c = jnp.arange(ts)[None, :]
    keep = (r // CHUNK == c // CHUNK) & ((r <= c) if upper else (r >= c))
    return keep.astype(F32)


def _hgrn(q, lf, k, v, gates, gate_col0, norm_w, *, batch, seq, ts=256):
    n = q.shape[0]
    nt = seq // ts
    blk = lambda col0: pl.BlockSpec((ts, HEAD_DIM), lambda b, h, t: (b * nt + t, h + col0))
    r = jnp.arange(SUB * HEAD_DIM)[:, None] // HEAD_DIM
    c = jnp.arange(CHUNK)[None, :] % SUB
    sel = (r == c).astype(BF16)
    return pl.pallas_call(
        _hgrn_body, out_shape=jax.ShapeDtypeStruct((n, HG_HEADS * HEAD_DIM), BF16),
        grid=(batch, HG_HEADS, nt),
        in_specs=[blk(0), blk(0), blk(0), blk(0), blk(gate_col0), _resident((1, HEAD_DIM)),
                  _resident((ts, ts)), _resident((SUB * HEAD_DIM, CHUNK))],
        out_specs=blk(0),
        scratch_shapes=[pltpu.VMEM((HEAD_DIM, HEAD_DIM), F32)],
        compiler_params=_params("arbitrary", "arbitrary", "arbitrary"), name="hgrn2",
    )(q, lf, k, v, gates, norm_w.reshape(1, HEAD_DIM), _chunk_tri(ts, upper=False), sel)


def _unit_lower_inverse(a):
    n = a.shape[0]
    eye = (lax.broadcasted_iota(jnp.int32, (n, n), 0) == lax.broadcasted_iota(jnp.int32, (n, n), 1)).astype(F32)
    hp = lambda x, y: jnp.dot(x, y, precision=HIGHEST, preferred_element_type=F32)
    inv = eye - a
    power = a
    span = 2
    while span < n:
        power = hp(power, power)
        inv = inv + hp(inv, power)
        span *= 2
    return inv


def _gdn_body(q_ref, k_ref, v_ref, z_ref, gb_ref, nw_ref, tri_ref, o_ref, s_ref):
    @pl.when(pl.program_id(2) == 0)
    def _():
        s_ref[...] = jnp.zeros_like(s_ref)

    ts = q_ref.shape[0]
    gb = gb_ref[...]
    gam_rows = jnp.dot(gb, tri_ref[...], precision=HIGHEST, preferred_element_type=F32)
    row8 = lax.broadcasted_iota(jnp.int32, gb.shape, 0)
    cols = jnp.where(row8 < 2, gam_rows, gb).T
    t_idx = lax.broadcasted_iota(jnp.int32, (CHUNK, CHUNK), 0)
    s_idx = lax.broadcasted_iota(jnp.int32, (CHUNK, CHUNK), 1)
    for c in range(ts // CHUNK):
        r0 = c * CHUNK
        q = q_ref[r0:r0 + CHUNK, :]
        k = k_ref[r0:r0 + CHUNK, :]
        kf = k.astype(F32)
        kk = _dot_nt(k, k)
        qk = _dot_nt(q, k)
        for i in range(2):
            gam_r = gam_rows[i:i + 1, r0:r0 + CHUNK]
            gam_c = cols[r0:r0 + CHUNK, i:i + 1]
            beta = cols[r0:r0 + CHUNK, 2 + i:3 + i]
            g_end = gam_c[CHUNK - 1:CHUNK, :]
            lmat = jnp.exp(jnp.where(t_idx >= s_idx, gam_c - gam_r, NEG_BIG))
            a = jnp.where(t_idx > s_idx, beta * kk * lmat, 0.0)
            t_inv = _unit_lower_inverse(a).astype(BF16)
            v = v_ref[r0:r0 + CHUNK, i * HEAD_DIM:(i + 1) * HEAD_DIM].astype(F32)
            rhs = jnp.concatenate([(v * beta).astype(BF16), (kf * (beta * jnp.exp(gam_c))).astype(BF16)], axis=1)
            uw = _dot(t_inv, rhs)
            u = uw[:, :HEAD_DIM]
            w = uw[:, HEAD_DIM:]
            qg = (q.astype(F32) * jnp.exp(gam_c)).astype(BF16)
            state = s_ref[i]
            ws = _dot(jnp.concatenate([w.astype(BF16), qg], axis=0), state.astype(BF16))
            v_new = (u - ws[:CHUNK]).astype(BF16)
            k_end = (kf * jnp.exp(g_end - gam_c)).astype(BF16)
            o = ws[CHUNK:] + _dot((qk * lmat).astype(BF16), v_new)
            s_ref[i] = state * jnp.exp(g_end) + _dot_tn(k_end, v_new)
            z = z_ref[r0:r0 + CHUNK, i * HEAD_DIM:(i + 1) * HEAD_DIM].astype(F32)
            o_ref[r0:r0 + CHUNK, i * HEAD_DIM:(i + 1) * HEAD_DIM] = (_rms(o, nw_ref[...]) * z).astype(BF16)


def _gdn(qk, v, gates, gate_col0, gab_t, norm_w, *, batch, seq, ts=256):
    n = qk.shape[0]
    nt = seq // ts
    rep = GDN_V_HEADS // GDN_QK_HEADS
    wide = rep * HEAD_DIM
    head = lambda col0: pl.BlockSpec((ts, HEAD_DIM), lambda b, h, t: (b * nt + t, h + col0))
    pair = lambda col0: pl.BlockSpec((ts, wide), lambda b, h, t: (b * nt + t, h + col0))
    return pl.pallas_call(
        _gdn_body, out_shape=jax.ShapeDtypeStruct((n, GDN_V_HEADS * HEAD_DIM), BF16),
        grid=(batch, GDN_QK_HEADS, nt),
        in_specs=[head(0), head(GDN_QK_HEADS), pair(0), pair(gate_col0),
                  pl.BlockSpec((SUBLANES, ts), lambda b, h, t: (h, b * nt + t)),
                  _resident((1, HEAD_DIM)), _resident((ts, ts))],
        out_specs=pair(0),
        scratch_shapes=[pltpu.VMEM((rep, HEAD_DIM, HEAD_DIM), F32)],
        compiler_params=_params("arbitrary", "arbitrary", "arbitrary"), name="gated_delta",
    )(qk, qk, v, gates, gab_t, norm_w.reshape(1, HEAD_DIM), _chunk_tri(ts, upper=True))


def _merge_body(oh_ref, og_ref, gate_ref, h_ref, wbh_ref, wbg_ref, wo_ref, o_ref):
    d = h_ref.shape[1]
    yh = _dot(oh_ref[...], wbh_ref[...])
    yg = _dot(og_ref[...], wbg_ref[...])
    y = gate_ref[:, :d].astype(F32) * yh + gate_ref[:, d:].astype(F32) * yg
    o_ref[...] = h_ref[...] + _dot(y.astype(BF16), wo_ref[...])


def _merge(o_h, o_g, merge_gates, h, wbh, wbg, wo, *, tm=512):
    n, d = h.shape
    row = lambda width: pl.BlockSpec((tm, width), lambda i: (i, 0))
    return pl.pallas_call(
        _merge_body, out_shape=jax.ShapeDtypeStruct((n, d), F32), grid=(n // tm,),
        in_specs=[row(o_h.shape[1]), row(o_g.shape[1]), row(2 * d), row(d),
                  _resident(wbh.shape), _resident(wbg.shape), _resident(wo.shape)],
        out_specs=row(d),
        compiler_params=_params("arbitrary"), name="merge",
    )(o_h, o_g, merge_gates, h, wbh.astype(BF16), wbg.astype(BF16), wo.astype(BF16))


def _mixer(u, h, w_in, lb_logits, hgrn_norm, conv_w, a_log, dt_bias, gdn_norm, wbh, wbg, wo, *, layer, batch, seq):
    n, d = h.shape
    hk = HG_HEADS * HEAD_DIM
    gk = GDN_QK_HEADS * HEAD_DIM
    gv = GDN_V_HEADS * HEAD_DIM
    sizes = (hk, hk, hk, hk, gk, gk, gv, GDN_V_HEADS, GDN_V_HEADS, gv, d, d)
    offs = [0]
    for s in sizes:
        offs.append(offs[-1] + s)
    col = lambda a, b: w_in[:, offs[a]:offs[b]].astype(BF16)
    tm, tn = 1024, 512
    tiles_per_seq = seq // tm

    (q_h,) = _proj_call(_proj_hq_body, u, col(0, 1), None, (BF16,), tm=tm, tn=tn, name="proj_hq")
    lf, k_h = _proj_call(functools.partial(_proj_hf_body, layer=layer), u, col(1, 2), lb_logits.astype(F32),
                         (F32, BF16), tm=tm, tn=tn, name="proj_hf")
    (v_h,) = _proj_call(_proj_id_body, u, col(2, 3), None, (BF16,), tm=tm, tn=tn, name="proj_hi")
    w_silu = jnp.concatenate([col(3, 4), col(9, 10)], axis=1)
    (act_gates,) = _proj_call(_proj_silu_body, u, w_silu, None, (BF16,), tm=tm, tn=tn, name="proj_silu")
    (merge_gates,) = _proj_call(_proj_sigmoid_body, u, col(10, 12), None, (BF16,), tm=tm, tn=tn,
                                name="proj_merge_gates")

    conv_pad = jnp.zeros((SUBLANES - CONV_K, conv_w.shape[1]), F32)
    cw = jnp.concatenate([conv_w, conv_pad], axis=0)
    conv_scratch = lambda cols: [pltpu.VMEM((tm + SUBLANES, tn), F32), pltpu.VMEM((cols // tn, SUBLANES, tn), F32)]
    (qk_g,) = _proj_call(
        functools.partial(_proj_conv_qk_body, tiles_per_seq=tiles_per_seq, q_tiles=gk // tn),
        u, col(4, 6), cw[:, :2 * gk], (BF16,), tm=tm, tn=tn, scratch=conv_scratch(2 * gk), name="proj_conv_qk")
    (v_g,) = _proj_call(
        functools.partial(_proj_conv_v_body, tiles_per_seq=tiles_per_seq),
        u, col(6, 7), cw[:, 2 * gk:], (BF16,), tm=tm, tn=tn, scratch=conv_scratch(gv), name="proj_conv_v")

    rep = GDN_V_HEADS // GDN_QK_HEADS
    pad = SUBLANES - 2 * rep
    regroup = lambda a, b, fill: jnp.concatenate(
        [a.reshape(-1, GDN_QK_HEADS, rep), b.reshape(-1, GDN_QK_HEADS, rep),
         jnp.full((a.shape[0], GDN_QK_HEADS, pad), fill, a.dtype)], axis=2).reshape(a.shape[0], -1)
    w_ga = w_in[:, offs[7]:offs[8]]
    w_gb = w_in[:, offs[8]:offs[9]]
    w_gab = regroup(w_ga, w_gb, 0.0)
    w_gab = jnp.pad(w_gab, ((0, 0), (0, LANES - w_gab.shape[1]))).astype(BF16)
    zeros = jnp.zeros((1, GDN_V_HEADS), F32)
    aux = jnp.concatenate([regroup(a_log.reshape(1, -1).astype(F32), zeros, 0.0),
                           regroup(dt_bias.reshape(1, -1).astype(F32), zeros, 0.0)], axis=0)
    aux = jnp.pad(aux, ((0, 0), (0, LANES - aux.shape[1])))
    gab_t = _proj_gab(u, w_gab, aux, tm=tm)

    o_h = _hgrn(q_h, lf, k_h, v_h, act_gates, 0, hgrn_norm, batch=batch, seq=seq)
    o_g = _gdn(qk_g, v_g, act_gates, hk // (rep * HEAD_DIM), gab_t, gdn_norm, batch=batch, seq=seq)
    return _merge(o_h, o_g, merge_gates, h, wbh, wbg, wo)


def kernel(x, ffn1_norm, ffn1_w_in, ffn1_w_out, mix_norm, w_in, hgrn_lb_logits, hgrn_out_norm, gdn_conv_w,
           gdn_a_log, gdn_dt_bias, gdn_out_norm, w_branch_hgrn, w_branch_gdn, w_out, ffn2_norm, ffn2_w_in,
           ffn2_w_out, final_norm):
    batch, seq, d = x.shape
    depth = ffn1_norm.shape[0]
    h = x.reshape(batch * seq, d)
    for l in range(depth):
        h, u = _ffn(h, ffn1_norm[l], ffn1_w_in[l], ffn1_w_out[l], mix_norm[l], final=False)
        h = _mixer(u, h, w_in[l], hgrn_lb_logits, hgrn_out_norm[l], gdn_conv_w[l], gdn_a_log[l],
                   gdn_dt_bias[l], gdn_out_norm[l], w_branch_hgrn[l], w_branch_gdn[l], w_out[l],
                   layer=l, batch=batch, seq=seq)
        last = l == depth - 1
        nxt = final_norm if last else ffn1_norm[l + 1]
        out = _ffn(h, ffn2_norm[l], ffn2_w_in[l], ffn2_w_out[l], nxt, final=last)
        h = out if last else out[0]
    return h.reshape(batch, seq, d)
```

```python
import functools

import jax
import jax.numpy as jnp
from jax import lax
from jax.experimental import pallas as pl
from jax.experimental.pallas import tpu as pltpu

F32 = jnp.float32
BF16 = jnp.bfloat16
HIGHEST = lax.Precision.HIGHEST

EPS = 1e-6
CHUNK = 64
SUB = 16
HEAD_DIM = 128
HG_HEADS = 8
GDN_QK_HEADS = 8
GDN_V_HEADS = 16
CONV_K = 4
LANES = 128
SUBLANES = 8
NEG_BIG = -1e30
VMEM_LIMIT = 56 * 1024 * 1024


def _dot(a, b):
    return jnp.dot(a, b, preferred_element_type=F32)


def _dot_nt(a, b):
    return lax.dot_general(a, b, (((1,), (1,)), ((), ())), preferred_element_type=F32)


def _dot_tn(a, b):
    return lax.dot_general(a, b, (((0,), (0,)), ((), ())), preferred_element_type=F32)


def _sigmoid(x):
    return 1.0 / (1.0 + jnp.exp(-x))


def _rms(x, w):
    ms = jnp.mean(x * x, axis=-1, keepdims=True)
    return x * lax.rsqrt(ms + EPS) * w


def _params(*sem):
    return pltpu.CompilerParams(dimension_semantics=sem, vmem_limit_bytes=VMEM_LIMIT)


def _resident(shape):
    return pl.BlockSpec(shape, lambda *_: (0,) * len(shape), pipeline_mode=pl.Buffered(1))


def _ffn_body(h_ref, nw_ref, wa_ref, wb_ref, wo_ref, onw_ref, *out_refs, ff_tile, final):
    h = h_ref[...]
    xn = _rms(h, nw_ref[...]).astype(BF16)
    d_ff = wa_ref.shape[1]
    acc = jnp.zeros(h.shape, F32)
    for j in range(0, d_ff, ff_tile):
        w = min(ff_tile, d_ff - j)
        a = _dot(xn, wa_ref[:, j:j + w])
        b = _dot(xn, wb_ref[:, j:j + w])
        g = (a * _sigmoid(a) * b).astype(BF16)
        acc = acc + _dot(g, wo_ref[j:j + w, :])
    hn = h + 0.5 * acc
    if final:
        out_refs[0][...] = _rms(hn, onw_ref[...])
    else:
        out_refs[0][...] = hn
        out_refs[1][...] = _rms(hn, onw_ref[...]).astype(BF16)


def _ffn(h, norm_w, w_in, w_out, next_norm_w, *, final, tm=512, ff_tile=512):
    n, d = h.shape
    d_ff = w_out.shape[0]
    wa = w_in[:, :d_ff].astype(BF16)
    wb = w_in[:, d_ff:].astype(BF16)
    wo = w_out.astype(BF16)
    row = pl.BlockSpec((tm, d), lambda i: (i, 0))
    if final:
        out_shape = jax.ShapeDtypeStruct((n, d), F32)
        out_specs = row
    else:
        out_shape = (jax.ShapeDtypeStruct((n, d), F32), jax.ShapeDtypeStruct((n, d), BF16))
        out_specs = (row, row)
    return pl.pallas_call(
        functools.partial(_ffn_body, ff_tile=ff_tile, final=final),
        out_shape=out_shape,
        grid=(n // tm,),
        in_specs=[row, _resident((1, d)), _resident((d, d_ff)), _resident((d, d_ff)),
                  _resident((d_ff, d)), _resident((1, d))],
        out_specs=out_specs,
        compiler_params=_params("arbitrary"),
        name="ffn_final" if final else "ffn",
    )(h, norm_w.reshape(1, d), wa, wb, wo, next_norm_w.reshape(1, d))


def _proj_call(body, u, w, aux, outs, *, tm, tn, scratch=(), name):
    n, d = u.shape
    cols = w.shape[1]
    in_specs = [pl.BlockSpec((tm, d), lambda i, j: (i, 0)),
                pl.BlockSpec((d, tn), lambda i, j: (0, j))]
    args = [u, w]
    if aux is not None:
        in_specs.append(pl.BlockSpec((aux.shape[0], tn), lambda i, j: (0, j)))
        args.append(aux)
    out_shape = tuple(jax.ShapeDtypeStruct((n, cols), dt) for dt in outs)
    out_specs = tuple(pl.BlockSpec((tm, tn), lambda i, j: (i, j)) for _ in outs)
    return pl.pallas_call(
        body, out_shape=out_shape, grid=(n // tm, cols // tn),
        in_specs=in_specs, out_specs=out_specs, scratch_shapes=list(scratch),
        compiler_params=_params("arbitrary", "arbitrary"), name=name,
    )(*args)


def _proj_hq_body(u_ref, w_ref, o_ref):
    p = _dot(u_ref[...], w_ref[...])
    o_ref[...] = (p * _sigmoid(p) * HEAD_DIM ** -0.5).astype(BF16)


def _proj_hf_body(u_ref, w_ref, logit_ref, lf_ref, k_ref, *, layer):
    p = _dot(u_ref[...], w_ref[...])
    lg = logit_ref[...]
    e = jnp.exp(lg - jnp.max(lg, axis=0, keepdims=True))
    lb = jnp.sum(e[0:layer + 1, :], axis=0, keepdims=True) / jnp.sum(e, axis=0, keepdims=True)
    s = _sigmoid(p)
    lf_ref[...] = jnp.log(lb + (1.0 - lb) * s)
    k_ref[...] = ((1.0 - lb) * (1.0 - s)).astype(BF16)


def _proj_id_body(u_ref, w_ref, o_ref):
    o_ref[...] = _dot(u_ref[...], w_ref[...]).astype(BF16)


def _proj_silu_body(u_ref, w_ref, o_ref):
    p = _dot(u_ref[...], w_ref[...])
    o_ref[...] = (p * _sigmoid(p)).astype(BF16)


def _proj_sigmoid_body(u_ref, w_ref, o_ref):
    o_ref[...] = _sigmoid(_dot(u_ref[...], w_ref[...])).astype(BF16)


def _causal_conv_silu(p, cw_ref, ext_ref, tail_ref, tiles_per_seq):
    i = pl.program_id(0)
    j = pl.program_id(1)
    tm = p.shape[0]

    @pl.when(i % tiles_per_seq == 0)
    def _():
        ext_ref[0:SUBLANES, :] = jnp.zeros((SUBLANES, p.shape[1]), F32)

    @pl.when(i % tiles_per_seq != 0)
    def _():
        ext_ref[0:SUBLANES, :] = tail_ref[j]

    ext_ref[SUBLANES:, :] = p
    tail_ref[j] = p[tm - SUBLANES:, :]
    acc = p * cw_ref[CONV_K - 1:CONV_K, :]
    for d in range(1, CONV_K):
        acc = acc + ext_ref[SUBLANES - d:SUBLANES - d + tm, :] * cw_ref[CONV_K - 1 - d:CONV_K - d, :]
    return acc * _sigmoid(acc)


def _proj_conv_qk_body(u_ref, w_ref, cw_ref, o_ref, ext_ref, tail_ref, *, tiles_per_seq, q_tiles):
    y = _causal_conv_silu(_dot(u_ref[...], w_ref[...]), cw_ref, ext_ref, tail_ref, tiles_per_seq)
    scale = jnp.where(pl.program_id(1) < q_tiles, HEAD_DIM ** -0.5, 1.0).astype(F32)
    for s in range(y.shape[1] // HEAD_DIM):
        ys = y[:, s * HEAD_DIM:(s + 1) * HEAD_DIM]
        inv = lax.rsqrt(jnp.sum(ys * ys, axis=-1, keepdims=True) + EPS) * scale
        o_ref[:, s * HEAD_DIM:(s + 1) * HEAD_DIM] = (ys * inv).astype(BF16)


def _proj_conv_v_body(u_ref, w_ref, cw_ref, o_ref, ext_ref, tail_ref, *, tiles_per_seq):
    y = _causal_conv_silu(_dot(u_ref[...], w_ref[...]), cw_ref, ext_ref, tail_ref, tiles_per_seq)
    o_ref[...] = y.astype(BF16)


def _proj_gab_body(u_ref, w_ref, aux_ref, o_ref):
    p = _dot(u_ref[...], w_ref[...])
    x = p + aux_ref[1:2, :]
    softplus = jnp.maximum(x, 0.0) + jnp.log1p(jnp.exp(-jnp.abs(x)))
    g = -jnp.exp(aux_ref[0:1, :]) * softplus
    lane = lax.broadcasted_iota(jnp.int32, p.shape, 1)
    res = jnp.where(lane % SUBLANES < 2, g, _sigmoid(p))
    o_ref[...] = res.T


def _proj_gab(u, w, aux, *, tm):
    n, d = u.shape
    return pl.pallas_call(
        _proj_gab_body, out_shape=jax.ShapeDtypeStruct((LANES, n), F32), grid=(n // tm,),
        in_specs=[pl.BlockSpec((tm, d), lambda i: (i, 0)), _resident((d, LANES)), _resident((2, LANES))],
        out_specs=pl.BlockSpec((LANES, tm), lambda i: (0, i)),
        compiler_params=_params("arbitrary"), name="proj_gab",
    )(u, w, aux)


def _hgrn_body(q_ref, lf_ref, k_ref, v_ref, gate_ref, nw_ref, tri_ref, sel_ref, o_ref, st_ref):
    @pl.when(pl.program_id(2) == 0)
    def _():
        st_ref[...] = jnp.zeros_like(st_ref)

    ts = q_ref.shape[0]
    b_all = jnp.dot(tri_ref[...], lf_ref[...], precision=HIGHEST, preferred_element_type=F32)
    row16 = lax.broadcasted_iota(jnp.int32, (SUB, HEAD_DIM), 0)
    t_idx = lax.broadcasted_iota(jnp.int32, (CHUNK, CHUNK), 0)
    s_idx = lax.broadcasted_iota(jnp.int32, (CHUNK, CHUNK), 1)
    same_sub = (t_idx // SUB) == (s_idx // SUB)
    nsub = CHUNK // SUB
    for c in range(ts // CHUNK):
        r0 = c * CHUNK
        b = b_all[r0:r0 + CHUNK]
        q = q_ref[r0:r0 + CHUNK, :].astype(F32)
        k = k_ref[r0:r0 + CHUNK, :].astype(F32)
        v = v_ref[r0:r0 + CHUNK, :]
        b_end = b[CHUNK - 1:CHUNK, :]
        st = st_ref[...]
        inter = _dot_nt((q * jnp.exp(b)).astype(BF16), st.astype(BF16))

        off_rows = [jnp.zeros((SUB, CHUNK), F32)]
        for bi in range(1, nsub):
            lo = bi * SUB
            b_ref = b[lo - 1:lo, :]
            q_i = (q[lo:lo + SUB] * jnp.exp(b[lo:lo + SUB] - b_ref)).astype(BF16)
            k_i = (k * jnp.exp(jnp.minimum(b_ref - b, 0.0))).astype(BF16)
            off_rows.append(_dot_nt(q_i, k_i))
        off = jnp.concatenate(off_rows, axis=0)

        diag_rows = []
        for bi in range(nsub):
            lo = bi * SUB
            q_i, k_i, b_i = q[lo:lo + SUB], k[lo:lo + SUB], b[lo:lo + SUB]
            cols = []
            for j in range(SUB):
                dec = jnp.exp(jnp.where(row16 >= j, b_i - b_i[j:j + 1, :], NEG_BIG))
                cols.append((q_i * k_i[j:j + 1, :] * dec).astype(BF16))
            diag_rows.append(jnp.concatenate(cols, axis=1))
        diag = _dot(jnp.concatenate(diag_rows, axis=0), sel_ref[...])

        scores = jnp.where(same_sub, diag, jnp.where(s_idx < t_idx, off, 0.0))
        o = inter + _dot(scores.astype(BF16), v)

        k_end = (k * jnp.exp(b_end - b)).astype(BF16)
        st_ref[...] = st * jnp.exp(b_end) + _dot_tn(v, k_end)

        o_ref[r0:r0 + CHUNK, :] = (_rms(o, nw_ref[...]) * gate_ref[r0:r0 + CHUNK, :].astype(F32)).astype(BF16)


def _chunk_tri(ts, upper):
    r = jnp.arange(ts)[:, None]
    c = jnp.arange(ts)[None, :]
    keep = (r // CHUNK == c // CHUNK) & ((r <= c) if upper else (r >= c))
    return keep.astype(F32)


def _hgrn(q, lf, k, v, gates, gate_col0, norm_w, *, batch, seq, ts=256):
    n = q.shape[0]
    nt = seq // ts
    blk = lambda col0: pl.BlockSpec((ts, HEAD_DIM), lambda b, h, t: (b * nt + t, h + col0))
    r = jnp.arange(SUB * HEAD_DIM)[:, None] // HEAD_DIM
    c = jnp.arange(CHUNK)[None, :] % SUB
    sel = (r == c).astype(BF16)
    return pl.pallas_call(
        _hgrn_body, out_shape=jax.ShapeDtypeStruct((n, HG_HEADS * HEAD_DIM), BF16),
        grid=(batch, HG_HEADS, nt),
        in_specs=[blk(0), blk(0), blk(0), blk(0), blk(gate_col0), _resident((1, HEAD_DIM)),
                  _resident((ts, ts)), _resident((SUB * HEAD_DIM, CHUNK))],
        out_specs=blk(0),
        scratch_shapes=[pltpu.VMEM((HEAD_DIM, HEAD_DIM), F32)],
        compiler_params=_params("arbitrary", "arbitrary", "arbitrary"), name="hgrn2",
    )(q, lf, k, v, gates, norm_w.reshape(1, HEAD_DIM), _chunk_tri(ts, upper=False), sel)


def _split_bf16(x):
    hi = x.astype(BF16)
    lo = (x - hi.astype(F32)).astype(BF16)
    return hi, lo


def _matmul_3term(x_dup, y):
    lane = lax.broadcasted_iota(jnp.int32, x_dup.shape, 1)
    xh, xl = _split_bf16(x_dup)
    yh, yl = _split_bf16(y)
    lhs = jnp.concatenate([jnp.where(lane < CHUNK, xh, xl), xh], axis=1)
    rhs = jnp.concatenate([yh, yh, yl, jnp.zeros_like(yh)], axis=0)
    return _dot(lhs, rhs)


def _unit_lower_inverses(a_dups):
    shape = a_dups[0].shape
    row = lax.broadcasted_iota(jnp.int32, shape, 0)
    lane = lax.broadcasted_iota(jnp.int32, shape, 1)
    partial = [jnp.where(lane < CHUNK, jnp.where(row == lane, 1.0, 0.0) - a, 0.0) for a in a_dups]
    power = [_matmul_3term(a, a) for a in a_dups]
    span = 2
    while span < CHUNK:
        last = 2 * span >= CHUNK
        nxt = []
        for p in range(len(a_dups)):
            rhs = partial[p] if last else jnp.concatenate([power[p], partial[p]], axis=1)
            res = _matmul_3term(power[p], rhs)
            if last:
                partial[p] = partial[p] + res
            else:
                nxt.append(res[:, :2 * CHUNK])
                partial[p] = partial[p] + res[:, 2 * CHUNK:]
        power = nxt
        span *= 2
    return partial


def _gdn_body(q_ref, k_ref, v_ref, z_ref, gb_ref, nw_ref, tri_ref, o_ref, s_ref):
    @pl.when(pl.program_id(2) == 0)
    def _():
        s_ref[...] = jnp.zeros_like(s_ref)

    ts = q_ref.shape[0]
    n_chunks = ts // CHUNK
    rep = s_ref.shape[0]
    dup = 2 * CHUNK
    gb = gb_ref[...]
    gam_rows = jnp.dot(gb, tri_ref[...], precision=HIGHEST, preferred_element_type=F32)
    gam_plain = jnp.concatenate([gam_rows[:, c * dup:c * dup + CHUNK] for c in range(n_chunks)], axis=1)
    row8 = lax.broadcasted_iota(jnp.int32, gb.shape, 0)
    cols = jnp.where(row8 < rep, gam_plain, gb).T
    t_dup = lax.broadcasted_iota(jnp.int32, (CHUNK, dup), 0)
    s_dup = lax.broadcasted_iota(jnp.int32, (CHUNK, dup), 1) % CHUNK
    s_idx = lax.broadcasted_iota(jnp.int32, (CHUNK, CHUNK), 0)
    t_idx = lax.broadcasted_iota(jnp.int32, (CHUNK, CHUNK), 1)

    probs = [(c, i) for c in range(n_chunks) for i in range(rep)]
    a_dups, pre = [], {}
    for c in range(n_chunks):
        r0 = c * CHUNK
        q = q_ref[r0:r0 + CHUNK, :]
        k = k_ref[r0:r0 + CHUNK, :]
        qf = q.astype(F32)
        kf = k.astype(F32)
        kk_dup = _dot_nt(k, jnp.concatenate([k, k], axis=0))
        kq = _dot_nt(k, q)
        for i in range(rep):
            gam_r_dup = gam_rows[i:i + 1, c * dup:(c + 1) * dup]
            gam_r = gam_r_dup[:, :CHUNK]
            gam_c = cols[r0:r0 + CHUNK, i:i + 1]
            beta = cols[r0:r0 + CHUNK, rep + i:rep + i + 1]
            g_end = gam_c[CHUNK - 1:CHUNK, :]
            lmat_dup = jnp.exp(jnp.where(t_dup >= s_dup, gam_c - gam_r_dup, NEG_BIG))
            a_dups.append(jnp.where(t_dup > s_dup, beta * kk_dup * lmat_dup, 0.0))
            lmat_t = jnp.exp(jnp.where(t_idx >= s_idx, gam_r - gam_c, NEG_BIG))
            e_gam = jnp.exp(gam_c)
            v = v_ref[r0:r0 + CHUNK, i * HEAD_DIM:(i + 1) * HEAD_DIM].astype(F32)
            rhs = jnp.concatenate([(v * beta).astype(BF16), (kf * (beta * e_gam)).astype(BF16)], axis=1)
            k_end = kf * jnp.exp(g_end - gam_c)
            x_t = jnp.concatenate([k_end, kq * lmat_t], axis=1)
            pre[(c, i)] = (rhs, x_t, qf * e_gam, jnp.exp(g_end))

    inverses = _unit_lower_inverses(a_dups)

    chain = {}
    for p, (c, i) in enumerate(probs):
        rhs, x_t, qg, decay = pre[(c, i)]
        t_inv = inverses[p][:, :CHUNK].astype(BF16)
        uw = _dot(t_inv, rhs).astype(BF16)
        r = _dot(x_t.T.astype(BF16), uw)
        ktu = r[:HEAD_DIM, :HEAD_DIM]
        ktw = r[:HEAD_DIM, HEAD_DIM:]
        qku = r[HEAD_DIM:, :HEAD_DIM]
        qkw = r[HEAD_DIM:, HEAD_DIM:]
        lhs = jnp.concatenate([ktw, qg - qkw], axis=0).astype(BF16)
        chain[(c, i)] = (lhs, ktu, qku, decay)

    for c in range(n_chunks):
        r0 = c * CHUNK
        for i in range(rep):
            lhs, ktu, qku, decay = chain[(c, i)]
            state = s_ref[i]
            m = _dot(lhs, state.astype(BF16))
            s_ref[i] = state * decay + ktu - m[:HEAD_DIM]
            o = m[HEAD_DIM:] + qku
            z = z_ref[r0:r0 + CHUNK, i * HEAD_DIM:(i + 1) * HEAD_DIM].astype(F32)
            o_ref[r0:r0 + CHUNK, i * HEAD_DIM:(i + 1) * HEAD_DIM] = (_rms(o, nw_ref[...]) * z).astype(BF16)


def _chunk_tri_dup(ts):
    r = jnp.arange(ts)[:, None]
    j = jnp.arange(2 * ts)[None, :]
    c = (j // (2 * CHUNK)) * CHUNK + j % CHUNK
    return ((r // CHUNK == c // CHUNK) & (r <= c)).astype(F32)


def _gdn(qk, v, gates, gate_col0, gab_t, norm_w, *, batch, seq, ts=256):
    n = qk.shape[0]
    nt = seq // ts
    rep = GDN_V_HEADS // GDN_QK_HEADS
    wide = rep * HEAD_DIM
    head = lambda col0: pl.BlockSpec((ts, HEAD_DIM), lambda b, h, t: (b * nt + t, h + col0))
    pair = lambda col0: pl.BlockSpec((ts, wide), lambda b, h, t: (b * nt + t, h + col0))
    return pl.pallas_call(
        _gdn_body, out_shape=jax.ShapeDtypeStruct((n, GDN_V_HEADS * HEAD_DIM), BF16),
        grid=(batch, GDN_QK_HEADS, nt),
        in_specs=[head(0), head(GDN_QK_HEADS), pair(0), pair(gate_col0),
                  pl.BlockSpec((SUBLANES, ts), lambda b, h, t: (h, b * nt + t)),
                  _resident((1, HEAD_DIM)), _resident((ts, 2 * ts))],
        out_specs=pair(0),
        scratch_shapes=[pltpu.VMEM((rep, HEAD_DIM, HEAD_DIM), F32)],
        compiler_params=_params("arbitrary", "arbitrary", "arbitrary"), name="gated_delta",
    )(qk, qk, v, gates, gab_t, norm_w.reshape(1, HEAD_DIM), _chunk_tri_dup(ts))


def _merge_body(oh_ref, og_ref, gate_ref, h_ref, wbh_ref, wbg_ref, wo_ref, o_ref):
    d = h_ref.shape[1]
    yh = _dot(oh_ref[...], wbh_ref[...])
    yg = _dot(og_ref[...], wbg_ref[...])
    y = gate_ref[:, :d].astype(F32) * yh + gate_ref[:, d:].astype(F32) * yg
    o_ref[...] = h_ref[...] + _dot(y.astype(BF16), wo_ref[...])


def _merge(o_h, o_g, merge_gates, h, wbh, wbg, wo, *, tm=512):
    n, d = h.shape
    row = lambda width: pl.BlockSpec((tm, width), lambda i: (i, 0))
    return pl.pallas_call(
        _merge_body, out_shape=jax.ShapeDtypeStruct((n, d), F32), grid=(n // tm,),
        in_specs=[row(o_h.shape[1]), row(o_g.shape[1]), row(2 * d), row(d),
                  _resident(wbh.shape), _resident(wbg.shape), _resident(wo.shape)],
        out_specs=row(d),
        compiler_params=_params("arbitrary"), name="merge",
    )(o_h, o_g, merge_gates, h, wbh.astype(BF16), wbg.astype(BF16), wo.astype(BF16))


def _mixer(u, h, w_in, lb_logits, hgrn_norm, conv_w, a_log, dt_bias, gdn_norm, wbh, wbg, wo, *, layer, batch, seq):
    n, d = h.shape
    hk = HG_HEADS * HEAD_DIM
    gk = GDN_QK_HEADS * HEAD_DIM
    gv = GDN_V_HEADS * HEAD_DIM
    sizes = (hk, hk, hk, hk, gk, gk, gv, GDN_V_HEADS, GDN_V_HEADS, gv, d, d)
    offs = [0]
    for s in sizes:
        offs.append(offs[-1] + s)
    col = lambda a, b: w_in[:, offs[a]:offs[b]].astype(BF16)
    tm, tn = 1024, 512
    tiles_per_seq = seq // tm

    (q_h,) = _proj_call(_proj_hq_body, u, col(0, 1), None, (BF16,), tm=tm, tn=tn, name="proj_hq")
    lf, k_h = _proj_call(functools.partial(_proj_hf_body, layer=layer), u, col(1, 2), lb_logits.astype(F32),
                         (F32, BF16), tm=tm, tn=tn, name="proj_hf")
    (v_h,) = _proj_call(_proj_id_body, u, col(2, 3), None, (BF16,), tm=tm, tn=tn, name="proj_hi")
    w_silu = jnp.concatenate([col(3, 4), col(9, 10)], axis=1)
    (act_gates,) = _proj_call(_proj_silu_body, u, w_silu, None, (BF16,), tm=tm, tn=tn, name="proj_silu")
    (merge_gates,) = _proj_call(_proj_sigmoid_body, u, col(10, 12), None, (BF16,), tm=tm, tn=tn,
                                name="proj_merge_gates")

    conv_pad = jnp.zeros((SUBLANES - CONV_K, conv_w.shape[1]), F32)
    cw = jnp.concatenate([conv_w, conv_pad], axis=0)
    conv_scratch = lambda cols: [pltpu.VMEM((tm + SUBLANES, tn), F32), pltpu.VMEM((cols // tn, SUBLANES, tn), F32)]
    (qk_g,) = _proj_call(
        functools.partial(_proj_conv_qk_body, tiles_per_seq=tiles_per_seq, q_tiles=gk // tn),
        u, col(4, 6), cw[:, :2 * gk], (BF16,), tm=tm, tn=tn, scratch=conv_scratch(2 * gk), name="proj_conv_qk")
    (v_g,) = _proj_call(
        functools.partial(_proj_conv_v_body, tiles_per_seq=tiles_per_seq),
        u, col(6, 7), cw[:, 2 * gk:], (BF16,), tm=tm, tn=tn, scratch=conv_scratch(gv), name="proj_conv_v")

    rep = GDN_V_HEADS // GDN_QK_HEADS
    pad = SUBLANES - 2 * rep
    regroup = lambda a, b, fill: jnp.concatenate(
        [a.reshape(-1, GDN_QK_HEADS, rep), b.reshape(-1, GDN_QK_HEADS, rep),
         jnp.full((a.shape[0], GDN_QK_HEADS, pad), fill, a.dtype)], axis=2).reshape(a.shape[0], -1)
    w_ga = w_in[:, offs[7]:offs[8]]
    w_gb = w_in[:, offs[8]:offs[9]]
    w_gab = regroup(w_ga, w_gb, 0.0)
    w_gab = jnp.pad(w_gab, ((0, 0), (0, LANES - w_gab.shape[1]))).astype(BF16)
    zeros = jnp.zeros((1, GDN_V_HEADS), F32)
    aux = jnp.concatenate([regroup(a_log.reshape(1, -1).astype(F32), zeros, 0.0),
                           regroup(dt_bias.reshape(1, -1).astype(F32), zeros, 0.0)], axis=0)
    aux = jnp.pad(aux, ((0, 0), (0, LANES - aux.shape[1])))
    gab_t = _proj_gab(u, w_gab, aux, tm=tm)

    o_h = _hgrn(q_h, lf, k_h, v_h, act_gates, 0, hgrn_norm, batch=batch, seq=seq)
    o_g = _gdn(qk_g, v_g, act_gates, hk // (rep * HEAD_DIM), gab_t, gdn_norm, batch=batch, seq=seq)
    return _merge(o_h, o_g, merge_gates, h, wbh, wbg, wo)


def kernel(x, ffn1_norm, ffn1_w_in, ffn1_w_out, mix_norm, w_in, hgrn_lb_logits, hgrn_out_norm, gdn_conv_w,
           gdn_a_log, gdn_dt_bias, gdn_out_norm, w_branch_hgrn, w_branch_gdn, w_out, ffn2_norm, ffn2_w_in,
           ffn2_w_out, final_norm):
    batch, seq, d = x.shape
    depth = ffn1_norm.shape[0]
    h = x.reshape(batch * seq, d)
    for l in range(depth):
        h, u = _ffn(h, ffn1_norm[l], ffn1_w_in[l], ffn1_w_out[l], mix_norm[l], final=False)
        h = _mixer(u, h, w_in[l], hgrn_lb_logits, hgrn_out_norm[l], gdn_conv_w[l], gdn_a_log[l],
                   gdn_dt_bias[l], gdn_out_norm[l], w_branch_hgrn[l], w_branch_gdn[l], w_out[l],
                   layer=l, batch=batch, seq=seq)
        last = l == depth - 1
        nxt = final_norm if last else ffn1_norm[l + 1]
        out = _ffn(h, ffn2_norm[l], ffn2_w_in[l], ffn2_w_out[l], nxt, final=last)
        h = out if last else out[0]
    return h.reshape(batch, seq, d)
```

```python
import functools

import jax
import jax.numpy as jnp
from jax import lax
from jax.experimental import pallas as pl
from jax.experimental.pallas import tpu as pltpu

F32 = jnp.float32
BF16 = jnp.bfloat16
HIGHEST = lax.Precision.HIGHEST

EPS = 1e-6
CHUNK = 64
SUB = 8
HEAD_DIM = 128
HG_HEADS = 8
GDN_QK_HEADS = 8
GDN_V_HEADS = 16
CONV_K = 4
LANES = 128
SUBLANES = 8
NEG_BIG = -1e30
VMEM_LIMIT = 56 * 1024 * 1024


def _dot(a, b):
    return jnp.dot(a, b, preferred_element_type=F32)


def _dot_nt(a, b):
    return lax.dot_general(a, b, (((1,), (1,)), ((), ())), preferred_element_type=F32)


def _dot_tn(a, b):
    return lax.dot_general(a, b, (((0,), (0,)), ((), ())), preferred_element_type=F32)


def _sigmoid(x):
    return 1.0 / (1.0 + jnp.exp(-x))


def _rms(x, w):
    ms = jnp.mean(x * x, axis=-1, keepdims=True)
    return x * lax.rsqrt(ms + EPS) * w


def _params(*sem):
    return pltpu.CompilerParams(dimension_semantics=sem, vmem_limit_bytes=VMEM_LIMIT)


def _resident(shape):
    return pl.BlockSpec(shape, lambda *_: (0,) * len(shape), pipeline_mode=pl.Buffered(1))


def _ffn_body(h_ref, nw_ref, wa_ref, wb_ref, wo_ref, onw_ref, *out_refs, ff_tile, final):
    h = h_ref[...]
    xn = _rms(h, nw_ref[...]).astype(BF16)
    d_ff = wa_ref.shape[1]
    acc = jnp.zeros(h.shape, F32)
    for j in range(0, d_ff, ff_tile):
        w = min(ff_tile, d_ff - j)
        a = _dot(xn, wa_ref[:, j:j + w])
        b = _dot(xn, wb_ref[:, j:j + w])
        g = (a * _sigmoid(a) * b).astype(BF16)
        acc = acc + _dot(g, wo_ref[j:j + w, :])
    hn = h + 0.5 * acc
    if final:
        out_refs[0][...] = _rms(hn, onw_ref[...])
    else:
        out_refs[0][...] = hn
        out_refs[1][...] = _rms(hn, onw_ref[...]).astype(BF16)


def _ffn(h, norm_w, w_in, w_out, next_norm_w, *, final, tm=512, ff_tile=512):
    n, d = h.shape
    d_ff = w_out.shape[0]
    wa = w_in[:, :d_ff].astype(BF16)
    wb = w_in[:, d_ff:].astype(BF16)
    wo = w_out.astype(BF16)
    row = pl.BlockSpec((tm, d), lambda i: (i, 0))
    if final:
        out_shape = jax.ShapeDtypeStruct((n, d), F32)
        out_specs = row
    else:
        out_shape = (jax.ShapeDtypeStruct((n, d), F32), jax.ShapeDtypeStruct((n, d), BF16))
        out_specs = (row, row)
    return pl.pallas_call(
        functools.partial(_ffn_body, ff_tile=ff_tile, final=final),
        out_shape=out_shape,
        grid=(n // tm,),
        in_specs=[row, _resident((1, d)), _resident((d, d_ff)), _resident((d, d_ff)),
                  _resident((d_ff, d)), _resident((1, d))],
        out_specs=out_specs,
        compiler_params=_params("arbitrary"),
        name="ffn_final" if final else "ffn",
    )(h, norm_w.reshape(1, d), wa, wb, wo, next_norm_w.reshape(1, d))


def _proj_call(body, u, w, aux, outs, *, tm, tn, scratch=(), name):
    n, d = u.shape
    cols = w.shape[1]
    in_specs = [pl.BlockSpec((tm, d), lambda i, j: (i, 0)),
                pl.BlockSpec((d, tn), lambda i, j: (0, j))]
    args = [u, w]
    if aux is not None:
        in_specs.append(pl.BlockSpec((aux.shape[0], tn), lambda i, j: (0, j)))
        args.append(aux)
    out_shape = tuple(jax.ShapeDtypeStruct((n, cols), dt) for dt in outs)
    out_specs = tuple(pl.BlockSpec((tm, tn), lambda i, j: (i, j)) for _ in outs)
    return pl.pallas_call(
        body, out_shape=out_shape, grid=(n // tm, cols // tn),
        in_specs=in_specs, out_specs=out_specs, scratch_shapes=list(scratch),
        compiler_params=_params("arbitrary", "arbitrary"), name=name,
    )(*args)


def _proj_hq_body(u_ref, w_ref, o_ref):
    p = _dot(u_ref[...], w_ref[...])
    o_ref[...] = (p * _sigmoid(p) * HEAD_DIM ** -0.5).astype(BF16)


def _proj_hf_body(u_ref, w_ref, logit_ref, lf_ref, k_ref, *, layer):
    p = _dot(u_ref[...], w_ref[...])
    lg = logit_ref[...]
    e = jnp.exp(lg - jnp.max(lg, axis=0, keepdims=True))
    lb = jnp.sum(e[0:layer + 1, :], axis=0, keepdims=True) / jnp.sum(e, axis=0, keepdims=True)
    s = _sigmoid(p)
    lf_ref[...] = jnp.log(lb + (1.0 - lb) * s)
    k_ref[...] = ((1.0 - lb) * (1.0 - s)).astype(BF16)


def _proj_id_body(u_ref, w_ref, o_ref):
    o_ref[...] = _dot(u_ref[...], w_ref[...]).astype(BF16)


def _proj_silu_body(u_ref, w_ref, o_ref):
    p = _dot(u_ref[...], w_ref[...])
    o_ref[...] = (p * _sigmoid(p)).astype(BF16)


def _proj_sigmoid_body(u_ref, w_ref, o_ref):
    o_ref[...] = _sigmoid(_dot(u_ref[...], w_ref[...])).astype(BF16)


def _causal_conv_silu(p, cw_ref, ext_ref, tail_ref, tiles_per_seq):
    i = pl.program_id(0)
    j = pl.program_id(1)
    tm = p.shape[0]

    @pl.when(i % tiles_per_seq == 0)
    def _():
        ext_ref[0:SUBLANES, :] = jnp.zeros((SUBLANES, p.shape[1]), F32)

    @pl.when(i % tiles_per_seq != 0)
    def _():
        ext_ref[0:SUBLANES, :] = tail_ref[j]

    ext_ref[SUBLANES:, :] = p
    tail_ref[j] = p[tm - SUBLANES:, :]
    acc = p * cw_ref[CONV_K - 1:CONV_K, :]
    for d in range(1, CONV_K):
        acc = acc + ext_ref[SUBLANES - d:SUBLANES - d + tm, :] * cw_ref[CONV_K - 1 - d:CONV_K - d, :]
    return acc * _sigmoid(acc)


def _proj_conv_qk_body(u_ref, w_ref, cw_ref, o_ref, ext_ref, tail_ref, *, tiles_per_seq, q_tiles):
    y = _causal_conv_silu(_dot(u_ref[...], w_ref[...]), cw_ref, ext_ref, tail_ref, tiles_per_seq)
    scale = jnp.where(pl.program_id(1) < q_tiles, HEAD_DIM ** -0.5, 1.0).astype(F32)
    for s in range(y.shape[1] // HEAD_DIM):
        ys = y[:, s * HEAD_DIM:(s + 1) * HEAD_DIM]
        inv = lax.rsqrt(jnp.sum(ys * ys, axis=-1, keepdims=True) + EPS) * scale
        o_ref[:, s * HEAD_DIM:(s + 1) * HEAD_DIM] = (ys * inv).astype(BF16)


def _proj_conv_v_body(u_ref, w_ref, cw_ref, o_ref, ext_ref, tail_ref, *, tiles_per_seq):
    y = _causal_conv_silu(_dot(u_ref[...], w_ref[...]), cw_ref, ext_ref, tail_ref, tiles_per_seq)
    o_ref[...] = y.astype(BF16)


def _proj_gab_body(u_ref, w_ref, aux_ref, o_ref):
    p = _dot(u_ref[...], w_ref[...])
    x = p + aux_ref[1:2, :]
    softplus = jnp.maximum(x, 0.0) + jnp.log1p(jnp.exp(-jnp.abs(x)))
    g = -jnp.exp(aux_ref[0:1, :]) * softplus
    lane = lax.broadcasted_iota(jnp.int32, p.shape, 1)
    res = jnp.where(lane % SUBLANES < 2, g, _sigmoid(p))
    o_ref[...] = res.T


def _proj_gab(u, w, aux, *, tm):
    n, d = u.shape
    return pl.pallas_call(
        _proj_gab_body, out_shape=jax.ShapeDtypeStruct((LANES, n), F32), grid=(n // tm,),
        in_specs=[pl.BlockSpec((tm, d), lambda i: (i, 0)), _resident((d, LANES)), _resident((2, LANES))],
        out_specs=pl.BlockSpec((LANES, tm), lambda i: (0, i)),
        compiler_params=_params("arbitrary"), name="proj_gab",
    )(u, w, aux)


def _hgrn_body(q_ref, lf_ref, k_ref, v_ref, gate_ref, nw_ref, tri_ref, o_ref, st_ref):
    @pl.when(pl.program_id(2) == 0)
    def _():
        st_ref[...] = jnp.zeros_like(st_ref)

    ts = q_ref.shape[0]
    n_chunks = ts // CHUNK
    lf = lf_ref[...]
    hi = lf.astype(BF16)
    r1 = lf - hi.astype(F32)
    mid = r1.astype(BF16)
    lo = (r1 - mid.astype(F32)).astype(BF16)
    by_chunk = lambda x: jnp.concatenate([x[c * CHUNK:(c + 1) * CHUNK] for c in range(n_chunks)], axis=1)
    b_all = _dot(tri_ref[...], jnp.concatenate([by_chunk(hi), by_chunk(mid), by_chunk(lo)], axis=0))

    row = lax.broadcasted_iota(jnp.int32, (CHUNK, HEAD_DIM), 0)
    t_idx = lax.broadcasted_iota(jnp.int32, (CHUNK, CHUNK), 0)
    s_idx = lax.broadcasted_iota(jnp.int32, (CHUNK, CHUNK), 1)
    levels = []
    half = CHUNK // 2
    while half >= SUB:
        size = 2 * half
        levels.append((half, row % size >= half,
                       (t_idx // size == s_idx // size) & (t_idx % size >= half) & (s_idx % size < half)))
        half //= 2
    causal_in_block = [row % SUB >= j for j in range(SUB)]
    place = [s_idx == (t_idx // SUB) * SUB + j for j in range(SUB)]
    for c in range(n_chunks):
        r0 = c * CHUNK
        b = b_all[:, c * HEAD_DIM:(c + 1) * HEAD_DIM]
        q = q_ref[r0:r0 + CHUNK, :].astype(F32)
        k = k_ref[r0:r0 + CHUNK, :].astype(F32)
        v = v_ref[r0:r0 + CHUNK, :]
        b_end = b[CHUNK - 1:CHUNK, :]
        st = st_ref[...]
        inter = _dot_nt((q * jnp.exp(b)).astype(BF16), st.astype(BF16))

        scores = jnp.zeros((CHUNK, CHUNK), F32)
        for half, past_mid, pair in levels:
            size = 2 * half
            b_mid = jnp.concatenate(
                [jnp.broadcast_to(b[m + half - 1:m + half, :], (size, HEAD_DIM)) for m in range(0, CHUNK, size)], axis=0)
            x = (jnp.where(past_mid, q, k) * jnp.exp(-jnp.abs(b - b_mid))).astype(BF16)
            scores = jnp.where(pair, _dot_nt(x, x), scores)

        blocks = lambda x: x.reshape(CHUNK // SUB, SUB, HEAD_DIM)
        for j in range(SUB):
            pick = lambda x: jnp.broadcast_to(blocks(x)[:, j:j + 1, :], blocks(x).shape).reshape(CHUNK, HEAD_DIM)
            dec = jnp.exp(jnp.where(causal_in_block[j], b - pick(b), NEG_BIG))
            col = jnp.sum(q * pick(k) * dec, axis=-1, keepdims=True)
            scores = jnp.where(place[j], col, scores)

        o = inter + _dot(scores.astype(BF16), v)

        k_end = (k * jnp.exp(b_end - b)).astype(BF16)
        st_ref[...] = st * jnp.exp(b_end) + _dot_tn(v, k_end)

        o_ref[r0:r0 + CHUNK, :] = (_rms(o, nw_ref[...]) * gate_ref[r0:r0 + CHUNK, :].astype(F32)).astype(BF16)


def _hgrn(q, lf, k, v, gates, gate_col0, norm_w, *, batch, seq, ts=512):
    n = q.shape[0]
    nt = seq // ts
    blk = lambda col0: pl.BlockSpec((ts, HEAD_DIM), lambda b, h, t: (b * nt + t, h + col0))
    tri = (jnp.arange(CHUNK)[:, None] >= jnp.arange(CHUNK)[None, :]).astype(BF16)
    tri3 = jnp.concatenate([tri, tri, tri], axis=1)
    return pl.pallas_call(
        _hgrn_body, out_shape=jax.ShapeDtypeStruct((n, HG_HEADS * HEAD_DIM), BF16),
        grid=(batch, HG_HEADS, nt),
        in_specs=[blk(0), blk(0), blk(0), blk(0), blk(gate_col0), _resident((1, HEAD_DIM)),
                  _resident((CHUNK, 3 * CHUNK))],
        out_specs=blk(0),
        scratch_shapes=[pltpu.VMEM((HEAD_DIM, HEAD_DIM), F32)],
        compiler_params=_params("arbitrary", "arbitrary", "arbitrary"), name="hgrn2",
    )(q, lf, k, v, gates, norm_w.reshape(1, HEAD_DIM), tri3)


def _split_bf16(x):
    hi = x.astype(BF16)
    lo = (x - hi.astype(F32)).astype(BF16)
    return hi, lo


def _matmul_3term(x_dup, y):
    lane = lax.broadcasted_iota(jnp.int32, x_dup.shape, 1)
    xh, xl = _split_bf16(x_dup)
    yh, yl = _split_bf16(y)
    lhs = jnp.concatenate([jnp.where(lane < CHUNK, xh, xl), xh], axis=1)
    rhs = jnp.concatenate([yh, yh, yl, jnp.zeros_like(yh)], axis=0)
    return _dot(lhs, rhs)


def _unit_lower_inverses(a_dups):
    shape = a_dups[0].shape
    row = lax.broadcasted_iota(jnp.int32, shape, 0)
    lane = lax.broadcasted_iota(jnp.int32, shape, 1)
    partial = [jnp.where(lane < CHUNK, jnp.where(row == lane, 1.0, 0.0) - a, 0.0) for a in a_dups]
    power = [_matmul_3term(a, a) for a in a_dups]
    span = 2
    while span < CHUNK:
        last = 2 * span >= CHUNK
        nxt = []
        for p in range(len(a_dups)):
            rhs = partial[p] if last else jnp.concatenate([power[p], partial[p]], axis=1)
            res = _matmul_3term(power[p], rhs)
            if last:
                partial[p] = partial[p] + res
            else:
                nxt.append(res[:, :2 * CHUNK])
                partial[p] = partial[p] + res[:, 2 * CHUNK:]
        power = nxt
        span *= 2
    return partial


def _gdn_body(q_ref, k_ref, v_ref, z_ref, gb_ref, nw_ref, tri_ref, o_ref, s_ref):
    @pl.when(pl.program_id(2) == 0)
    def _():
        s_ref[...] = jnp.zeros_like(s_ref)

    ts = q_ref.shape[0]
    n_chunks = ts // CHUNK
    rep = s_ref.shape[0]
    dup = 2 * CHUNK
    gb = gb_ref[...]
    gam_rows = jnp.dot(gb, tri_ref[...], precision=HIGHEST, preferred_element_type=F32)
    gam_plain = jnp.concatenate([gam_rows[:, c * dup:c * dup + CHUNK] for c in range(n_chunks)], axis=1)
    row8 = lax.broadcasted_iota(jnp.int32, gb.shape, 0)
    cols = jnp.where(row8 < rep, gam_plain, gb).T
    t_dup = lax.broadcasted_iota(jnp.int32, (CHUNK, dup), 0)
    s_dup = lax.broadcasted_iota(jnp.int32, (CHUNK, dup), 1) % CHUNK
    s_idx = lax.broadcasted_iota(jnp.int32, (CHUNK, CHUNK), 0)
    t_idx = lax.broadcasted_iota(jnp.int32, (CHUNK, CHUNK), 1)

    probs = [(c, i) for c in range(n_chunks) for i in range(rep)]
    a_dups, pre = [], {}
    for c in range(n_chunks):
        r0 = c * CHUNK
        q = q_ref[r0:r0 + CHUNK, :]
        k = k_ref[r0:r0 + CHUNK, :]
        qf = q.astype(F32)
        kf = k.astype(F32)
        kk_dup = _dot_nt(k, jnp.concatenate([k, k], axis=0))
        kq = _dot_nt(k, q)
        for i in range(rep):
            gam_r_dup = gam_rows[i:i + 1, c * dup:(c + 1) * dup]
            gam_r = gam_r_dup[:, :CHUNK]
            gam_c = cols[r0:r0 + CHUNK, i:i + 1]
            beta = cols[r0:r0 + CHUNK, rep + i:rep + i + 1]
            g_end = gam_c[CHUNK - 1:CHUNK, :]
            lmat_dup = jnp.exp(jnp.where(t_dup >= s_dup, gam_c - gam_r_dup, NEG_BIG))
            a_dups.append(jnp.where(t_dup > s_dup, beta * kk_dup * lmat_dup, 0.0))
            lmat_t = jnp.exp(jnp.where(t_idx >= s_idx, gam_r - gam_c, NEG_BIG))
            e_gam = jnp.exp(gam_c)
            v = v_ref[r0:r0 + CHUNK, i * HEAD_DIM:(i + 1) * HEAD_DIM].astype(F32)
            rhs = jnp.concatenate([(v * beta).astype(BF16), (kf * (beta * e_gam)).astype(BF16)], axis=1)
            k_end = kf * jnp.exp(g_end - gam_c)
            x_t = jnp.concatenate([k_end, kq * lmat_t], axis=1)
            pre[(c, i)] = (rhs, x_t, qf * e_gam, jnp.exp(g_end))

    inverses = _unit_lower_inverses(a_dups)

    chain = {}
    for p, (c, i) in enumerate(probs):
        rhs, x_t, qg, decay = pre[(c, i)]
        t_inv = inverses[p][:, :CHUNK].astype(BF16)
        uw = _dot(t_inv, rhs).astype(BF16)
        r = _dot(x_t.T.astype(BF16), uw)
        ktu = r[:HEAD_DIM, :HEAD_DIM]
        ktw = r[:HEAD_DIM, HEAD_DIM:]
        qku = r[HEAD_DIM:, :HEAD_DIM]
        qkw = r[HEAD_DIM:, HEAD_DIM:]
        lhs = jnp.concatenate([ktw, qg - qkw], axis=0).astype(BF16)
        chain[(c, i)] = (lhs, ktu, qku, decay)

    for c in range(n_chunks):
        r0 = c * CHUNK
        for i in range(rep):
            lhs, ktu, qku, decay = chain[(c, i)]
            state = s_ref[i]
            m = _dot(lhs, state.astype(BF16))
            s_ref[i] = state * decay + ktu - m[:HEAD_DIM]
            o = m[HEAD_DIM:] + qku
            z = z_ref[r0:r0 + CHUNK, i * HEAD_DIM:(i + 1) * HEAD_DIM].astype(F32)
            o_ref[r0:r0 + CHUNK, i * HEAD_DIM:(i + 1) * HEAD_DIM] = (_rms(o, nw_ref[...]) * z).astype(BF16)


def _chunk_tri_dup(ts):
    r = jnp.arange(ts)[:, None]
    j = jnp.arange(2 * ts)[None, :]
    c = (j // (2 * CHUNK)) * CHUNK + j % CHUNK
    return ((r // CHUNK == c // CHUNK) & (r <= c)).astype(F32)


def _gdn(qk, v, gates, gate_col0, gab_t, norm_w, *, batch, seq, ts=256):
    n = qk.shape[0]
    nt = seq // ts
    rep = GDN_V_HEADS // GDN_QK_HEADS
    wide = rep * HEAD_DIM
    head = lambda col0: pl.BlockSpec((ts, HEAD_DIM), lambda b, h, t: (b * nt + t, h + col0))
    pair = lambda col0: pl.BlockSpec((ts, wide), lambda b, h, t: (b * nt + t, h + col0))
    return pl.pallas_call(
        _gdn_body, out_shape=jax.ShapeDtypeStruct((n, GDN_V_HEADS * HEAD_DIM), BF16),
        grid=(batch, GDN_QK_HEADS, nt),
        in_specs=[head(0), head(GDN_QK_HEADS), pair(0), pair(gate_col0),
                  pl.BlockSpec((SUBLANES, ts), lambda b, h, t: (h, b * nt + t)),
                  _resident((1, HEAD_DIM)), _resident((ts, 2 * ts))],
        out_specs=pair(0),
        scratch_shapes=[pltpu.VMEM((rep, HEAD_DIM, HEAD_DIM), F32)],
        compiler_params=_params("arbitrary", "arbitrary", "arbitrary"), name="gated_delta",
    )(qk, qk, v, gates, gab_t, norm_w.reshape(1, HEAD_DIM), _chunk_tri_dup(ts))


def _merge_body(oh_ref, og_ref, gate_ref, h_ref, wbh_ref, wbg_ref, wo_ref, o_ref):
    d = h_ref.shape[1]
    yh = _dot(oh_ref[...], wbh_ref[...])
    yg = _dot(og_ref[...], wbg_ref[...])
    y = gate_ref[:, :d].astype(F32) * yh + gate_ref[:, d:].astype(F32) * yg
    o_ref[...] = h_ref[...] + _dot(y.astype(BF16), wo_ref[...])


def _merge(o_h, o_g, merge_gates, h, wbh, wbg, wo, *, tm=512):
    n, d = h.shape
    row = lambda width: pl.BlockSpec((tm, width), lambda i: (i, 0))
    return pl.pallas_call(
        _merge_body, out_shape=jax.ShapeDtypeStruct((n, d), F32), grid=(n // tm,),
        in_specs=[row(o_h.shape[1]), row(o_g.shape[1]), row(2 * d), row(d),
                  _resident(wbh.shape), _resident(wbg.shape), _resident(wo.shape)],
        out_specs=row(d),
        compiler_params=_params("arbitrary"), name="merge",
    )(o_h, o_g, merge_gates, h, wbh.astype(BF16), wbg.astype(BF16), wo.astype(BF16))


def _mixer(u, h, w_in, lb_logits, hgrn_norm, conv_w, a_log, dt_bias, gdn_norm, wbh, wbg, wo, *, layer, batch, seq):
    n, d = h.shape
    hk = HG_HEADS * HEAD_DIM
    gk = GDN_QK_HEADS * HEAD_DIM
    gv = GDN_V_HEADS * HEAD_DIM
    sizes = (hk, hk, hk, hk, gk, gk, gv, GDN_V_HEADS, GDN_V_HEADS, gv, d, d)
    offs = [0]
    for s in sizes:
        offs.append(offs[-1] + s)
    col = lambda a, b: w_in[:, offs[a]:offs[b]].astype(BF16)
    tm, tn = 1024, 512
    tiles_per_seq = seq // tm

    (q_h,) = _proj_call(_proj_hq_body, u, col(0, 1), None, (BF16,), tm=tm, tn=tn, name="proj_hq")
    lf, k_h = _proj_call(functools.partial(_proj_hf_body, layer=layer), u, col(1, 2), lb_logits.astype(F32),
                         (F32, BF16), tm=tm, tn=tn, name="proj_hf")
    (v_h,) = _proj_call(_proj_id_body, u, col(2, 3), None, (BF16,), tm=tm, tn=tn, name="proj_hi")
    w_silu = jnp.concatenate([col(3, 4), col(9, 10)], axis=1)
    (act_gates,) = _proj_call(_proj_silu_body, u, w_silu, None, (BF16,), tm=tm, tn=tn, name="proj_silu")
    (merge_gates,) = _proj_call(_proj_sigmoid_body, u, col(10, 12), None, (BF16,), tm=tm, tn=tn,
                                name="proj_merge_gates")

    conv_pad = jnp.zeros((SUBLANES - CONV_K, conv_w.shape[1]), F32)
    cw = jnp.concatenate([conv_w, conv_pad], axis=0)
    conv_scratch = lambda cols: [pltpu.VMEM((tm + SUBLANES, tn), F32), pltpu.VMEM((cols // tn, SUBLANES, tn), F32)]
    (qk_g,) = _proj_call(
        functools.partial(_proj_conv_qk_body, tiles_per_seq=tiles_per_seq, q_tiles=gk // tn),
        u, col(4, 6), cw[:, :2 * gk], (BF16,), tm=tm, tn=tn, scratch=conv_scratch(2 * gk), name="proj_conv_qk")
    (v_g,) = _proj_call(
        functools.partial(_proj_conv_v_body, tiles_per_seq=tiles_per_seq),
        u, col(6, 7), cw[:, 2 * gk:], (BF16,), tm=tm, tn=tn, scratch=conv_scratch(gv), name="proj_conv_v")

    rep = GDN_V_HEADS // GDN_QK_HEADS
    pad = SUBLANES - 2 * rep
    regroup = lambda a, b, fill: jnp.concatenate(
        [a.reshape(-1, GDN_QK_HEADS, rep), b.reshape(-1, GDN_QK_HEADS, rep),
         jnp.full((a.shape[0], GDN_QK_HEADS, pad), fill, a.dtype)], axis=2).reshape(a.shape[0], -1)
    w_ga = w_in[:, offs[7]:offs[8]]
    w_gb = w_in[:, offs[8]:offs[9]]
    w_gab = regroup(w_ga, w_gb, 0.0)
    w_gab = jnp.pad(w_gab, ((0, 0), (0, LANES - w_gab.shape[1]))).astype(BF16)
    zeros = jnp.zeros((1, GDN_V_HEADS), F32)
    aux = jnp.concatenate([regroup(a_log.reshape(1, -1).astype(F32), zeros, 0.0),
                           regroup(dt_bias.reshape(1, -1).astype(F32), zeros, 0.0)], axis=0)
    aux = jnp.pad(aux, ((0, 0), (0, LANES - aux.shape[1])))
    gab_t = _proj_gab(u, w_gab, aux, tm=tm)

    o_h = _hgrn(q_h, lf, k_h, v_h, act_gates, 0, hgrn_norm, batch=batch, seq=seq)
    o_g = _gdn(qk_g, v_g, act_gates, hk // (rep * HEAD_DIM), gab_t, gdn_norm, batch=batch, seq=seq)
    return _merge(o_h, o_g, merge_gates, h, wbh, wbg, wo)


def kernel(x, ffn1_norm, ffn1_w_in, ffn1_w_out, mix_norm, w_in, hgrn_lb_logits, hgrn_out_norm, gdn_conv_w,
           gdn_a_log, gdn_dt_bias, gdn_out_norm, w_branch_hgrn, w_branch_gdn, w_out, ffn2_norm, ffn2_w_in,
           ffn2_w_out, final_norm):
    batch, seq, d = x.shape
    depth = ffn1_norm.shape[0]
    h = x.reshape(batch * seq, d)
    for l in range(depth):
        h, u = _ffn(h, ffn1_norm[l], ffn1_w_in[l], ffn1_w_out[l], mix_norm[l], final=False)
        h = _mixer(u, h, w_in[l], hgrn_lb_logits, hgrn_out_norm[l], gdn_conv_w[l], gdn_a_log[l],
                   gdn_dt_bias[l], gdn_out_norm[l], w_branch_hgrn[l], w_branch_gdn[l], w_out[l],
                   layer=l, batch=batch, seq=seq)
        last = l == depth - 1
        nxt = final_norm if last else ffn1_norm[l + 1]
        out = _ffn(h, ffn2_norm[l], ffn2_w_in[l], ffn2_w_out[l], nxt, final=last)
        h = out if last else out[0]
    return h.reshape(batch, seq, d)
```

```python
import functools

import jax
import jax.numpy as jnp
from jax import lax
from jax.experimental import pallas as pl
from jax.experimental.pallas import tpu as pltpu

F32 = jnp.float32
BF16 = jnp.bfloat16
HIGHEST = lax.Precision.HIGHEST

EPS = 1e-6
CHUNK = 64
SUB = 8
HEAD_DIM = 128
HG_HEADS = 8
GDN_QK_HEADS = 8
GDN_V_HEADS = 16
CONV_K = 4
LANES = 128
SUBLANES = 8
NEG_BIG = -1e30
VMEM_LIMIT = 56 * 1024 * 1024


def _dot(a, b):
    return jnp.dot(a, b, preferred_element_type=F32)


def _dot_nt(a, b):
    return lax.dot_general(a, b, (((1,), (1,)), ((), ())), preferred_element_type=F32)


def _dot_tn(a, b):
    return lax.dot_general(a, b, (((0,), (0,)), ((), ())), preferred_element_type=F32)


def _sigmoid(x):
    return 1.0 / (1.0 + jnp.exp(-x))


def _rms(x, w):
    ms = jnp.mean(x * x, axis=-1, keepdims=True)
    return x * lax.rsqrt(ms + EPS) * w


def _params(*sem):
    return pltpu.CompilerParams(dimension_semantics=sem, vmem_limit_bytes=VMEM_LIMIT)


def _resident(shape):
    return pl.BlockSpec(shape, lambda *_: (0,) * len(shape), pipeline_mode=pl.Buffered(1))


def _ffn_body(h_ref, nw_ref, wa_ref, wb_ref, wo_ref, onw_ref, *out_refs, ff_tile, final):
    h = h_ref[...]
    xn = _rms(h, nw_ref[...]).astype(BF16)
    d_ff = wa_ref.shape[1]
    acc = jnp.zeros(h.shape, F32)
    for j in range(0, d_ff, ff_tile):
        w = min(ff_tile, d_ff - j)
        a = _dot(xn, wa_ref[:, j:j + w])
        b = _dot(xn, wb_ref[:, j:j + w])
        g = (a * _sigmoid(a) * b).astype(BF16)
        acc = acc + _dot(g, wo_ref[j:j + w, :])
    hn = h + 0.5 * acc
    if final:
        out_refs[0][...] = _rms(hn, onw_ref[...])
    else:
        out_refs[0][...] = hn
        out_refs[1][...] = _rms(hn, onw_ref[...]).astype(BF16)


def _ffn(h, norm_w, w_in, w_out, next_norm_w, *, final, tm=512, ff_tile=512):
    n, d = h.shape
    d_ff = w_out.shape[0]
    wa = w_in[:, :d_ff].astype(BF16)
    wb = w_in[:, d_ff:].astype(BF16)
    wo = w_out.astype(BF16)
    row = pl.BlockSpec((tm, d), lambda i: (i, 0))
    if final:
        out_shape = jax.ShapeDtypeStruct((n, d), F32)
        out_specs = row
    else:
        out_shape = (jax.ShapeDtypeStruct((n, d), F32), jax.ShapeDtypeStruct((n, d), BF16))
        out_specs = (row, row)
    return pl.pallas_call(
        functools.partial(_ffn_body, ff_tile=ff_tile, final=final),
        out_shape=out_shape,
        grid=(n // tm,),
        in_specs=[row, _resident((1, d)), _resident((d, d_ff)), _resident((d, d_ff)),
                  _resident((d_ff, d)), _resident((1, d))],
        out_specs=out_specs,
        compiler_params=_params("arbitrary"),
        name="ffn_final" if final else "ffn",
    )(h, norm_w.reshape(1, d), wa, wb, wo, next_norm_w.reshape(1, d))


def _proj_call(body, u, w, aux, outs, *, tm, tn, scratch=(), name):
    n, d = u.shape
    cols = w.shape[1]
    in_specs = [pl.BlockSpec((tm, d), lambda i, j: (i, 0)),
                pl.BlockSpec((d, tn), lambda i, j: (0, j))]
    args = [u, w]
    if aux is not None:
        in_specs.append(pl.BlockSpec((aux.shape[0], tn), lambda i, j: (0, j)))
        args.append(aux)
    out_shape = tuple(jax.ShapeDtypeStruct((n, cols), dt) for dt in outs)
    out_specs = tuple(pl.BlockSpec((tm, tn), lambda i, j: (i, j)) for _ in outs)
    return pl.pallas_call(
        body, out_shape=out_shape, grid=(n // tm, cols // tn),
        in_specs=in_specs, out_specs=out_specs, scratch_shapes=list(scratch),
        compiler_params=_params("arbitrary", "arbitrary"), name=name,
    )(*args)


def _proj_hq_body(u_ref, w_ref, o_ref):
    p = _dot(u_ref[...], w_ref[...])
    o_ref[...] = (p * _sigmoid(p) * HEAD_DIM ** -0.5).astype(BF16)


def _proj_hf_body(u_ref, w_ref, logit_ref, lf_ref, k_ref, *, layer):
    p = _dot(u_ref[...], w_ref[...])
    lg = logit_ref[...]
    e = jnp.exp(lg - jnp.max(lg, axis=0, keepdims=True))
    lb = jnp.sum(e[0:layer + 1, :], axis=0, keepdims=True) / jnp.sum(e, axis=0, keepdims=True)
    s = _sigmoid(p)
    lf_ref[...] = jnp.log(lb + (1.0 - lb) * s)
    k_ref[...] = ((1.0 - lb) * (1.0 - s)).astype(BF16)


def _proj_id_body(u_ref, w_ref, o_ref):
    o_ref[...] = _dot(u_ref[...], w_ref[...]).astype(BF16)


def _proj_silu_body(u_ref, w_ref, o_ref):
    p = _dot(u_ref[...], w_ref[...])
    o_ref[...] = (p * _sigmoid(p)).astype(BF16)


def _proj_sigmoid_body(u_ref, w_ref, o_ref):
    o_ref[...] = _sigmoid(_dot(u_ref[...], w_ref[...])).astype(BF16)


def _causal_conv_silu(p, cw_ref, ext_ref, tail_ref, tiles_per_seq):
    i = pl.program_id(0)
    j = pl.program_id(1)
    tm = p.shape[0]

    @pl.when(i % tiles_per_seq == 0)
    def _():
        ext_ref[0:SUBLANES, :] = jnp.zeros((SUBLANES, p.shape[1]), F32)

    @pl.when(i % tiles_per_seq != 0)
    def _():
        ext_ref[0:SUBLANES, :] = tail_ref[j]

    ext_ref[SUBLANES:, :] = p
    tail_ref[j] = p[tm - SUBLANES:, :]
    acc = p * cw_ref[CONV_K - 1:CONV_K, :]
    for d in range(1, CONV_K):
        acc = acc + ext_ref[SUBLANES - d:SUBLANES - d + tm, :] * cw_ref[CONV_K - 1 - d:CONV_K - d, :]
    return acc * _sigmoid(acc)


def _proj_conv_qk_body(u_ref, w_ref, cw_ref, o_ref, ext_ref, tail_ref, *, tiles_per_seq, q_tiles):
    y = _causal_conv_silu(_dot(u_ref[...], w_ref[...]), cw_ref, ext_ref, tail_ref, tiles_per_seq)
    scale = jnp.where(pl.program_id(1) < q_tiles, HEAD_DIM ** -0.5, 1.0).astype(F32)
    for s in range(y.shape[1] // HEAD_DIM):
        ys = y[:, s * HEAD_DIM:(s + 1) * HEAD_DIM]
        inv = lax.rsqrt(jnp.sum(ys * ys, axis=-1, keepdims=True) + EPS) * scale
        o_ref[:, s * HEAD_DIM:(s + 1) * HEAD_DIM] = (ys * inv).astype(BF16)


def _proj_conv_v_body(u_ref, w_ref, cw_ref, o_ref, ext_ref, tail_ref, *, tiles_per_seq):
    y = _causal_conv_silu(_dot(u_ref[...], w_ref[...]), cw_ref, ext_ref, tail_ref, tiles_per_seq)
    o_ref[...] = y.astype(BF16)


def _proj_gab_body(u_ref, w_ref, aux_ref, o_ref):
    p = _dot(u_ref[...], w_ref[...])
    x = p + aux_ref[1:2, :]
    softplus = jnp.maximum(x, 0.0) + jnp.log1p(jnp.exp(-jnp.abs(x)))
    g = -jnp.exp(aux_ref[0:1, :]) * softplus
    lane = lax.broadcasted_iota(jnp.int32, p.shape, 1)
    res = jnp.where(lane % SUBLANES < 2, g, _sigmoid(p))
    o_ref[...] = res.T


def _proj_gab(u, w, aux, *, tm):
    n, d = u.shape
    return pl.pallas_call(
        _proj_gab_body, out_shape=jax.ShapeDtypeStruct((LANES, n), F32), grid=(n // tm,),
        in_specs=[pl.BlockSpec((tm, d), lambda i: (i, 0)), _resident((d, LANES)), _resident((2, LANES))],
        out_specs=pl.BlockSpec((LANES, tm), lambda i: (0, i)),
        compiler_params=_params("arbitrary"), name="proj_gab",
    )(u, w, aux)


def _hgrn_body(q_ref, lf_ref, k_ref, v_ref, gate_ref, nw_ref, tri_ref, o_ref, st_ref):
    @pl.when(pl.program_id(2) == 0)
    def _():
        st_ref[...] = jnp.zeros_like(st_ref)

    ts = q_ref.shape[0]
    n_chunks = ts // CHUNK
    lf = lf_ref[...]
    hi = lf.astype(BF16)
    r1 = lf - hi.astype(F32)
    mid = r1.astype(BF16)
    lo = (r1 - mid.astype(F32)).astype(BF16)
    by_chunk = lambda x: jnp.concatenate([x[c * CHUNK:(c + 1) * CHUNK] for c in range(n_chunks)], axis=1)
    b_all = _dot(tri_ref[...], jnp.concatenate([by_chunk(hi), by_chunk(mid), by_chunk(lo)], axis=0))

    row = lax.broadcasted_iota(jnp.int32, (CHUNK, HEAD_DIM), 0)
    t_idx = lax.broadcasted_iota(jnp.int32, (CHUNK, CHUNK), 0)
    s_idx = lax.broadcasted_iota(jnp.int32, (CHUNK, CHUNK), 1)
    levels = []
    half = CHUNK // 2
    while half >= SUB:
        size = 2 * half
        levels.append((half, row % size >= half,
                       (t_idx // size == s_idx // size) & (t_idx % size >= half) & (s_idx % size < half)))
        half //= 2
    causal_in_block = [row % SUB >= j for j in range(SUB)]
    place = [s_idx == (t_idx // SUB) * SUB + j for j in range(SUB)]
    for c in range(n_chunks):
        r0 = c * CHUNK
        b = b_all[:, c * HEAD_DIM:(c + 1) * HEAD_DIM]
        q = q_ref[r0:r0 + CHUNK, :].astype(F32)
        k = k_ref[r0:r0 + CHUNK, :].astype(F32)
        v = v_ref[r0:r0 + CHUNK, :]
        b_end = b[CHUNK - 1:CHUNK, :]
        st = st_ref[...]
        inter = _dot_nt((q * jnp.exp(b)).astype(BF16), st.astype(BF16))

        scores = jnp.zeros((CHUNK, CHUNK), F32)
        for half, past_mid, pair in levels:
            size = 2 * half
            b_mid = jnp.concatenate(
                [jnp.broadcast_to(b[m + half - 1:m + half, :], (size, HEAD_DIM)) for m in range(0, CHUNK, size)], axis=0)
            x = (jnp.where(past_mid, q, k) * jnp.exp(-jnp.abs(b - b_mid))).astype(BF16)
            scores = jnp.where(pair, _dot_nt(x, x), scores)

        blocks = lambda x: x.reshape(CHUNK // SUB, SUB, HEAD_DIM)
        for j in range(SUB):
            pick = lambda x: jnp.broadcast_to(blocks(x)[:, j:j + 1, :], blocks(x).shape).reshape(CHUNK, HEAD_DIM)
            dec = jnp.exp(jnp.where(causal_in_block[j], b - pick(b), NEG_BIG))
            col = jnp.sum(q * pick(k) * dec, axis=-1, keepdims=True)
            scores = jnp.where(place[j], col, scores)

        o = inter + _dot(scores.astype(BF16), v)

        k_end = (k * jnp.exp(b_end - b)).astype(BF16)
        st_ref[...] = st * jnp.exp(b_end) + _dot_tn(v, k_end)

        o_ref[r0:r0 + CHUNK, :] = (_rms(o, nw_ref[...]) * gate_ref[r0:r0 + CHUNK, :].astype(F32)).astype(BF16)


def _hgrn(q, lf, k, v, gates, gate_col0, norm_w, *, batch, seq, ts=512):
    n = q.shape[0]
    nt = seq // ts
    blk = lambda col0: pl.BlockSpec((ts, HEAD_DIM), lambda b, h, t: (b * nt + t, h + col0))
    tri = (jnp.arange(CHUNK)[:, None] >= jnp.arange(CHUNK)[None, :]).astype(BF16)
    tri3 = jnp.concatenate([tri, tri, tri], axis=1)
    return pl.pallas_call(
        _hgrn_body, out_shape=jax.ShapeDtypeStruct((n, HG_HEADS * HEAD_DIM), BF16),
        grid=(batch, HG_HEADS, nt),
        in_specs=[blk(0), blk(0), blk(0), blk(0), blk(gate_col0), _resident((1, HEAD_DIM)),
                  _resident((CHUNK, 3 * CHUNK))],
        out_specs=blk(0),
        scratch_shapes=[pltpu.VMEM((HEAD_DIM, HEAD_DIM), F32)],
        compiler_params=_params("arbitrary", "arbitrary", "arbitrary"), name="hgrn2",
    )(q, lf, k, v, gates, norm_w.reshape(1, HEAD_DIM), tri3)


def _split_bf16(x):
    hi = x.astype(BF16)
    lo = (x - hi.astype(F32)).astype(BF16)
    return hi, lo


def _split3_bf16(x):
    hi = x.astype(BF16)
    r1 = x - hi.astype(F32)
    mid = r1.astype(BF16)
    lo = (r1 - mid.astype(F32)).astype(BF16)
    return hi, mid, lo


def _matmul_3term(x_dup, y):
    lane = lax.broadcasted_iota(jnp.int32, x_dup.shape, 1)
    xh, xl = _split_bf16(x_dup)
    yh, yl = _split_bf16(y)
    lhs = jnp.concatenate([jnp.where(lane < CHUNK, xh, xl), xh], axis=1)
    rhs = jnp.concatenate([yh, yh, yl, jnp.zeros_like(yh)], axis=0)
    return _dot(lhs, rhs)


def _unit_lower_inverses(a_dups):
    shape = a_dups[0].shape
    row = lax.broadcasted_iota(jnp.int32, shape, 0)
    lane = lax.broadcasted_iota(jnp.int32, shape, 1)
    partial = [jnp.where(lane < CHUNK, jnp.where(row == lane, 1.0, 0.0) - a, 0.0) for a in a_dups]
    power = [_matmul_3term(a, jnp.concatenate([a, a], axis=1))[:, :2 * CHUNK] for a in a_dups]
    span = 2
    while span < CHUNK:
        nxt = []
        for p in range(len(a_dups)):
            res = _matmul_3term(power[p], jnp.concatenate([power[p], partial[p]], axis=1))
            nxt.append(res[:, :2 * CHUNK])
            partial[p] = partial[p] + res[:, 2 * CHUNK:]
        power = nxt
        span *= 2
    return partial


def _gdn_body(q_ref, k_ref, v_ref, z_ref, gb_ref, nw_ref, triu_ref, tril_ref, o_ref, s_ref):
    @pl.when(pl.program_id(2) == 0)
    def _():
        s_ref[...] = jnp.zeros_like(s_ref)

    ts = q_ref.shape[0]
    n_chunks = ts // CHUNK
    heads = q_ref.shape[1] // HEAD_DIM
    rep = s_ref.shape[0] // heads
    dup = 2 * CHUNK
    g_stack = jnp.concatenate([gb_ref[SUBLANES * h:SUBLANES * (h + 1), c * CHUNK:(c + 1) * CHUNK]
                               for h in range(heads) for c in range(n_chunks)], axis=0)
    g_cols = jnp.concatenate([g_stack, jnp.zeros_like(g_stack)], axis=1).T[:CHUNK]
    gam_rows = _dot(jnp.concatenate(_split3_bf16(g_stack), axis=1), triu_ref[...])
    gam_cols = _dot(tril_ref[...], jnp.concatenate(_split3_bf16(g_cols), axis=0))
    t_dup = lax.broadcasted_iota(jnp.int32, (CHUNK, dup), 0)
    s_dup = lax.broadcasted_iota(jnp.int32, (CHUNK, dup), 1) % CHUNK
    s_idx = lax.broadcasted_iota(jnp.int32, (CHUNK, CHUNK), 0)
    t_idx = lax.broadcasted_iota(jnp.int32, (CHUNK, CHUNK), 1)

    probs = [(h, c, i) for h in range(heads) for c in range(n_chunks) for i in range(rep)]
    a_dups, pre = [], {}
    for h in range(heads):
        for c in range(n_chunks):
            r0 = c * CHUNK
            q = q_ref[r0:r0 + CHUNK, h * HEAD_DIM:(h + 1) * HEAD_DIM]
            k = k_ref[r0:r0 + CHUNK, h * HEAD_DIM:(h + 1) * HEAD_DIM]
            qf = q.astype(F32)
            kf = k.astype(F32)
            kkq = _dot_nt(k, jnp.concatenate([k, k, q, q], axis=0))
            kk_dup = kkq[:, :dup]
            kq = kkq[:, dup:dup + CHUNK]
            for i in range(rep):
                col = SUBLANES * (h * n_chunks + c) + i
                vcol = (h * rep + i) * HEAD_DIM
                gam_r_dup = gam_rows[col:col + 1, :]
                gam_r = gam_r_dup[:, :CHUNK]
                gam_c = gam_cols[:, col:col + 1]
                beta = g_cols[:, col + rep:col + rep + 1]
                g_end = gam_c[CHUNK - 1:CHUNK, :]
                lmat_dup = jnp.exp(jnp.where(t_dup >= s_dup, gam_c - gam_r_dup, NEG_BIG))
                a_dups.append(jnp.where(t_dup > s_dup, beta * kk_dup * lmat_dup, 0.0))
                lmat_t = jnp.exp(jnp.where(t_idx >= s_idx, gam_r - gam_c, NEG_BIG))
                e_gam = jnp.exp(gam_c)
                v = v_ref[r0:r0 + CHUNK, vcol:vcol + HEAD_DIM].astype(F32)
                rhs = jnp.concatenate([(v * beta).astype(BF16), (kf * (beta * e_gam)).astype(BF16)], axis=1)
                k_end = kf * jnp.exp(g_end - gam_c)
                x_t = jnp.concatenate([k_end, kq * lmat_t], axis=1)
                pre[(h, c, i)] = (rhs, x_t, qf * e_gam, jnp.exp(g_end))

    inverses = _unit_lower_inverses(a_dups)

    chain = {}
    for p, key in enumerate(probs):
        rhs, x_t, qg, decay = pre[key]
        t_inv = inverses[p][:, :CHUNK].astype(BF16)
        uw = _dot(t_inv, rhs).astype(BF16)
        r = _dot(x_t.T.astype(BF16), uw)
        ktu = r[:HEAD_DIM, :HEAD_DIM]
        ktw = r[:HEAD_DIM, HEAD_DIM:]
        qku = r[HEAD_DIM:, :HEAD_DIM]
        qkw = r[HEAD_DIM:, HEAD_DIM:]
        lhs = jnp.concatenate([ktw, qg - qkw], axis=0).astype(BF16)
        chain[key] = (lhs, ktu, qku, decay)

    zero_state = jnp.zeros((HEAD_DIM, HEAD_DIM), BF16)
    states = [s_ref[j] for j in range(heads * rep)]
    for c in range(n_chunks):
        r0 = c * CHUNK
        for h in range(heads):
            lhs = jnp.concatenate([chain[(h, c, i)][0] for i in range(rep)], axis=1)
            diag = jnp.concatenate(
                [jnp.concatenate([states[h * rep + i].astype(BF16) if j == i else zero_state for j in range(rep)],
                                 axis=1) for i in range(rep)], axis=0)
            m_all = _dot(lhs, diag)
            for i in range(rep):
                _, ktu, qku, decay = chain[(h, c, i)]
                vcol = (h * rep + i) * HEAD_DIM
                m = m_all[:, i * HEAD_DIM:(i + 1) * HEAD_DIM]
                states[h * rep + i] = states[h * rep + i] * decay + ktu - m[:HEAD_DIM]
                o = m[HEAD_DIM:] + qku
                z = z_ref[r0:r0 + CHUNK, vcol:vcol + HEAD_DIM].astype(F32)
                o_ref[r0:r0 + CHUNK, vcol:vcol + HEAD_DIM] = (_rms(o, nw_ref[...]) * z).astype(BF16)
    for j in range(heads * rep):
        s_ref[j] = states[j]


def _gdn(qk, v, gates, gab_t, norm_w, *, batch, seq, ts=128, heads=8):
    n = qk.shape[0]
    nt = seq // ts
    groups = GDN_QK_HEADS // heads
    rep = GDN_V_HEADS // GDN_QK_HEADS
    blk = lambda width, col0: pl.BlockSpec((ts, width * HEAD_DIM), lambda b, g, t: (b * nt + t, g + col0))
    r = jnp.arange(CHUNK)[:, None]
    c = jnp.arange(CHUNK)[None, :]
    upper = (r <= c).astype(BF16)
    upper_dup = jnp.concatenate([upper, upper], axis=1)
    triu3 = jnp.concatenate([upper_dup] * 3, axis=0)
    tril3 = jnp.concatenate([(r >= c).astype(BF16)] * 3, axis=1)
    return pl.pallas_call(
        _gdn_body, out_shape=jax.ShapeDtypeStruct((n, GDN_V_HEADS * HEAD_DIM), BF16),
        grid=(batch, groups, nt),
        in_specs=[blk(heads, 0), blk(heads, groups), blk(heads * rep, 0), blk(heads * rep, 0),
                  pl.BlockSpec((SUBLANES * heads, ts), lambda b, g, t: (g, b * nt + t)),
                  _resident((1, HEAD_DIM)), _resident(triu3.shape), _resident(tril3.shape)],
        out_specs=blk(heads * rep, 0),
        scratch_shapes=[pltpu.VMEM((heads * rep, HEAD_DIM, HEAD_DIM), F32)],
        compiler_params=_params("arbitrary", "arbitrary", "arbitrary"), name="gated_delta",
    )(qk, qk, v, gates, gab_t, norm_w.reshape(1, HEAD_DIM), triu3, tril3)


def _merge_body(oh_ref, og_ref, gate_ref, h_ref, wbh_ref, wbg_ref, wo_ref, o_ref):
    d = h_ref.shape[1]
    yh = _dot(oh_ref[...], wbh_ref[...])
    yg = _dot(og_ref[...], wbg_ref[...])
    y = gate_ref[:, :d].astype(F32) * yh + gate_ref[:, d:].astype(F32) * yg
    o_ref[...] = h_ref[...] + _dot(y.astype(BF16), wo_ref[...])


def _merge(o_h, o_g, merge_gates, h, wbh, wbg, wo, *, tm=512):
    n, d = h.shape
    row = lambda width: pl.BlockSpec((tm, width), lambda i: (i, 0))
    return pl.pallas_call(
        _merge_body, out_shape=jax.ShapeDtypeStruct((n, d), F32), grid=(n // tm,),
        in_specs=[row(o_h.shape[1]), row(o_g.shape[1]), row(2 * d), row(d),
                  _resident(wbh.shape), _resident(wbg.shape), _resident(wo.shape)],
        out_specs=row(d),
        compiler_params=_params("arbitrary"), name="merge",
    )(o_h, o_g, merge_gates, h, wbh.astype(BF16), wbg.astype(BF16), wo.astype(BF16))


def _mixer(u, h, w_in, lb_logits, hgrn_norm, conv_w, a_log, dt_bias, gdn_norm, wbh, wbg, wo, *, layer, batch, seq):
    n, d = h.shape
    hk = HG_HEADS * HEAD_DIM
    gk = GDN_QK_HEADS * HEAD_DIM
    gv = GDN_V_HEADS * HEAD_DIM
    sizes = (hk, hk, hk, hk, gk, gk, gv, GDN_V_HEADS, GDN_V_HEADS, gv, d, d)
    offs = [0]
    for s in sizes:
        offs.append(offs[-1] + s)
    col = lambda a, b: w_in[:, offs[a]:offs[b]].astype(BF16)
    tm, tn = 1024, 512
    tiles_per_seq = seq // tm

    (q_h,) = _proj_call(_proj_hq_body, u, col(0, 1), None, (BF16,), tm=tm, tn=tn, name="proj_hq")
    lf, k_h = _proj_call(functools.partial(_proj_hf_body, layer=layer), u, col(1, 2), lb_logits.astype(F32),
                         (F32, BF16), tm=tm, tn=tn, name="proj_hf")
    (v_h,) = _proj_call(_proj_id_body, u, col(2, 3), None, (BF16,), tm=tm, tn=tn, name="proj_hi")
    w_silu = jnp.concatenate([col(9, 10), col(3, 4)], axis=1)
    (act_gates,) = _proj_call(_proj_silu_body, u, w_silu, None, (BF16,), tm=tm, tn=tn, name="proj_silu")
    (merge_gates,) = _proj_call(_proj_sigmoid_body, u, col(10, 12), None, (BF16,), tm=tm, tn=tn,
                                name="proj_merge_gates")

    conv_pad = jnp.zeros((SUBLANES - CONV_K, conv_w.shape[1]), F32)
    cw = jnp.concatenate([conv_w, conv_pad], axis=0)
    conv_scratch = lambda cols: [pltpu.VMEM((tm + SUBLANES, tn), F32), pltpu.VMEM((cols // tn, SUBLANES, tn), F32)]
    (qk_g,) = _proj_call(
        functools.partial(_proj_conv_qk_body, tiles_per_seq=tiles_per_seq, q_tiles=gk // tn),
        u, col(4, 6), cw[:, :2 * gk], (BF16,), tm=tm, tn=tn, scratch=conv_scratch(2 * gk), name="proj_conv_qk")
    (v_g,) = _proj_call(
        functools.partial(_proj_conv_v_body, tiles_per_seq=tiles_per_seq),
        u, col(6, 7), cw[:, 2 * gk:], (BF16,), tm=tm, tn=tn, scratch=conv_scratch(gv), name="proj_conv_v")

    rep = GDN_V_HEADS // GDN_QK_HEADS
    pad = SUBLANES - 2 * rep
    regroup = lambda a, b, fill: jnp.concatenate(
        [a.reshape(-1, GDN_QK_HEADS, rep), b.reshape(-1, GDN_QK_HEADS, rep),
         jnp.full((a.shape[0], GDN_QK_HEADS, pad), fill, a.dtype)], axis=2).reshape(a.shape[0], -1)
    w_ga = w_in[:, offs[7]:offs[8]]
    w_gb = w_in[:, offs[8]:offs[9]]
    w_gab = regroup(w_ga, w_gb, 0.0)
    w_gab = jnp.pad(w_gab, ((0, 0), (0, LANES - w_gab.shape[1]))).astype(BF16)
    zeros = jnp.zeros((1, GDN_V_HEADS), F32)
    aux = jnp.concatenate([regroup(a_log.reshape(1, -1).astype(F32), zeros, 0.0),
                           regroup(dt_bias.reshape(1, -1).astype(F32), zeros, 0.0)], axis=0)
    aux = jnp.pad(aux, ((0, 0), (0, LANES - aux.shape[1])))
    gab_t = _proj_gab(u, w_gab, aux, tm=tm)

    o_h = _hgrn(q_h, lf, k_h, v_h, act_gates, gv // HEAD_DIM, hgrn_norm, batch=batch, seq=seq)
    o_g = _gdn(qk_g, v_g, act_gates, gab_t, gdn_norm, batch=batch, seq=seq)
    return _merge(o_h, o_g, merge_gates, h, wbh, wbg, wo)


def kernel(x, ffn1_norm, ffn1_w_in, ffn1_w_out, mix_norm, w_in, hgrn_lb_logits, hgrn_out_norm, gdn_conv_w,
           gdn_a_log, gdn_dt_bias, gdn_out_norm, w_branch_hgrn, w_branch_gdn, w_out, ffn2_norm, ffn2_w_in,
           ffn2_w_out, final_norm):
    batch, seq, d = x.shape
    depth = ffn1_norm.shape[0]
    h = x.reshape(batch * seq, d)
    for l in range(depth):
        h, u = _ffn(h, ffn1_norm[l], ffn1_w_in[l], ffn1_w_out[l], mix_norm[l], final=False)
        h = _mixer(u, h, w_in[l], hgrn_lb_logits, hgrn_out_norm[l], gdn_conv_w[l], gdn_a_log[l],
                   gdn_dt_bias[l], gdn_out_norm[l], w_branch_hgrn[l], w_branch_gdn[l], w_out[l],
                   layer=l, batch=batch, seq=seq)
        last = l == depth - 1
        nxt = final_norm if last else ffn1_norm[l + 1]
        out = _ffn(h, ffn2_norm[l], ffn2_w_in[l], ffn2_w_out[l], nxt, final=last)
        h = out if last else out[0]
    return h.reshape(batch, seq, d)
```

```python
import functools

import jax
import jax.numpy as jnp
from jax import lax
from jax.experimental import pallas as pl
from jax.experimental.pallas import tpu as pltpu

F32 = jnp.float32
BF16 = jnp.bfloat16
HIGHEST = lax.Precision.HIGHEST

EPS = 1e-6
CHUNK = 64
SUB = 8
HEAD_DIM = 128
HG_HEADS = 8
GDN_QK_HEADS = 8
GDN_V_HEADS = 16
CONV_K = 4
LANES = 128
SUBLANES = 8
NEG_BIG = -1e30
VMEM_LIMIT = 56 * 1024 * 1024


def _dot(a, b):
    return jnp.dot(a, b, preferred_element_type=F32)


def _dot_nt(a, b):
    return lax.dot_general(a, b, (((1,), (1,)), ((), ())), preferred_element_type=F32)


def _dot_tn(a, b):
    return lax.dot_general(a, b, (((0,), (0,)), ((), ())), preferred_element_type=F32)


def _sigmoid(x):
    return 1.0 / (1.0 + jnp.exp(-x))


def _rms(x, w):
    ms = jnp.mean(x * x, axis=-1, keepdims=True)
    return x * lax.rsqrt(ms + EPS) * w


def _params(*sem):
    return pltpu.CompilerParams(dimension_semantics=sem, vmem_limit_bytes=VMEM_LIMIT)


def _resident(shape):
    return pl.BlockSpec(shape, lambda *_: (0,) * len(shape), pipeline_mode=pl.Buffered(1))


def _ffn_body(h_ref, nw_ref, wa_ref, wb_ref, wo_ref, onw_ref, *out_refs, ff_tile, final):
    h = h_ref[...]
    xn = _rms(h, nw_ref[...]).astype(BF16)
    d_ff = wa_ref.shape[1]
    acc = jnp.zeros(h.shape, F32)
    for j in range(0, d_ff, ff_tile):
        w = min(ff_tile, d_ff - j)
        a = _dot(xn, wa_ref[:, j:j + w])
        b = _dot(xn, wb_ref[:, j:j + w])
        g = (a * _sigmoid(a) * b).astype(BF16)
        acc = acc + _dot(g, wo_ref[j:j + w, :])
    hn = h + 0.5 * acc
    if final:
        out_refs[0][...] = _rms(hn, onw_ref[...])
    else:
        out_refs[0][...] = hn
        out_refs[1][...] = _rms(hn, onw_ref[...]).astype(BF16)


def _ffn(h, norm_w, w_in, w_out, next_norm_w, *, final, tm=512, ff_tile=512):
    n, d = h.shape
    d_ff = w_out.shape[0]
    wa = w_in[:, :d_ff].astype(BF16)
    wb = w_in[:, d_ff:].astype(BF16)
    wo = w_out.astype(BF16)
    row = pl.BlockSpec((tm, d), lambda i: (i, 0))
    if final:
        out_shape = jax.ShapeDtypeStruct((n, d), F32)
        out_specs = row
    else:
        out_shape = (jax.ShapeDtypeStruct((n, d), F32), jax.ShapeDtypeStruct((n, d), BF16))
        out_specs = (row, row)
    return pl.pallas_call(
        functools.partial(_ffn_body, ff_tile=ff_tile, final=final),
        out_shape=out_shape,
        grid=(n // tm,),
        in_specs=[row, _resident((1, d)), _resident((d, d_ff)), _resident((d, d_ff)),
                  _resident((d_ff, d)), _resident((1, d))],
        out_specs=out_specs,
        compiler_params=_params("arbitrary"),
        name="ffn_final" if final else "ffn",
    )(h, norm_w.reshape(1, d), wa, wb, wo, next_norm_w.reshape(1, d))


def _proj_body(u_ref, w_ref, logit_ref, cw_ref, gaux_ref,
               qh_ref, lf_ref, kh_ref, vh_ref, act_ref, mg_ref, qk_ref, vg_ref, gab_ref, ext_ref,
               *, layer, tiles_per_seq, tn, widths):
    w_hq, w_hf, w_hi, w_act, w_mg, w_qk, w_gv = widths
    u = u_ref[...]
    tm = u.shape[0]
    tiles = lambda width: range(0, width, tn)
    proj = lambda c0, width=tn: _dot(u, w_ref[:, c0:c0 + width])
    base = 0

    for o in tiles(w_hq):
        p = proj(base + o)
        qh_ref[:, o:o + tn] = (p * _sigmoid(p) * HEAD_DIM ** -0.5).astype(BF16)
    base += w_hq

    lg = logit_ref[...]
    e = jnp.exp(lg - jnp.max(lg, axis=0, keepdims=True))
    lb_all = jnp.sum(e[0:layer + 1, :], axis=0, keepdims=True) / jnp.sum(e, axis=0, keepdims=True)
    for o in tiles(w_hf):
        s = _sigmoid(proj(base + o))
        lb = lb_all[:, o:o + tn]
        lf_ref[:, o:o + tn] = jnp.log(lb + (1.0 - lb) * s)
        kh_ref[:, o:o + tn] = ((1.0 - lb) * (1.0 - s)).astype(BF16)
    base += w_hf

    for o in tiles(w_hi):
        vh_ref[:, o:o + tn] = proj(base + o).astype(BF16)
    base += w_hi

    for o in tiles(w_act):
        p = proj(base + o)
        act_ref[:, o:o + tn] = (p * _sigmoid(p)).astype(BF16)
    base += w_act

    for o in tiles(w_mg):
        mg_ref[:, o:o + tn] = _sigmoid(proj(base + o)).astype(BF16)
    base += w_mg

    @pl.when(pl.program_id(0) % tiles_per_seq == 0)
    def _():
        ext_ref[0:SUBLANES, :] = jnp.zeros((SUBLANES, ext_ref.shape[1]), F32)

    for o in tiles(w_qk + w_gv):
        p = proj(base + o)
        ext_ref[SUBLANES:, o:o + tn] = p
        acc = p * cw_ref[CONV_K - 1:CONV_K, o:o + tn]
        for d in range(1, CONV_K):
            acc = acc + ext_ref[SUBLANES - d:SUBLANES - d + tm, o:o + tn] * cw_ref[CONV_K - 1 - d:CONV_K - d, o:o + tn]
        y = acc * _sigmoid(acc)
        if o < w_qk:
            scale = HEAD_DIM ** -0.5 if o < w_qk // 2 else 1.0
            for s in range(0, tn, HEAD_DIM):
                ys = y[:, s:s + HEAD_DIM]
                inv = lax.rsqrt(jnp.sum(ys * ys, axis=-1, keepdims=True) + EPS) * scale
                qk_ref[:, o + s:o + s + HEAD_DIM] = (ys * inv).astype(BF16)
        else:
            vg_ref[:, o - w_qk:o - w_qk + tn] = y.astype(BF16)
    ext_ref[0:SUBLANES, :] = ext_ref[tm:tm + SUBLANES, :]
    base += w_qk + w_gv

    p = proj(base, LANES)
    x = p + gaux_ref[1:2, :]
    softplus = jnp.maximum(x, 0.0) + jnp.log1p(jnp.exp(-jnp.abs(x)))
    g = -jnp.exp(gaux_ref[0:1, :]) * softplus
    lane = lax.broadcasted_iota(jnp.int32, p.shape, 1)
    gab_ref[...] = jnp.where(lane % SUBLANES < 2, g, _sigmoid(p)).T


def _proj(u, w_all, lb_logits, cw, gaux, widths, *, layer, seq, tm=256, tn=512):
    n, d = u.shape
    w_hq, w_hf, w_hi, w_act, w_mg, w_qk, w_gv = widths
    row = lambda width: pl.BlockSpec((tm, width), lambda i: (i, 0))
    out_widths = (w_hq, w_hf, w_hf, w_hi, w_act, w_mg, w_qk, w_gv)
    out_dtypes = (BF16, F32, BF16, BF16, BF16, BF16, BF16, BF16)
    out_shape = tuple(jax.ShapeDtypeStruct((n, w), dt) for w, dt in zip(out_widths, out_dtypes))
    out_shape += (jax.ShapeDtypeStruct((LANES, n), F32),)
    out_specs = tuple(row(w) for w in out_widths) + (pl.BlockSpec((LANES, tm), lambda i: (0, i)),)
    return pl.pallas_call(
        functools.partial(_proj_body, layer=layer, tiles_per_seq=seq // tm, tn=tn, widths=widths),
        out_shape=out_shape, grid=(n // tm,),
        in_specs=[row(d), _resident(w_all.shape), _resident(lb_logits.shape), _resident(cw.shape),
                  _resident(gaux.shape)],
        out_specs=out_specs,
        scratch_shapes=[pltpu.VMEM((tm + SUBLANES, w_qk + w_gv), F32)],
        compiler_params=_params("arbitrary"), name="proj",
    )(u, w_all, lb_logits, cw, gaux)


def _hgrn_body(q_ref, lf_ref, k_ref, v_ref, gate_ref, nw_ref, tri_ref, o_ref, st_ref):
    @pl.when(pl.program_id(2) == 0)
    def _():
        st_ref[...] = jnp.zeros_like(st_ref)

    ts = q_ref.shape[0]
    n_chunks = ts // CHUNK
    lf = lf_ref[...]
    hi = lf.astype(BF16)
    r1 = lf - hi.astype(F32)
    mid = r1.astype(BF16)
    lo = (r1 - mid.astype(F32)).astype(BF16)
    by_chunk = lambda x: jnp.concatenate([x[c * CHUNK:(c + 1) * CHUNK] for c in range(n_chunks)], axis=1)
    b_all = _dot(tri_ref[...], jnp.concatenate([by_chunk(hi), by_chunk(mid), by_chunk(lo)], axis=0))

    row = lax.broadcasted_iota(jnp.int32, (CHUNK, HEAD_DIM), 0)
    t_idx = lax.broadcasted_iota(jnp.int32, (CHUNK, CHUNK), 0)
    s_idx = lax.broadcasted_iota(jnp.int32, (CHUNK, CHUNK), 1)
    levels = []
    half = CHUNK // 2
    while half >= SUB:
        size = 2 * half
        levels.append((half, row % size >= half,
                       (t_idx // size == s_idx // size) & (t_idx % size >= half) & (s_idx % size < half)))
        half //= 2
    causal_in_block = [row % SUB >= j for j in range(SUB)]
    place = [s_idx == (t_idx // SUB) * SUB + j for j in range(SUB)]
    for c in range(n_chunks):
        r0 = c * CHUNK
        b = b_all[:, c * HEAD_DIM:(c + 1) * HEAD_DIM]
        q = q_ref[r0:r0 + CHUNK, :].astype(F32)
        k = k_ref[r0:r0 + CHUNK, :].astype(F32)
        v = v_ref[r0:r0 + CHUNK, :]
        b_end = b[CHUNK - 1:CHUNK, :]
        st = st_ref[...]
        inter = _dot_nt((q * jnp.exp(b)).astype(BF16), st.astype(BF16))

        scores = jnp.zeros((CHUNK, CHUNK), F32)
        for half, past_mid, pair in levels:
            size = 2 * half
            b_mid = jnp.concatenate(
                [jnp.broadcast_to(b[m + half - 1:m + half, :], (size, HEAD_DIM)) for m in range(0, CHUNK, size)], axis=0)
            x = (jnp.where(past_mid, q, k) * jnp.exp(-jnp.abs(b - b_mid))).astype(BF16)
            scores = jnp.where(pair, _dot_nt(x, x), scores)

        blocks = lambda x: x.reshape(CHUNK // SUB, SUB, HEAD_DIM)
        for j in range(SUB):
            pick = lambda x: jnp.broadcast_to(blocks(x)[:, j:j + 1, :], blocks(x).shape).reshape(CHUNK, HEAD_DIM)
            dec = jnp.exp(jnp.where(causal_in_block[j], b - pick(b), NEG_BIG))
            col = jnp.sum(q * pick(k) * dec, axis=-1, keepdims=True)
            scores = jnp.where(place[j], col, scores)

        o = inter + _dot(scores.astype(BF16), v)

        k_end = (k * jnp.exp(b_end - b)).astype(BF16)
        st_ref[...] = st * jnp.exp(b_end) + _dot_tn(v, k_end)

        o_ref[r0:r0 + CHUNK, :] = (_rms(o, nw_ref[...]) * gate_ref[r0:r0 + CHUNK, :].astype(F32)).astype(BF16)


def _hgrn(q, lf, k, v, gates, gate_col0, norm_w, *, batch, seq, ts=512):
    n = q.shape[0]
    nt = seq // ts
    blk = lambda col0: pl.BlockSpec((ts, HEAD_DIM), lambda b, h, t: (b * nt + t, h + col0))
    tri = (jnp.arange(CHUNK)[:, None] >= jnp.arange(CHUNK)[None, :]).astype(BF16)
    tri3 = jnp.concatenate([tri, tri, tri], axis=1)
    return pl.pallas_call(
        _hgrn_body, out_shape=jax.ShapeDtypeStruct((n, HG_HEADS * HEAD_DIM), BF16),
        grid=(batch, HG_HEADS, nt),
        in_specs=[blk(0), blk(0), blk(0), blk(0), blk(gate_col0), _resident((1, HEAD_DIM)),
                  _resident((CHUNK, 3 * CHUNK))],
        out_specs=blk(0),
        scratch_shapes=[pltpu.VMEM((HEAD_DIM, HEAD_DIM), F32)],
        compiler_params=_params("arbitrary", "arbitrary", "arbitrary"), name="hgrn2",
    )(q, lf, k, v, gates, norm_w.reshape(1, HEAD_DIM), tri3)


def _split_bf16(x):
    hi = x.astype(BF16)
    lo = (x - hi.astype(F32)).astype(BF16)
    return hi, lo


def _split3_bf16(x):
    hi = x.astype(BF16)
    r1 = x - hi.astype(F32)
    mid = r1.astype(BF16)
    lo = (r1 - mid.astype(F32)).astype(BF16)
    return hi, mid, lo


def _matmul_3term(x_dup, y):
    lane = lax.broadcasted_iota(jnp.int32, x_dup.shape, 1)
    xh, xl = _split_bf16(x_dup)
    yh, yl = _split_bf16(y)
    lhs = jnp.concatenate([jnp.where(lane < CHUNK, xh, xl), xh], axis=1)
    rhs = jnp.concatenate([yh, yh, yl, jnp.zeros_like(yh)], axis=0)
    return _dot(lhs, rhs)


def _unit_lower_inverses(a_dups):
    shape = a_dups[0].shape
    row = lax.broadcasted_iota(jnp.int32, shape, 0)
    lane = lax.broadcasted_iota(jnp.int32, shape, 1)
    partial = [jnp.where(lane < CHUNK, jnp.where(row == lane, 1.0, 0.0) - a, 0.0) for a in a_dups]
    power = [_matmul_3term(a, jnp.concatenate([a, a], axis=1))[:, :2 * CHUNK] for a in a_dups]
    span = 2
    while span < CHUNK:
        nxt = []
        for p in range(len(a_dups)):
            res = _matmul_3term(power[p], jnp.concatenate([power[p], partial[p]], axis=1))
            nxt.append(res[:, :2 * CHUNK])
            partial[p] = partial[p] + res[:, 2 * CHUNK:]
        power = nxt
        span *= 2
    return partial


def _gdn_body(q_ref, k_ref, v_ref, z_ref, gb_ref, nw_ref, triu_ref, tril_ref, o_ref, s_ref):
    @pl.when(pl.program_id(2) == 0)
    def _():
        s_ref[...] = jnp.zeros_like(s_ref)

    ts = q_ref.shape[0]
    n_chunks = ts // CHUNK
    heads = q_ref.shape[1] // HEAD_DIM
    rep = s_ref.shape[0] // heads
    dup = 2 * CHUNK
    g_stack = jnp.concatenate([gb_ref[SUBLANES * h:SUBLANES * (h + 1), c * CHUNK:(c + 1) * CHUNK]
                               for h in range(heads) for c in range(n_chunks)], axis=0)
    g_cols = jnp.concatenate([g_stack, jnp.zeros_like(g_stack)], axis=1).T[:CHUNK]
    gam_rows = _dot(jnp.concatenate(_split3_bf16(g_stack), axis=1), triu_ref[...])
    gam_cols = _dot(tril_ref[...], jnp.concatenate(_split3_bf16(g_cols), axis=0))
    t_dup = lax.broadcasted_iota(jnp.int32, (CHUNK, dup), 0)
    s_dup = lax.broadcasted_iota(jnp.int32, (CHUNK, dup), 1) % CHUNK
    s_idx = lax.broadcasted_iota(jnp.int32, (CHUNK, CHUNK), 0)
    t_idx = lax.broadcasted_iota(jnp.int32, (CHUNK, CHUNK), 1)

    probs = [(h, c, i) for h in range(heads) for c in range(n_chunks) for i in range(rep)]
    a_dups, pre = [], {}
    for h in range(heads):
        for c in range(n_chunks):
            r0 = c * CHUNK
            q = q_ref[r0:r0 + CHUNK, h * HEAD_DIM:(h + 1) * HEAD_DIM]
            k = k_ref[r0:r0 + CHUNK, h * HEAD_DIM:(h + 1) * HEAD_DIM]
            qf = q.astype(F32)
            kf = k.astype(F32)
            kkq = _dot_nt(k, jnp.concatenate([k, k, q, q], axis=0))
            kk_dup = kkq[:, :dup]
            kq = kkq[:, dup:dup + CHUNK]
            for i in range(rep):
                col = SUBLANES * (h * n_chunks + c) + i
                vcol = (h * rep + i) * HEAD_DIM
                gam_r_dup = gam_rows[col:col + 1, :]
                gam_r = gam_r_dup[:, :CHUNK]
                gam_c = gam_cols[:, col:col + 1]
                beta = g_cols[:, col + rep:col + rep + 1]
                g_end = gam_c[CHUNK - 1:CHUNK, :]
                lmat_dup = jnp.exp(jnp.where(t_dup >= s_dup, gam_c - gam_r_dup, NEG_BIG))
                a_dups.append(jnp.where(t_dup > s_dup, beta * kk_dup * lmat_dup, 0.0))
                lmat_t = jnp.exp(jnp.where(t_idx >= s_idx, gam_r - gam_c, NEG_BIG))
                e_gam = jnp.exp(gam_c)
                v = v_ref[r0:r0 + CHUNK, vcol:vcol + HEAD_DIM].astype(F32)
                rhs = jnp.concatenate([(v * beta).astype(BF16), (kf * (beta * e_gam)).astype(BF16)], axis=1)
                k_end = kf * jnp.exp(g_end - gam_c)
                x_t = jnp.concatenate([k_end, kq * lmat_t], axis=1)
                pre[(h, c, i)] = (rhs, x_t, qf * e_gam, jnp.exp(g_end))

    inverses = _unit_lower_inverses(a_dups)

    chain = {}
    for p, key in enumerate(probs):
        rhs, x_t, qg, decay = pre[key]
        t_inv = inverses[p][:, :CHUNK].astype(BF16)
        uw = _dot(t_inv, rhs).astype(BF16)
        r = _dot(x_t.T.astype(BF16), uw)
        ktu = r[:HEAD_DIM, :HEAD_DIM]
        ktw = r[:HEAD_DIM, HEAD_DIM:]
        qku = r[HEAD_DIM:, :HEAD_DIM]
        qkw = r[HEAD_DIM:, HEAD_DIM:]
        lhs = jnp.concatenate([ktw, qg - qkw], axis=0).astype(BF16)
        chain[key] = (lhs, ktu, qku, decay)

    zero_state = jnp.zeros((HEAD_DIM, HEAD_DIM), BF16)
    states = [s_ref[j] for j in range(heads * rep)]
    for c in range(n_chunks):
        r0 = c * CHUNK
        for h in range(heads):
            lhs = jnp.concatenate([chain[(h, c, i)][0] for i in range(rep)], axis=1)
            diag = jnp.concatenate(
                [jnp.concatenate([states[h * rep + i].astype(BF16) if j == i else zero_state for j in range(rep)],
                                 axis=1) for i in range(rep)], axis=0)
            m_all = _dot(lhs, diag)
            for i in range(rep):
                _, ktu, qku, decay = chain[(h, c, i)]
                vcol = (h * rep + i) * HEAD_DIM
                m = m_all[:, i * HEAD_DIM:(i + 1) * HEAD_DIM]
                states[h * rep + i] = states[h * rep + i] * decay + ktu - m[:HEAD_DIM]
                o = m[HEAD_DIM:] + qku
                z = z_ref[r0:r0 + CHUNK, vcol:vcol + HEAD_DIM].astype(F32)
                o_ref[r0:r0 + CHUNK, vcol:vcol + HEAD_DIM] = (_rms(o, nw_ref[...]) * z).astype(BF16)
    for j in range(heads * rep):
        s_ref[j] = states[j]


def _gdn(qk, v, gates, gab_t, norm_w, *, batch, seq, ts=128, heads=8):
    n = qk.shape[0]
    nt = seq // ts
    groups = GDN_QK_HEADS // heads
    rep = GDN_V_HEADS // GDN_QK_HEADS
    blk = lambda width, col0: pl.BlockSpec((ts, width * HEAD_DIM), lambda b, g, t: (b * nt + t, g + col0))
    r = jnp.arange(CHUNK)[:, None]
    c = jnp.arange(CHUNK)[None, :]
    upper = (r <= c).astype(BF16)
    upper_dup = jnp.concatenate([upper, upper], axis=1)
    triu3 = jnp.concatenate([upper_dup] * 3, axis=0)
    tril3 = jnp.concatenate([(r >= c).astype(BF16)] * 3, axis=1)
    return pl.pallas_call(
        _gdn_body, out_shape=jax.ShapeDtypeStruct((n, GDN_V_HEADS * HEAD_DIM), BF16),
        grid=(batch, groups, nt),
        in_specs=[blk(heads, 0), blk(heads, groups), blk(heads * rep, 0), blk(heads * rep, 0),
                  pl.BlockSpec((SUBLANES * heads, ts), lambda b, g, t: (g, b * nt + t)),
                  _resident((1, HEAD_DIM)), _resident(triu3.shape), _resident(tril3.shape)],
        out_specs=blk(heads * rep, 0),
        scratch_shapes=[pltpu.VMEM((heads * rep, HEAD_DIM, HEAD_DIM), F32)],
        compiler_params=_params("arbitrary", "arbitrary", "arbitrary"), name="gated_delta",
    )(qk, qk, v, gates, gab_t, norm_w.reshape(1, HEAD_DIM), triu3, tril3)


def _merge_body(oh_ref, og_ref, gate_ref, h_ref, wbh_ref, wbg_ref, wo_ref, o_ref):
    d = h_ref.shape[1]
    yh = _dot(oh_ref[...], wbh_ref[...])
    yg = _dot(og_ref[...], wbg_ref[...])
    y = gate_ref[:, :d].astype(F32) * yh + gate_ref[:, d:].astype(F32) * yg
    o_ref[...] = h_ref[...] + _dot(y.astype(BF16), wo_ref[...])


def _merge(o_h, o_g, merge_gates, h, wbh, wbg, wo, *, tm=512):
    n, d = h.shape
    row = lambda width: pl.BlockSpec((tm, width), lambda i: (i, 0))
    return pl.pallas_call(
        _merge_body, out_shape=jax.ShapeDtypeStruct((n, d), F32), grid=(n // tm,),
        in_specs=[row(o_h.shape[1]), row(o_g.shape[1]), row(2 * d), row(d),
                  _resident(wbh.shape), _resident(wbg.shape), _resident(wo.shape)],
        out_specs=row(d),
        compiler_params=_params("arbitrary"), name="merge",
    )(o_h, o_g, merge_gates, h, wbh.astype(BF16), wbg.astype(BF16), wo.astype(BF16))


def _mixer(u, h, w_in, lb_logits, hgrn_norm, conv_w, a_log, dt_bias, gdn_norm, wbh, wbg, wo, *, layer, batch, seq):
    n, d = h.shape
    hk = HG_HEADS * HEAD_DIM
    gk = GDN_QK_HEADS * HEAD_DIM
    gv = GDN_V_HEADS * HEAD_DIM
    sizes = (hk, hk, hk, hk, gk, gk, gv, GDN_V_HEADS, GDN_V_HEADS, gv, d, d)
    offs = [0]
    for s in sizes:
        offs.append(offs[-1] + s)
    col = lambda a, b: w_in[:, offs[a]:offs[b]]
    conv_pad = jnp.zeros((SUBLANES - CONV_K, conv_w.shape[1]), F32)
    cw = jnp.concatenate([conv_w, conv_pad], axis=0)

    rep = GDN_V_HEADS // GDN_QK_HEADS
    pad = SUBLANES - 2 * rep
    regroup = lambda a, b, fill: jnp.concatenate(
        [a.reshape(-1, GDN_QK_HEADS, rep), b.reshape(-1, GDN_QK_HEADS, rep),
         jnp.full((a.shape[0], GDN_QK_HEADS, pad), fill, a.dtype)], axis=2).reshape(a.shape[0], -1)
    w_ga = w_in[:, offs[7]:offs[8]]
    w_gb = w_in[:, offs[8]:offs[9]]
    w_gab = regroup(w_ga, w_gb, 0.0)
    w_gab = jnp.pad(w_gab, ((0, 0), (0, LANES - w_gab.shape[1])))
    zeros = jnp.zeros((1, GDN_V_HEADS), F32)
    aux = jnp.concatenate([regroup(a_log.reshape(1, -1).astype(F32), zeros, 0.0),
                           regroup(dt_bias.reshape(1, -1).astype(F32), zeros, 0.0)], axis=0)
    aux = jnp.pad(aux, ((0, 0), (0, LANES - aux.shape[1])))

    w_all = jnp.concatenate([col(0, 1), col(1, 2), col(2, 3), col(9, 10), col(3, 4), col(10, 12), col(4, 6),
                             col(6, 7), w_gab], axis=1).astype(BF16)
    widths = (hk, hk, hk, gv + hk, 2 * d, 2 * gk, gv)
    q_h, lf, k_h, v_h, act_gates, merge_gates, qk_g, v_g, gab_t = _proj(
        u, w_all, lb_logits.astype(F32), cw, aux, widths, layer=layer, seq=seq)

    o_h = _hgrn(q_h, lf, k_h, v_h, act_gates, gv // HEAD_DIM, hgrn_norm, batch=batch, seq=seq)
    o_g = _gdn(qk_g, v_g, act_gates, gab_t, gdn_norm, batch=batch, seq=seq)
    return _merge(o_h, o_g, merge_gates, h, wbh, wbg, wo)


def kernel(x, ffn1_norm, ffn1_w_in, ffn1_w_out, mix_norm, w_in, hgrn_lb_logits, hgrn_out_norm, gdn_conv_w,
           gdn_a_log, gdn_dt_bias, gdn_out_norm, w_branch_hgrn, w_branch_gdn, w_out, ffn2_norm, ffn2_w_in,
           ffn2_w_out, final_norm):
    batch, seq, d = x.shape
    depth = ffn1_norm.shape[0]
    h = x.reshape(batch * seq, d)
    for l in range(depth):
        h, u = _ffn(h, ffn1_norm[l], ffn1_w_in[l], ffn1_w_out[l], mix_norm[l], final=False)
        h = _mixer(u, h, w_in[l], hgrn_lb_logits, hgrn_out_norm[l], gdn_conv_w[l], gdn_a_log[l],
                   gdn_dt_bias[l], gdn_out_norm[l], w_branch_hgrn[l], w_branch_gdn[l], w_out[l],
                   layer=l, batch=batch, seq=seq)
        last = l == depth - 1
        nxt = final_norm if last else ffn1_norm[l + 1]
        out = _ffn(h, ffn2_norm[l], ffn2_w_in[l], ffn2_w_out[l], nxt, final=last)
        h = out if last else out[0]
    return h.reshape(batch, seq, d)
```

```python
import functools

import jax
import jax.numpy as jnp
from jax import lax
from jax.experimental import pallas as pl
from jax.experimental.pallas import tpu as pltpu

F32 = jnp.float32
BF16 = jnp.bfloat16

EPS = 1e-6
CHUNK = 64
SUB = 8
HEAD_DIM = 128
HG_HEADS = 8
GDN_QK_HEADS = 8
GDN_V_HEADS = 16
CONV_K = 4
LANES = 128
SUBLANES = 8
NEG_BIG = -1e30
VMEM_LIMIT = 56 * 1024 * 1024


def _dot(a, b):
    return jnp.dot(a, b, preferred_element_type=F32)


def _dot_nt(a, b):
    return lax.dot_general(a, b, (((1,), (1,)), ((), ())), preferred_element_type=F32)


def _dot_tn(a, b):
    return lax.dot_general(a, b, (((0,), (0,)), ((), ())), preferred_element_type=F32)


def _sigmoid(x):
    return 1.0 / (1.0 + jnp.exp(-x))


def _rms(x, w):
    ms = jnp.mean(x * x, axis=-1, keepdims=True)
    return x * lax.rsqrt(ms + EPS) * w


def _params(*sem):
    return pltpu.CompilerParams(dimension_semantics=sem, vmem_limit_bytes=VMEM_LIMIT)


def _resident(shape):
    return pl.BlockSpec(shape, lambda *_: (0,) * len(shape), pipeline_mode=pl.Buffered(1))


def _ffn_body(h_ref, nw_ref, wa_ref, wb_ref, wo_ref, onw_ref, *out_refs, ff_tile, final):
    h = h_ref[...]
    xn = _rms(h, nw_ref[...]).astype(BF16)
    d_ff = wa_ref.shape[1]
    acc = jnp.zeros(h.shape, F32)
    for j in range(0, d_ff, ff_tile):
        w = min(ff_tile, d_ff - j)
        a = _dot(xn, wa_ref[:, j:j + w])
        b = _dot(xn, wb_ref[:, j:j + w])
        g = (a * _sigmoid(a) * b).astype(BF16)
        acc = acc + _dot(g, wo_ref[j:j + w, :])
    hn = h + 0.5 * acc
    if final:
        out_refs[0][...] = _rms(hn, onw_ref[...])
    else:
        out_refs[0][...] = hn
        out_refs[1][...] = _rms(hn, onw_ref[...]).astype(BF16)


def _ffn(h, norm_w, w_in, w_out, next_norm_w, *, final, tm=512, ff_tile=512):
    n, d = h.shape
    d_ff = w_out.shape[0]
    wa = w_in[:, :d_ff].astype(BF16)
    wb = w_in[:, d_ff:].astype(BF16)
    wo = w_out.astype(BF16)
    row = pl.BlockSpec((tm, d), lambda i: (i, 0))
    if final:
        out_shape = jax.ShapeDtypeStruct((n, d), F32)
        out_specs = row
    else:
        out_shape = (jax.ShapeDtypeStruct((n, d), F32), jax.ShapeDtypeStruct((n, d), BF16))
        out_specs = (row, row)
    return pl.pallas_call(
        functools.partial(_ffn_body, ff_tile=ff_tile, final=final),
        out_shape=out_shape,
        grid=(n // tm,),
        in_specs=[row, _resident((1, d)), _resident((d, d_ff)), _resident((d, d_ff)),
                  _resident((d_ff, d)), _resident((1, d))],
        out_specs=out_specs,
        compiler_params=_params("arbitrary"),
        name="ffn_final" if final else "ffn",
    )(h, norm_w.reshape(1, d), wa, wb, wo, next_norm_w.reshape(1, d))


def _proj_body(u_ref, w_ref, logit_ref, cw_ref, gaux_ref,
               qh_ref, lf_ref, kh_ref, vh_ref, act_ref, mg_ref, qk_ref, vg_ref, gab_ref, ext_ref,
               *, layer, tiles_per_seq, tn, widths):
    w_hq, w_hf, w_hi, w_act, w_mg, w_qk, w_gv = widths
    u = u_ref[...]
    tm = u.shape[0]
    tiles = lambda width: range(0, width, tn)
    proj = lambda c0, width=tn: _dot(u, w_ref[:, c0:c0 + width])
    base = 0

    for o in tiles(w_hq):
        p = proj(base + o)
        qh_ref[:, o:o + tn] = (p * _sigmoid(p) * HEAD_DIM ** -0.5).astype(BF16)
    base += w_hq

    lg = logit_ref[...]
    e = jnp.exp(lg - jnp.max(lg, axis=0, keepdims=True))
    lb_all = jnp.sum(e[0:layer + 1, :], axis=0, keepdims=True) / jnp.sum(e, axis=0, keepdims=True)
    for o in tiles(w_hf):
        s = _sigmoid(proj(base + o))
        lb = lb_all[:, o:o + tn]
        lf_ref[:, o:o + tn] = jnp.log(lb + (1.0 - lb) * s)
        kh_ref[:, o:o + tn] = ((1.0 - lb) * (1.0 - s)).astype(BF16)
    base += w_hf

    for o in tiles(w_hi):
        vh_ref[:, o:o + tn] = proj(base + o).astype(BF16)
    base += w_hi

    for o in tiles(w_act):
        p = proj(base + o)
        act_ref[:, o:o + tn] = (p * _sigmoid(p)).astype(BF16)
    base += w_act

    for o in tiles(w_mg):
        mg_ref[:, o:o + tn] = _sigmoid(proj(base + o)).astype(BF16)
    base += w_mg

    @pl.when(pl.program_id(0) % tiles_per_seq == 0)
    def _():
        ext_ref[0:SUBLANES, :] = jnp.zeros((SUBLANES, ext_ref.shape[1]), F32)

    for o in tiles(w_qk + w_gv):
        p = proj(base + o)
        ext_ref[SUBLANES:, o:o + tn] = p
        acc = p * cw_ref[CONV_K - 1:CONV_K, o:o + tn]
        for d in range(1, CONV_K):
            acc = acc + ext_ref[SUBLANES - d:SUBLANES - d + tm, o:o + tn] * cw_ref[CONV_K - 1 - d:CONV_K - d, o:o + tn]
        y = acc * _sigmoid(acc)
        if o < w_qk:
            scale = HEAD_DIM ** -0.5 if o < w_qk // 2 else 1.0
            for s in range(0, tn, HEAD_DIM):
                ys = y[:, s:s + HEAD_DIM]
                inv = lax.rsqrt(jnp.sum(ys * ys, axis=-1, keepdims=True) + EPS) * scale
                qk_ref[:, o + s:o + s + HEAD_DIM] = (ys * inv).astype(BF16)
        else:
            vg_ref[:, o - w_qk:o - w_qk + tn] = y.astype(BF16)
    ext_ref[0:SUBLANES, :] = ext_ref[tm:tm + SUBLANES, :]
    base += w_qk + w_gv

    p = proj(base, LANES)
    x = p + gaux_ref[1:2, :]
    softplus = jnp.maximum(x, 0.0) + jnp.log1p(jnp.exp(-jnp.abs(x)))
    g = -jnp.exp(gaux_ref[0:1, :]) * softplus
    lane = lax.broadcasted_iota(jnp.int32, p.shape, 1)
    gab_ref[...] = jnp.where(lane % SUBLANES < 2, g, _sigmoid(p)).T


def _proj(u, w_all, lb_logits, cw, gaux, widths, *, layer, seq, tm=256, tn=512):
    n, d = u.shape
    w_hq, w_hf, w_hi, w_act, w_mg, w_qk, w_gv = widths
    row = lambda width: pl.BlockSpec((tm, width), lambda i: (i, 0))
    out_widths = (w_hq, w_hf, w_hf, w_hi, w_act, w_mg, w_qk, w_gv)
    out_dtypes = (BF16, F32, BF16, BF16, BF16, BF16, BF16, BF16)
    out_shape = tuple(jax.ShapeDtypeStruct((n, w), dt) for w, dt in zip(out_widths, out_dtypes))
    out_shape += (jax.ShapeDtypeStruct((LANES, n), F32),)
    out_specs = tuple(row(w) for w in out_widths) + (pl.BlockSpec((LANES, tm), lambda i: (0, i)),)
    return pl.pallas_call(
        functools.partial(_proj_body, layer=layer, tiles_per_seq=seq // tm, tn=tn, widths=widths),
        out_shape=out_shape, grid=(n // tm,),
        in_specs=[row(d), _resident(w_all.shape), _resident(lb_logits.shape), _resident(cw.shape),
                  _resident(gaux.shape)],
        out_specs=out_specs,
        scratch_shapes=[pltpu.VMEM((tm + SUBLANES, w_qk + w_gv), F32)],
        compiler_params=_params("arbitrary"), name="proj",
    )(u, w_all, lb_logits, cw, gaux)


def _hgrn_body(q_ref, lf_ref, k_ref, v_ref, gate_ref, nw_ref, tri_ref, o_ref, st_ref):
    @pl.when(pl.program_id(2) == 0)
    def _():
        st_ref[...] = jnp.zeros_like(st_ref)

    ts = q_ref.shape[0]
    n_chunks = ts // CHUNK
    lf = lf_ref[...]
    hi = lf.astype(BF16)
    r1 = lf - hi.astype(F32)
    mid = r1.astype(BF16)
    lo = (r1 - mid.astype(F32)).astype(BF16)
    by_chunk = lambda x: jnp.concatenate([x[c * CHUNK:(c + 1) * CHUNK] for c in range(n_chunks)], axis=1)
    b_all = _dot(tri_ref[...], jnp.concatenate([by_chunk(hi), by_chunk(mid), by_chunk(lo)], axis=0))

    row = lax.broadcasted_iota(jnp.int32, (CHUNK, HEAD_DIM), 0)
    t_idx = lax.broadcasted_iota(jnp.int32, (CHUNK, CHUNK), 0)
    s_idx = lax.broadcasted_iota(jnp.int32, (CHUNK, CHUNK), 1)
    levels = []
    half = CHUNK // 2
    while half >= SUB:
        size = 2 * half
        levels.append((half, row % size >= half,
                       (t_idx // size == s_idx // size) & (t_idx % size >= half) & (s_idx % size < half)))
        half //= 2
    causal_in_block = [row % SUB >= j for j in range(SUB)]
    place = [s_idx == (t_idx // SUB) * SUB + j for j in range(SUB)]
    for c in range(n_chunks):
        r0 = c * CHUNK
        b = b_all[:, c * HEAD_DIM:(c + 1) * HEAD_DIM]
        q = q_ref[r0:r0 + CHUNK, :].astype(F32)
        k = k_ref[r0:r0 + CHUNK, :].astype(F32)
        v = v_ref[r0:r0 + CHUNK, :]
        b_end = b[CHUNK - 1:CHUNK, :]
        st = st_ref[...]
        inter = _dot_nt((q * jnp.exp(b)).astype(BF16), st.astype(BF16))

        scores = jnp.zeros((CHUNK, CHUNK), F32)
        for half, past_mid, pair in levels:
            size = 2 * half
            b_mid = jnp.concatenate(
                [jnp.broadcast_to(b[m + half - 1:m + half, :], (size, HEAD_DIM)) for m in range(0, CHUNK, size)], axis=0)
            x = (jnp.where(past_mid, q, k) * jnp.exp(-jnp.abs(b - b_mid))).astype(BF16)
            scores = jnp.where(pair, _dot_nt(x, x), scores)

        blocks = lambda x: x.reshape(CHUNK // SUB, SUB, HEAD_DIM)
        for j in range(SUB):
            pick = lambda x: jnp.broadcast_to(blocks(x)[:, j:j + 1, :], blocks(x).shape).reshape(CHUNK, HEAD_DIM)
            dec = jnp.exp(jnp.where(causal_in_block[j], b - pick(b), NEG_BIG))
            col = jnp.sum(q * pick(k) * dec, axis=-1, keepdims=True)
            scores = jnp.where(place[j], col, scores)

        o = inter + _dot(scores.astype(BF16), v)

        k_end = (k * jnp.exp(b_end - b)).astype(BF16)
        st_ref[...] = st * jnp.exp(b_end) + _dot_tn(v, k_end)

        o_ref[r0:r0 + CHUNK, :] = (_rms(o, nw_ref[...]) * gate_ref[r0:r0 + CHUNK, :].astype(F32)).astype(BF16)


def _hgrn(q, lf, k, v, gates, gate_col0, norm_w, *, batch, seq, ts=512):
    n = q.shape[0]
    nt = seq // ts
    blk = lambda col0: pl.BlockSpec((ts, HEAD_DIM), lambda b, h, t: (b * nt + t, h + col0))
    tri = (jnp.arange(CHUNK)[:, None] >= jnp.arange(CHUNK)[None, :]).astype(BF16)
    tri3 = jnp.concatenate([tri, tri, tri], axis=1)
    return pl.pallas_call(
        _hgrn_body, out_shape=jax.ShapeDtypeStruct((n, HG_HEADS * HEAD_DIM), BF16),
        grid=(batch, HG_HEADS, nt),
        in_specs=[blk(0), blk(0), blk(0), blk(0), blk(gate_col0), _resident((1, HEAD_DIM)),
                  _resident((CHUNK, 3 * CHUNK))],
        out_specs=blk(0),
        scratch_shapes=[pltpu.VMEM((HEAD_DIM, HEAD_DIM), F32)],
        compiler_params=_params("arbitrary", "arbitrary", "arbitrary"), name="hgrn2",
    )(q, lf, k, v, gates, norm_w.reshape(1, HEAD_DIM), tri3)


def _split3_bf16(x):
    hi = x.astype(BF16)
    r1 = x - hi.astype(F32)
    mid = r1.astype(BF16)
    lo = (r1 - mid.astype(F32)).astype(BF16)
    return hi, mid, lo


def _unit_lower_inverse_pairs(a_pairs):
    shape = a_pairs[0].shape
    row = lax.broadcasted_iota(jnp.int32, shape, 0)
    lane = lax.broadcasted_iota(jnp.int32, shape, 1)
    first = lane < CHUNK
    eye = jnp.where(row == lane % CHUNK, 1.0, 0.0)

    def level(power, partial):
        top = jnp.concatenate([jnp.where(first, power, 0.0), jnp.where(first, partial, 0.0)], axis=1)
        bottom = jnp.concatenate([jnp.where(first, 0.0, power), jnp.where(first, 0.0, partial)], axis=1)
        res = _dot(power.astype(BF16), jnp.concatenate([top, bottom], axis=0).astype(BF16))
        return res[:, :2 * CHUNK], partial + res[:, 2 * CHUNK:]

    state = [level(-a, eye) for a in a_pairs]
    span = 2
    while span < CHUNK:
        state = [level(power, partial) for power, partial in state]
        span *= 2
    return [partial for _, partial in state]


def _gdn_body(q_ref, k_ref, v_ref, z_ref, gb_ref, nw_ref, triu_ref, tril_ref, o_ref, s_ref):
    @pl.when(pl.program_id(2) == 0)
    def _():
        s_ref[...] = jnp.zeros_like(s_ref)

    ts = q_ref.shape[0]
    n_chunks = ts // CHUNK
    heads = q_ref.shape[1] // HEAD_DIM
    rep = s_ref.shape[0] // heads
    dup = 2 * CHUNK
    g_stack = jnp.concatenate([gb_ref[SUBLANES * h:SUBLANES * (h + 1), c * CHUNK:(c + 1) * CHUNK]
                               for h in range(heads) for c in range(n_chunks)], axis=0)
    g_cols = jnp.concatenate([g_stack, jnp.zeros_like(g_stack)], axis=1).T[:CHUNK]
    gam_rows = _dot(jnp.concatenate(_split3_bf16(g_stack), axis=1), triu_ref[...])
    gam_cols = _dot(tril_ref[...], jnp.concatenate(_split3_bf16(g_cols), axis=0))
    t_dup = lax.broadcasted_iota(jnp.int32, (CHUNK, dup), 0)
    s_dup_raw = lax.broadcasted_iota(jnp.int32, (CHUNK, dup), 1)
    s_dup = s_dup_raw % CHUNK
    s_idx = lax.broadcasted_iota(jnp.int32, (CHUNK, CHUNK), 0)
    t_idx = lax.broadcasted_iota(jnp.int32, (CHUNK, CHUNK), 1)

    pairs = [(h, c) for h in range(heads) for c in range(n_chunks)]
    first = s_dup_raw < CHUNK
    a_pairs, pre = [], {}
    for h, c in pairs:
        r0 = c * CHUNK
        q = q_ref[r0:r0 + CHUNK, h * HEAD_DIM:(h + 1) * HEAD_DIM]
        k = k_ref[r0:r0 + CHUNK, h * HEAD_DIM:(h + 1) * HEAD_DIM]
        qf = q.astype(F32)
        kf = k.astype(F32)
        kkq = _dot_nt(k, jnp.concatenate([k, k, q, q], axis=0))
        kk_dup = kkq[:, :dup]
        kq = kkq[:, dup:dup + CHUNK]
        col0 = SUBLANES * (h * n_chunks + c)
        gam_c = [gam_cols[:, col0 + i:col0 + i + 1] for i in range(rep)]
        beta = [g_cols[:, col0 + rep + i:col0 + rep + i + 1] for i in range(rep)]
        gam_r_pair = jnp.where(first[0:1], gam_rows[col0:col0 + 1, :], gam_rows[col0 + 1:col0 + 2, :])
        lmat_pair = jnp.exp(jnp.where(t_dup >= s_dup, jnp.where(first, gam_c[0], gam_c[1]) - gam_r_pair, NEG_BIG))
        a_pairs.append(jnp.where(t_dup > s_dup, jnp.where(first, beta[0], beta[1]) * kk_dup * lmat_pair, 0.0))
        for i in range(rep):
            vcol = (h * rep + i) * HEAD_DIM
            gam_r = gam_rows[col0 + i:col0 + i + 1, :CHUNK]
            g_end = gam_c[i][CHUNK - 1:CHUNK, :]
            lmat_t = jnp.exp(jnp.where(t_idx >= s_idx, gam_r - gam_c[i], NEG_BIG))
            e_gam = jnp.exp(gam_c[i])
            v = v_ref[r0:r0 + CHUNK, vcol:vcol + HEAD_DIM].astype(F32)
            rhs = jnp.concatenate([(v * beta[i]).astype(BF16), (kf * (beta[i] * e_gam)).astype(BF16)], axis=1)
            k_end = kf * jnp.exp(g_end - gam_c[i])
            x_t = jnp.concatenate([k_end, kq * lmat_t], axis=1)
            pre[(h, c, i)] = (rhs, x_t, qf * e_gam, jnp.exp(g_end))

    inverses = _unit_lower_inverse_pairs(a_pairs)

    chain = {}
    for p, (h, c) in enumerate(pairs):
        inv_pair = inverses[p].astype(BF16)
        for i in range(rep):
            rhs, x_t, qg, decay = pre[(h, c, i)]
            pad = jnp.zeros_like(rhs)
            uw = _dot(inv_pair, jnp.concatenate([rhs, pad] if i == 0 else [pad, rhs], axis=0)).astype(BF16)
            r = _dot(x_t.T.astype(BF16), uw)
            ktu = r[:HEAD_DIM, :HEAD_DIM]
            ktw = r[:HEAD_DIM, HEAD_DIM:]
            qku = r[HEAD_DIM:, :HEAD_DIM]
            qkw = r[HEAD_DIM:, HEAD_DIM:]
            lhs = jnp.concatenate([ktw, qg - qkw], axis=0).astype(BF16)
            chain[(h, c, i)] = (lhs, ktu, qku, decay)

    zero_state = jnp.zeros((HEAD_DIM, HEAD_DIM), BF16)
    states = [s_ref[j] for j in range(heads * rep)]
    for c in range(n_chunks):
        r0 = c * CHUNK
        for h in range(heads):
            lhs = jnp.concatenate([chain[(h, c, i)][0] for i in range(rep)], axis=1)
            diag = jnp.concatenate(
                [jnp.concatenate([states[h * rep + i].astype(BF16) if j == i else zero_state for j in range(rep)],
                                 axis=1) for i in range(rep)], axis=0)
            m_all = _dot(lhs, diag)
            for i in range(rep):
                _, ktu, qku, decay = chain[(h, c, i)]
                vcol = (h * rep + i) * HEAD_DIM
                m = m_all[:, i * HEAD_DIM:(i + 1) * HEAD_DIM]
                states[h * rep + i] = states[h * rep + i] * decay + ktu - m[:HEAD_DIM]
                o = m[HEAD_DIM:] + qku
                z = z_ref[r0:r0 + CHUNK, vcol:vcol + HEAD_DIM].astype(F32)
                o_ref[r0:r0 + CHUNK, vcol:vcol + HEAD_DIM] = (_rms(o, nw_ref[...]) * z).astype(BF16)
    for j in range(heads * rep):
        s_ref[j] = states[j]


def _gdn(qk, v, gates, gab_t, norm_w, *, batch, seq, ts=128, heads=8):
    n = qk.shape[0]
    nt = seq // ts
    groups = GDN_QK_HEADS // heads
    rep = GDN_V_HEADS // GDN_QK_HEADS
    blk = lambda width, col0: pl.BlockSpec((ts, width * HEAD_DIM), lambda b, g, t: (b * nt + t, g + col0))
    r = jnp.arange(CHUNK)[:, None]
    c = jnp.arange(CHUNK)[None, :]
    upper = (r <= c).astype(BF16)
    upper_dup = jnp.concatenate([upper, upper], axis=1)
    triu3 = jnp.concatenate([upper_dup] * 3, axis=0)
    tril3 = jnp.concatenate([(r >= c).astype(BF16)] * 3, axis=1)
    return pl.pallas_call(
        _gdn_body, out_shape=jax.ShapeDtypeStruct((n, GDN_V_HEADS * HEAD_DIM), BF16),
        grid=(batch, groups, nt),
        in_specs=[blk(heads, 0), blk(heads, groups), blk(heads * rep, 0), blk(heads * rep, 0),
                  pl.BlockSpec((SUBLANES * heads, ts), lambda b, g, t: (g, b * nt + t)),
                  _resident((1, HEAD_DIM)), _resident(triu3.shape), _resident(tril3.shape)],
        out_specs=blk(heads * rep, 0),
        scratch_shapes=[pltpu.VMEM((heads * rep, HEAD_DIM, HEAD_DIM), F32)],
        compiler_params=_params("arbitrary", "arbitrary", "arbitrary"), name="gated_delta",
    )(qk, qk, v, gates, gab_t, norm_w.reshape(1, HEAD_DIM), triu3, tril3)


def _merge_body(oh_ref, og_ref, gate_ref, h_ref, wbh_ref, wbg_ref, wo_ref, o_ref):
    d = h_ref.shape[1]
    yh = _dot(oh_ref[...], wbh_ref[...])
    yg = _dot(og_ref[...], wbg_ref[...])
    y = gate_ref[:, :d].astype(F32) * yh + gate_ref[:, d:].astype(F32) * yg
    o_ref[...] = h_ref[...] + _dot(y.astype(BF16), wo_ref[...])


def _merge(o_h, o_g, merge_gates, h, wbh, wbg, wo, *, tm=512):
    n, d = h.shape
    row = lambda width: pl.BlockSpec((tm, width), lambda i: (i, 0))
    return pl.pallas_call(
        _merge_body, out_shape=jax.ShapeDtypeStruct((n, d), F32), grid=(n // tm,),
        in_specs=[row(o_h.shape[1]), row(o_g.shape[1]), row(2 * d), row(d),
                  _resident(wbh.shape), _resident(wbg.shape), _resident(wo.shape)],
        out_specs=row(d),
        compiler_params=_params("arbitrary"), name="merge",
    )(o_h, o_g, merge_gates, h, wbh.astype(BF16), wbg.astype(BF16), wo.astype(BF16))


def _mixer(u, h, w_in, lb_logits, hgrn_norm, conv_w, a_log, dt_bias, gdn_norm, wbh, wbg, wo, *, layer, batch, seq):
    n, d = h.shape
    hk = HG_HEADS * HEAD_DIM
    gk = GDN_QK_HEADS * HEAD_DIM
    gv = GDN_V_HEADS * HEAD_DIM
    sizes = (hk, hk, hk, hk, gk, gk, gv, GDN_V_HEADS, GDN_V_HEADS, gv, d, d)
    offs = [0]
    for s in sizes:
        offs.append(offs[-1] + s)
    col = lambda a, b: w_in[:, offs[a]:offs[b]]
    conv_pad = jnp.zeros((SUBLANES - CONV_K, conv_w.shape[1]), F32)
    cw = jnp.concatenate([conv_w, conv_pad], axis=0)

    rep = GDN_V_HEADS // GDN_QK_HEADS
    pad = SUBLANES - 2 * rep
    regroup = lambda a, b, fill: jnp.concatenate(
        [a.reshape(-1, GDN_QK_HEADS, rep), b.reshape(-1, GDN_QK_HEADS, rep),
         jnp.full((a.shape[0], GDN_QK_HEADS, pad), fill, a.dtype)], axis=2).reshape(a.shape[0], -1)
    w_ga = w_in[:, offs[7]:offs[8]]
    w_gb = w_in[:, offs[8]:offs[9]]
    w_gab = regroup(w_ga, w_gb, 0.0)
    w_gab = jnp.pad(w_gab, ((0, 0), (0, LANES - w_gab.shape[1])))
    zeros = jnp.zeros((1, GDN_V_HEADS), F32)
    aux = jnp.concatenate([regroup(a_log.reshape(1, -1).astype(F32), zeros, 0.0),
                           regroup(dt_bias.reshape(1, -1).astype(F32), zeros, 0.0)], axis=0)
    aux = jnp.pad(aux, ((0, 0), (0, LANES - aux.shape[1])))

    w_all = jnp.concatenate([col(0, 1), col(1, 2), col(2, 3), col(9, 10), col(3, 4), col(10, 12), col(4, 6),
                             col(6, 7), w_gab], axis=1).astype(BF16)
    widths = (hk, hk, hk, gv + hk, 2 * d, 2 * gk, gv)
    q_h, lf, k_h, v_h, act_gates, merge_gates, qk_g, v_g, gab_t = _proj(
        u, w_all, lb_logits.astype(F32), cw, aux, widths, layer=layer, seq=seq)

    o_h = _hgrn(q_h, lf, k_h, v_h, act_gates, gv // HEAD_DIM, hgrn_norm, batch=batch, seq=seq)
    o_g = _gdn(qk_g, v_g, act_gates, gab_t, gdn_norm, batch=batch, seq=seq)
    return _merge(o_h, o_g, merge_gates, h, wbh, wbg, wo)


def kernel(x, ffn1_norm, ffn1_w_in, ffn1_w_out, mix_norm, w_in, hgrn_lb_logits, hgrn_out_norm, gdn_conv_w,
           gdn_a_log, gdn_dt_bias, gdn_out_norm, w_branch_hgrn, w_branch_gdn, w_out, ffn2_norm, ffn2_w_in,
           ffn2_w_out, final_norm):
    batch, seq, d = x.shape
    depth = ffn1_norm.shape[0]
    h = x.reshape(batch * seq, d)
    for l in range(depth):
        h, u = _ffn(h, ffn1_norm[l], ffn1_w_in[l], ffn1_w_out[l], mix_norm[l], final=False)
        h = _mixer(u, h, w_in[l], hgrn_lb_logits, hgrn_out_norm[l], gdn_conv_w[l], gdn_a_log[l],
                   gdn_dt_bias[l], gdn_out_norm[l], w_branch_hgrn[l], w_branch_gdn[l], w_out[l],
                   layer=l, batch=batch, seq=seq)
        last = l == depth - 1
        nxt = final_norm if last else ffn1_norm[l + 1]
        out = _ffn(h, ffn2_norm[l], ffn2_w_in[l], ffn2_w_out[l], nxt, final=last)
        h = out if last else out[0]
    return h.reshape(batch, seq, d)
```

```python
import functools

import jax
import jax.numpy as jnp
from jax import lax
from jax.experimental import pallas as pl
from jax.experimental.pallas import tpu as pltpu

F32 = jnp.float32
BF16 = jnp.bfloat16

EPS = 1e-6
CHUNK = 64
SUB = 8
HEAD_DIM = 128
HG_HEADS = 8
GDN_QK_HEADS = 8
GDN_V_HEADS = 16
CONV_K = 4
LANES = 128
SUBLANES = 8
NEG_BIG = -1e30
LOG2E = 1.4426950408889634
VMEM_LIMIT = 56 * 1024 * 1024


def _dot(a, b):
    return jnp.dot(a, b, preferred_element_type=F32)


def _dot_nt(a, b):
    return lax.dot_general(a, b, (((1,), (1,)), ((), ())), preferred_element_type=F32)


def _dot_tn(a, b):
    return lax.dot_general(a, b, (((0,), (0,)), ((), ())), preferred_element_type=F32)


def _sigmoid(x):
    return 1.0 / (1.0 + jnp.exp2(x * -LOG2E))


def _rms(x, w):
    ms = jnp.mean(x * x, axis=-1, keepdims=True)
    return x * lax.rsqrt(ms + EPS) * w


def _params(*sem):
    return pltpu.CompilerParams(dimension_semantics=sem, vmem_limit_bytes=VMEM_LIMIT)


def _resident(shape):
    return pl.BlockSpec(shape, lambda *_: (0,) * len(shape), pipeline_mode=pl.Buffered(1))


def _ffn_body(h_ref, nw_ref, wa_ref, wb_ref, wo_ref, onw_ref, *out_refs, ff_tile, final):
    h = h_ref[...]
    xn = _rms(h, nw_ref[...]).astype(BF16)
    d_ff = wa_ref.shape[1]
    acc = jnp.zeros(h.shape, F32)
    for j in range(0, d_ff, ff_tile):
        w = min(ff_tile, d_ff - j)
        a = _dot(xn, wa_ref[:, j:j + w])
        b = _dot(xn, wb_ref[:, j:j + w])
        g = (a * _sigmoid(a) * b).astype(BF16)
        acc = acc + _dot(g, wo_ref[j:j + w, :])
    hn = h + 0.5 * acc
    if final:
        out_refs[0][...] = _rms(hn, onw_ref[...])
    else:
        out_refs[0][...] = hn
        out_refs[1][...] = _rms(hn, onw_ref[...]).astype(BF16)


def _ffn(h, norm_w, w_in, w_out, next_norm_w, *, final, tm=512, ff_tile=512):
    n, d = h.shape
    d_ff = w_out.shape[0]
    wa = w_in[:, :d_ff].astype(BF16)
    wb = w_in[:, d_ff:].astype(BF16)
    wo = w_out.astype(BF16)
    row = pl.BlockSpec((tm, d), lambda i: (i, 0))
    if final:
        out_shape = jax.ShapeDtypeStruct((n, d), F32)
        out_specs = row
    else:
        out_shape = (jax.ShapeDtypeStruct((n, d), F32), jax.ShapeDtypeStruct((n, d), BF16))
        out_specs = (row, row)
    return pl.pallas_call(
        functools.partial(_ffn_body, ff_tile=ff_tile, final=final),
        out_shape=out_shape,
        grid=(n // tm,),
        in_specs=[row, _resident((1, d)), _resident((d, d_ff)), _resident((d, d_ff)),
                  _resident((d_ff, d)), _resident((1, d))],
        out_specs=out_specs,
        compiler_params=_params("arbitrary"),
        name="ffn_final" if final else "ffn",
    )(h, norm_w.reshape(1, d), wa, wb, wo, next_norm_w.reshape(1, d))


def _proj_body(u_ref, w_ref, logit_ref, cw_ref, gaux_ref,
               qh_ref, lf_ref, kh_ref, vh_ref, act_ref, mg_ref, qk_ref, vg_ref, gab_ref, ext_ref,
               *, layer, tiles_per_seq, tn, widths):
    w_hq, w_hf, w_hi, w_act, w_mg, w_qk, w_gv = widths
    u = u_ref[...]
    tm = u.shape[0]
    proj = lambda c0, width=tn: _dot(u, w_ref[:, c0:c0 + width])
    bases = [0]
    for w in widths:
        bases.append(bases[-1] + w)

    def hq_tile(o):
        p = proj(bases[0] + o)
        qh_ref[:, o:o + tn] = (p * _sigmoid(p) * HEAD_DIM ** -0.5).astype(BF16)

    lg = logit_ref[...]
    e = jnp.exp(lg - jnp.max(lg, axis=0, keepdims=True))
    lb_all = jnp.sum(e[0:layer + 1, :], axis=0, keepdims=True) / jnp.sum(e, axis=0, keepdims=True)

    def hf_tile(o):
        s = _sigmoid(proj(bases[1] + o))
        lb = lb_all[:, o:o + tn]
        lf_ref[:, o:o + tn] = jnp.log2(lb + (1.0 - lb) * s)
        kh_ref[:, o:o + tn] = ((1.0 - lb) * (1.0 - s)).astype(BF16)

    def hi_tile(o):
        vh_ref[:, o:o + tn] = proj(bases[2] + o).astype(BF16)

    def act_tile(o):
        p = proj(bases[3] + o)
        act_ref[:, o:o + tn] = (p * _sigmoid(p)).astype(BF16)

    def mg_tile(o):
        mg_ref[:, o:o + tn] = _sigmoid(proj(bases[4] + o)).astype(BF16)

    @pl.when(pl.program_id(0) % tiles_per_seq == 0)
    def _():
        ext_ref[0:SUBLANES, :] = jnp.zeros((SUBLANES, ext_ref.shape[1]), F32)

    def conv_tile(o):
        p = proj(bases[5] + o)
        ext_ref[SUBLANES:, o:o + tn] = p
        acc = p * cw_ref[CONV_K - 1:CONV_K, o:o + tn]
        for d in range(1, CONV_K):
            acc = acc + ext_ref[SUBLANES - d:SUBLANES - d + tm, o:o + tn] * cw_ref[CONV_K - 1 - d:CONV_K - d, o:o + tn]
        y = acc * _sigmoid(acc)
        if o < w_qk:
            scale = HEAD_DIM ** -0.5 if o < w_qk // 2 else 1.0
            for s in range(0, tn, HEAD_DIM):
                ys = y[:, s:s + HEAD_DIM]
                inv = lax.rsqrt(jnp.sum(ys * ys, axis=-1, keepdims=True) + EPS) * scale
                qk_ref[:, o + s:o + s + HEAD_DIM] = (ys * inv).astype(BF16)
        else:
            vg_ref[:, o - w_qk:o - w_qk + tn] = y.astype(BF16)

    light = [(f, o) for f, w in ((hq_tile, w_hq), (hf_tile, w_hf), (hi_tile, w_hi), (act_tile, w_act), (mg_tile, w_mg))
             for o in range(0, w, tn)]
    heavy = [(conv_tile, o) for o in range(0, w_qk + w_gv, tn)]
    per_heavy = -(-len(light) // len(heavy))
    while light or heavy:
        for f, o in light[:per_heavy]:
            f(o)
        light = light[per_heavy:]
        if heavy:
            f, o = heavy.pop(0)
            f(o)
    ext_ref[0:SUBLANES, :] = ext_ref[tm:tm + SUBLANES, :]

    p = proj(bases[7], LANES)
    x = p + gaux_ref[1:2, :]
    softplus = jnp.maximum(x, 0.0) + jnp.log1p(jnp.exp(-jnp.abs(x)))
    g = -jnp.exp(gaux_ref[0:1, :]) * softplus
    lane = lax.broadcasted_iota(jnp.int32, p.shape, 1)
    gab_ref[...] = jnp.where(lane % SUBLANES < 2, g, _sigmoid(p)).T


def _proj(u, w_all, lb_logits, cw, gaux, widths, *, layer, seq, tm=256, tn=256):
    n, d = u.shape
    w_hq, w_hf, w_hi, w_act, w_mg, w_qk, w_gv = widths
    row = lambda width: pl.BlockSpec((tm, width), lambda i: (i, 0))
    out_widths = (w_hq, w_hf, w_hf, w_hi, w_act, w_mg, w_qk, w_gv)
    out_dtypes = (BF16, F32, BF16, BF16, BF16, BF16, BF16, BF16)
    out_shape = tuple(jax.ShapeDtypeStruct((n, w), dt) for w, dt in zip(out_widths, out_dtypes))
    out_shape += (jax.ShapeDtypeStruct((LANES, n), F32),)
    out_specs = tuple(row(w) for w in out_widths) + (pl.BlockSpec((LANES, tm), lambda i: (0, i)),)
    return pl.pallas_call(
        functools.partial(_proj_body, layer=layer, tiles_per_seq=seq // tm, tn=tn, widths=widths),
        out_shape=out_shape, grid=(n // tm,),
        in_specs=[row(d), _resident(w_all.shape), _resident(lb_logits.shape), _resident(cw.shape),
                  _resident(gaux.shape)],
        out_specs=out_specs,
        scratch_shapes=[pltpu.VMEM((tm + SUBLANES, w_qk + w_gv), F32)],
        compiler_params=_params("arbitrary"), name="proj",
    )(u, w_all, lb_logits, cw, gaux)


def _hgrn_body(q_ref, lf_ref, k_ref, v_ref, gate_ref, nw_ref, tri_ref, o_ref, st_ref):
    @pl.when(pl.program_id(2) == 0)
    def _():
        st_ref[...] = jnp.zeros_like(st_ref)

    ts = q_ref.shape[0]
    n_chunks = ts // CHUNK
    lf = lf_ref[...]
    hi = lf.astype(BF16)
    r1 = lf - hi.astype(F32)
    mid = r1.astype(BF16)
    lo = (r1 - mid.astype(F32)).astype(BF16)
    by_chunk = lambda x: jnp.concatenate([x[c * CHUNK:(c + 1) * CHUNK] for c in range(n_chunks)], axis=1)
    b_all = _dot(tri_ref[...], jnp.concatenate([by_chunk(hi), by_chunk(mid), by_chunk(lo)], axis=0))

    row = lax.broadcasted_iota(jnp.int32, (CHUNK, HEAD_DIM), 0)
    t_idx = lax.broadcasted_iota(jnp.int32, (CHUNK, CHUNK), 0)
    s_idx = lax.broadcasted_iota(jnp.int32, (CHUNK, CHUNK), 1)
    levels = []
    half = CHUNK // 2
    while half >= SUB:
        size = 2 * half
        levels.append((half, row % size >= half,
                       (t_idx // size == s_idx // size) & (t_idx % size >= half) & (s_idx % size < half)))
        half //= 2
    causal_bias = [jnp.where(row % SUB >= j, 0.0, NEG_BIG) for j in range(SUB)]
    place = [s_idx == (t_idx // SUB) * SUB + j for j in range(SUB)]
    for c in range(n_chunks):
        r0 = c * CHUNK
        b = b_all[:, c * HEAD_DIM:(c + 1) * HEAD_DIM]
        q = q_ref[r0:r0 + CHUNK, :].astype(F32)
        k = k_ref[r0:r0 + CHUNK, :].astype(F32)
        v = v_ref[r0:r0 + CHUNK, :]
        b_end = b[CHUNK - 1:CHUNK, :]
        st = st_ref[...]
        inter = _dot_nt((q * jnp.exp2(b)).astype(BF16), st.astype(BF16))

        scores = jnp.zeros((CHUNK, CHUNK), F32)
        for half, past_mid, pair in levels:
            size = 2 * half
            b_mid = jnp.concatenate(
                [jnp.broadcast_to(b[m + half - 1:m + half, :], (size, HEAD_DIM)) for m in range(0, CHUNK, size)], axis=0)
            x = (jnp.where(past_mid, q, k) * jnp.exp2(-jnp.abs(b - b_mid))).astype(BF16)
            scores = jnp.where(pair, _dot_nt(x, x), scores)

        blocks = lambda x: x.reshape(CHUNK // SUB, SUB, HEAD_DIM)
        for j in range(SUB):
            pick = lambda x: jnp.broadcast_to(blocks(x)[:, j:j + 1, :], blocks(x).shape).reshape(CHUNK, HEAD_DIM)
            dec = jnp.exp2(b - pick(b) + causal_bias[j])
            col = jnp.sum(q * pick(k) * dec, axis=-1, keepdims=True)
            scores = jnp.where(place[j], col, scores)

        o = inter + _dot(scores.astype(BF16), v)

        k_end = (k * jnp.exp2(b_end - b)).astype(BF16)
        st_ref[...] = st * jnp.exp2(b_end) + _dot_tn(v, k_end)

        o_ref[r0:r0 + CHUNK, :] = (_rms(o, nw_ref[...]) * gate_ref[r0:r0 + CHUNK, :].astype(F32)).astype(BF16)


def _hgrn(q, lf, k, v, gates, gate_col0, norm_w, *, batch, seq, ts=512):
    n = q.shape[0]
    nt = seq // ts
    blk = lambda col0: pl.BlockSpec((ts, HEAD_DIM), lambda b, h, t: (b * nt + t, h + col0))
    tri = (jnp.arange(CHUNK)[:, None] >= jnp.arange(CHUNK)[None, :]).astype(BF16)
    tri3 = jnp.concatenate([tri, tri, tri], axis=1)
    return pl.pallas_call(
        _hgrn_body, out_shape=jax.ShapeDtypeStruct((n, HG_HEADS * HEAD_DIM), BF16),
        grid=(batch, HG_HEADS, nt),
        in_specs=[blk(0), blk(0), blk(0), blk(0), blk(gate_col0), _resident((1, HEAD_DIM)),
                  _resident((CHUNK, 3 * CHUNK))],
        out_specs=blk(0),
        scratch_shapes=[pltpu.VMEM((HEAD_DIM, HEAD_DIM), F32)],
        compiler_params=_params("arbitrary", "arbitrary", "arbitrary"), name="hgrn2",
    )(q, lf, k, v, gates, norm_w.reshape(1, HEAD_DIM), tri3)


def _split3_bf16(x):
    hi = x.astype(BF16)
    r1 = x - hi.astype(F32)
    mid = r1.astype(BF16)
    lo = (r1 - mid.astype(F32)).astype(BF16)
    return hi, mid, lo


def _unit_lower_inverse_pairs(a_pairs):
    shape = a_pairs[0].shape
    row = lax.broadcasted_iota(jnp.int32, shape, 0)
    lane = lax.broadcasted_iota(jnp.int32, shape, 1)
    first = lane < CHUNK
    eye = jnp.where(row == lane % CHUNK, 1.0, 0.0)

    def level(power, partial):
        top = jnp.concatenate([jnp.where(first, power, 0.0), jnp.where(first, partial, 0.0)], axis=1)
        bottom = jnp.concatenate([jnp.where(first, 0.0, power), jnp.where(first, 0.0, partial)], axis=1)
        res = _dot(power.astype(BF16), jnp.concatenate([top, bottom], axis=0).astype(BF16))
        return res[:, :2 * CHUNK], partial + res[:, 2 * CHUNK:]

    state = [level(-a, eye) for a in a_pairs]
    span = 2
    while span < CHUNK:
        state = [level(power, partial) for power, partial in state]
        span *= 2
    return [partial for _, partial in state]


def _gdn_body(q_ref, k_ref, v_ref, z_ref, gb_ref, nw_ref, triu_ref, tril_ref, o_ref, s_ref):
    @pl.when(pl.program_id(2) == 0)
    def _():
        s_ref[...] = jnp.zeros_like(s_ref)

    ts = q_ref.shape[0]
    n_chunks = ts // CHUNK
    heads = q_ref.shape[1] // HEAD_DIM
    rep = s_ref.shape[0] // heads
    dup = 2 * CHUNK
    g_stack = jnp.concatenate([gb_ref[SUBLANES * h:SUBLANES * (h + 1), c * CHUNK:(c + 1) * CHUNK]
                               for h in range(heads) for c in range(n_chunks)], axis=0)
    g_cols = jnp.concatenate([g_stack, jnp.zeros_like(g_stack)], axis=1).T[:CHUNK]
    gam_rows = _dot(jnp.concatenate(_split3_bf16(g_stack), axis=1), triu_ref[...])
    gam_cols = _dot(tril_ref[...], jnp.concatenate(_split3_bf16(g_cols), axis=0))
    t_dup = lax.broadcasted_iota(jnp.int32, (CHUNK, dup), 0)
    s_dup_raw = lax.broadcasted_iota(jnp.int32, (CHUNK, dup), 1)
    s_dup = s_dup_raw % CHUNK
    s_idx = lax.broadcasted_iota(jnp.int32, (CHUNK, CHUNK), 0)
    t_idx = lax.broadcasted_iota(jnp.int32, (CHUNK, CHUNK), 1)

    pairs = [(h, c) for h in range(heads) for c in range(n_chunks)]
    first = s_dup_raw < CHUNK
    a_pairs, pre = [], {}
    for h, c in pairs:
        r0 = c * CHUNK
        q = q_ref[r0:r0 + CHUNK, h * HEAD_DIM:(h + 1) * HEAD_DIM]
        k = k_ref[r0:r0 + CHUNK, h * HEAD_DIM:(h + 1) * HEAD_DIM]
        qf = q.astype(F32)
        kf = k.astype(F32)
        kkq = _dot_nt(k, jnp.concatenate([k, k, q, q], axis=0))
        kk_dup = kkq[:, :dup]
        kq = kkq[:, dup:dup + CHUNK]
        col0 = SUBLANES * (h * n_chunks + c)
        gam_c = [gam_cols[:, col0 + i:col0 + i + 1] for i in range(rep)]
        beta = [g_cols[:, col0 + rep + i:col0 + rep + i + 1] for i in range(rep)]
        gam_r_pair = jnp.where(first[0:1], gam_rows[col0:col0 + 1, :], gam_rows[col0 + 1:col0 + 2, :])
        lmat_pair = jnp.exp(jnp.where(t_dup >= s_dup, jnp.where(first, gam_c[0], gam_c[1]) - gam_r_pair, NEG_BIG))
        a_pairs.append(jnp.where(t_dup > s_dup, jnp.where(first, beta[0], beta[1]) * kk_dup * lmat_pair, 0.0))
        for i in range(rep):
            vcol = (h * rep + i) * HEAD_DIM
            gam_r = gam_rows[col0 + i:col0 + i + 1, :CHUNK]
            g_end = gam_c[i][CHUNK - 1:CHUNK, :]
            lmat_t = jnp.exp(jnp.where(t_idx >= s_idx, gam_r - gam_c[i], NEG_BIG))
            e_gam = jnp.exp(gam_c[i])
            v = v_ref[r0:r0 + CHUNK, vcol:vcol + HEAD_DIM].astype(F32)
            rhs = jnp.concatenate([(v * beta[i]).astype(BF16), (kf * (beta[i] * e_gam)).astype(BF16)], axis=1)
            k_end = kf * jnp.exp(g_end - gam_c[i])
            x_t = jnp.concatenate([k_end, kq * lmat_t], axis=1)
            pre[(h, c, i)] = (rhs, x_t, qf * e_gam, jnp.exp(g_end))

    inverses = _unit_lower_inverse_pairs(a_pairs)

    chain = {}
    for p, (h, c) in enumerate(pairs):
        inv_pair = inverses[p].astype(BF16)
        for i in range(rep):
            rhs, x_t, qg, decay = pre[(h, c, i)]
            pad = jnp.zeros_like(rhs)
            uw = _dot(inv_pair, jnp.concatenate([rhs, pad] if i == 0 else [pad, rhs], axis=0)).astype(BF16)
            r = _dot(x_t.T.astype(BF16), uw)
            ktu = r[:HEAD_DIM, :HEAD_DIM]
            ktw = r[:HEAD_DIM, HEAD_DIM:]
            qku = r[HEAD_DIM:, :HEAD_DIM]
            qkw = r[HEAD_DIM:, HEAD_DIM:]
            lhs = jnp.concatenate([ktw, qg - qkw], axis=0).astype(BF16)
            chain[(h, c, i)] = (lhs, ktu, qku, decay)

    zero_state = jnp.zeros((HEAD_DIM, HEAD_DIM), BF16)
    states = [s_ref[j] for j in range(heads * rep)]
    for c in range(n_chunks):
        r0 = c * CHUNK
        for h in range(heads):
            lhs = jnp.concatenate([chain[(h, c, i)][0] for i in range(rep)], axis=1)
            diag = jnp.concatenate(
                [jnp.concatenate([states[h * rep + i].astype(BF16) if j == i else zero_state for j in range(rep)],
                                 axis=1) for i in range(rep)], axis=0)
            m_all = _dot(lhs, diag)
            for i in range(rep):
                _, ktu, qku, decay = chain[(h, c, i)]
                vcol = (h * rep + i) * HEAD_DIM
                m = m_all[:, i * HEAD_DIM:(i + 1) * HEAD_DIM]
                states[h * rep + i] = states[h * rep + i] * decay + ktu - m[:HEAD_DIM]
                o = m[HEAD_DIM:] + qku
                z = z_ref[r0:r0 + CHUNK, vcol:vcol + HEAD_DIM].astype(F32)
                o_ref[r0:r0 + CHUNK, vcol:vcol + HEAD_DIM] = (_rms(o, nw_ref[...]) * z).astype(BF16)
    for j in range(heads * rep):
        s_ref[j] = states[j]


def _gdn(qk, v, gates, gab_t, norm_w, *, batch, seq, ts=128, heads=8):
    n = qk.shape[0]
    nt = seq // ts
    groups = GDN_QK_HEADS // heads
    rep = GDN_V_HEADS // GDN_QK_HEADS
    blk = lambda width, col0: pl.BlockSpec((ts, width * HEAD_DIM), lambda b, g, t: (b * nt + t, g + col0))
    r = jnp.arange(CHUNK)[:, None]
    c = jnp.arange(CHUNK)[None, :]
    upper = (r <= c).astype(BF16)
    upper_dup = jnp.concatenate([upper, upper], axis=1)
    triu3 = jnp.concatenate([upper_dup] * 3, axis=0)
    tril3 = jnp.concatenate([(r >= c).astype(BF16)] * 3, axis=1)
    return pl.pallas_call(
        _gdn_body, out_shape=jax.ShapeDtypeStruct((n, GDN_V_HEADS * HEAD_DIM), BF16),
        grid=(batch, groups, nt),
        in_specs=[blk(heads, 0), blk(heads, groups), blk(heads * rep, 0), blk(heads * rep, 0),
                  pl.BlockSpec((SUBLANES * heads, ts), lambda b, g, t: (g, b * nt + t)),
                  _resident((1, HEAD_DIM)), _resident(triu3.shape), _resident(tril3.shape)],
        out_specs=blk(heads * rep, 0),
        scratch_shapes=[pltpu.VMEM((heads * rep, HEAD_DIM, HEAD_DIM), F32)],
        compiler_params=_params("arbitrary", "arbitrary", "arbitrary"), name="gated_delta",
    )(qk, qk, v, gates, gab_t, norm_w.reshape(1, HEAD_DIM), triu3, tril3)


def _merge_body(oh_ref, og_ref, gate_ref, h_ref, wbh_ref, wbg_ref, wo_ref, o_ref):
    d = h_ref.shape[1]
    yh = _dot(oh_ref[...], wbh_ref[...])
    yg = _dot(og_ref[...], wbg_ref[...])
    y = gate_ref[:, :d].astype(F32) * yh + gate_ref[:, d:].astype(F32) * yg
    o_ref[...] = h_ref[...] + _dot(y.astype(BF16), wo_ref[...])


def _merge(o_h, o_g, merge_gates, h, wbh, wbg, wo, *, tm=512):
    n, d = h.shape
    row = lambda width: pl.BlockSpec((tm, width), lambda i: (i, 0))
    return pl.pallas_call(
        _merge_body, out_shape=jax.ShapeDtypeStruct((n, d), F32), grid=(n // tm,),
        in_specs=[row(o_h.shape[1]), row(o_g.shape[1]), row(2 * d), row(d),
                  _resident(wbh.shape), _resident(wbg.shape), _resident(wo.shape)],
        out_specs=row(d),
        compiler_params=_params("arbitrary"), name="merge",
    )(o_h, o_g, merge_gates, h, wbh.astype(BF16), wbg.astype(BF16), wo.astype(BF16))


def _mixer(u, h, w_in, lb_logits, hgrn_norm, conv_w, a_log, dt_bias, gdn_norm, wbh, wbg, wo, *, layer, batch, seq):
    n, d = h.shape
    hk = HG_HEADS * HEAD_DIM
    gk = GDN_QK_HEADS * HEAD_DIM
    gv = GDN_V_HEADS * HEAD_DIM
    sizes = (hk, hk, hk, hk, gk, gk, gv, GDN_V_HEADS, GDN_V_HEADS, gv, d, d)
    offs = [0]
    for s in sizes:
        offs.append(offs[-1] + s)
    col = lambda a, b: w_in[:, offs[a]:offs[b]]
    conv_pad = jnp.zeros((SUBLANES - CONV_K, conv_w.shape[1]), F32)
    cw = jnp.concatenate([conv_w, conv_pad], axis=0)

    rep = GDN_V_HEADS // GDN_QK_HEADS
    pad = SUBLANES - 2 * rep
    regroup = lambda a, b, fill: jnp.concatenate(
        [a.reshape(-1, GDN_QK_HEADS, rep), b.reshape(-1, GDN_QK_HEADS, rep),
         jnp.full((a.shape[0], GDN_QK_HEADS, pad), fill, a.dtype)], axis=2).reshape(a.shape[0], -1)
    w_ga = w_in[:, offs[7]:offs[8]]
    w_gb = w_in[:, offs[8]:offs[9]]
    w_gab = regroup(w_ga, w_gb, 0.0)
    w_gab = jnp.pad(w_gab, ((0, 0), (0, LANES - w_gab.shape[1])))
    zeros = jnp.zeros((1, GDN_V_HEADS), F32)
    aux = jnp.concatenate([regroup(a_log.reshape(1, -1).astype(F32), zeros, 0.0),
                           regroup(dt_bias.reshape(1, -1).astype(F32), zeros, 0.0)], axis=0)
    aux = jnp.pad(aux, ((0, 0), (0, LANES - aux.shape[1])))

    w_all = jnp.concatenate([col(0, 1), col(1, 2), col(2, 3), col(9, 10), col(3, 4), col(10, 12), col(4, 6),
                             col(6, 7), w_gab], axis=1).astype(BF16)
    widths = (hk, hk, hk, gv + hk, 2 * d, 2 * gk, gv)
    q_h, lf, k_h, v_h, act_gates, merge_gates, qk_g, v_g, gab_t = _proj(
        u, w_all, lb_logits.astype(F32), cw, aux, widths, layer=layer, seq=seq)

    o_h = _hgrn(q_h, lf, k_h, v_h, act_gates, gv // HEAD_DIM, hgrn_norm, batch=batch, seq=seq)
    o_g = _gdn(qk_g, v_g, act_gates, gab_t, gdn_norm, batch=batch, seq=seq)
    return _merge(o_h, o_g, merge_gates, h, wbh, wbg, wo)


def kernel(x, ffn1_norm, ffn1_w_in, ffn1_w_out, mix_norm, w_in, hgrn_lb_logits, hgrn_out_norm, gdn_conv_w,
           gdn_a_log, gdn_dt_bias, gdn_out_norm, w_branch_hgrn, w_branch_gdn, w_out, ffn2_norm, ffn2_w_in,
           ffn2_w_out, final_norm):
    batch, seq, d = x.shape
    depth = ffn1_norm.shape[0]
    h = x.reshape(batch * seq, d)
    for l in range(depth):
        h, u = _ffn(h, ffn1_norm[l], ffn1_w_in[l], ffn1_w_out[l], mix_norm[l], final=False)
        h = _mixer(u, h, w_in[l], hgrn_lb_logits, hgrn_out_norm[l], gdn_conv_w[l], gdn_a_log[l],
                   gdn_dt_bias[l], gdn_out_norm[l], w_branch_hgrn[l], w_branch_gdn[l], w_out[l],
                   layer=l, batch=batch, seq=seq)
        last = l == depth - 1
        nxt = final_norm if last else ffn1_norm[l + 1]
        out = _ffn(h, ffn2_norm[l], ffn2_w_in[l], ffn2_w_out[l], nxt, final=last)
        h = out if last else out[0]
    return h.reshape(batch, seq, d)
```

```python
import functools

import jax
import jax.numpy as jnp
from jax import lax
from jax.experimental import pallas as pl
from jax.experimental.pallas import tpu as pltpu

F32 = jnp.float32
BF16 = jnp.bfloat16

EPS = 1e-6
CHUNK = 64
SUB = 8
HEAD_DIM = 128
HG_HEADS = 8
GDN_QK_HEADS = 8
GDN_V_HEADS = 16
CONV_K = 4
LANES = 128
SUBLANES = 8
NEG_BIG = -1e30
LOG2E = 1.4426950408889634
VMEM_LIMIT = 56 * 1024 * 1024


def _dot(a, b):
    return jnp.dot(a, b, preferred_element_type=F32)


def _dot_nt(a, b):
    return lax.dot_general(a, b, (((1,), (1,)), ((), ())), preferred_element_type=F32)


def _dot_tn(a, b):
    return lax.dot_general(a, b, (((0,), (0,)), ((), ())), preferred_element_type=F32)


def _sigmoid(x):
    return 1.0 / (1.0 + jnp.exp2(x * -LOG2E))


def _rms(x, w):
    ms = jnp.mean(x * x, axis=-1, keepdims=True)
    return x * lax.rsqrt(ms + EPS) * w


def _params(*sem):
    return pltpu.CompilerParams(dimension_semantics=sem, vmem_limit_bytes=VMEM_LIMIT)


def _resident(shape):
    return pl.BlockSpec(shape, lambda *_: (0,) * len(shape), pipeline_mode=pl.Buffered(1))


def _ffn_body(h_ref, nw_ref, wa_ref, wb_ref, wo_ref, onw_ref, *out_refs, ff_tile, final):
    h = h_ref[...]
    xn = _rms(h, nw_ref[...]).astype(BF16)
    d_ff = wa_ref.shape[1]
    acc = jnp.zeros(h.shape, F32)
    for j in range(0, d_ff, ff_tile):
        w = min(ff_tile, d_ff - j)
        a = _dot(xn, wa_ref[:, j:j + w])
        b = _dot(xn, wb_ref[:, j:j + w])
        g = (a * _sigmoid(a) * b).astype(BF16)
        acc = acc + _dot(g, wo_ref[j:j + w, :])
    hn = h + 0.5 * acc
    if final:
        out_refs[0][...] = _rms(hn, onw_ref[...])
    else:
        out_refs[0][...] = hn
        out_refs[1][...] = _rms(hn, onw_ref[...]).astype(BF16)


def _ffn(h, norm_w, w_in, w_out, next_norm_w, *, final, tm=512, ff_tile=512):
    n, d = h.shape
    d_ff = w_out.shape[0]
    wa = w_in[:, :d_ff].astype(BF16)
    wb = w_in[:, d_ff:].astype(BF16)
    wo = w_out.astype(BF16)
    row = pl.BlockSpec((tm, d), lambda i: (i, 0))
    if final:
        out_shape = jax.ShapeDtypeStruct((n, d), F32)
        out_specs = row
    else:
        out_shape = (jax.ShapeDtypeStruct((n, d), F32), jax.ShapeDtypeStruct((n, d), BF16))
        out_specs = (row, row)
    return pl.pallas_call(
        functools.partial(_ffn_body, ff_tile=ff_tile, final=final),
        out_shape=out_shape,
        grid=(n // tm,),
        in_specs=[row, _resident((1, d)), _resident((d, d_ff)), _resident((d, d_ff)),
                  _resident((d_ff, d)), _resident((1, d))],
        out_specs=out_specs,
        compiler_params=_params("arbitrary"),
        name="ffn_final" if final else "ffn",
    )(h, norm_w.reshape(1, d), wa, wb, wo, next_norm_w.reshape(1, d))


def _proj_body(u_ref, w_ref, logit_ref, cw_ref, gaux_ref,
               qh_ref, lf_ref, kh_ref, vh_ref, act_ref, mg_ref, qk_ref, vg_ref, gab_ref, ext_ref,
               *, layer, tiles_per_seq, tn, widths):
    w_hq, w_hf, w_hi, w_act, w_mg, w_qk, w_gv = widths
    u = u_ref[...]
    tm = u.shape[0]
    proj = lambda c0, width=tn: _dot(u, w_ref[:, c0:c0 + width])
    bases = [0]
    for w in widths:
        bases.append(bases[-1] + w)

    def hq_tile(o):
        p = proj(bases[0] + o)
        qh_ref[:, o:o + tn] = (p * _sigmoid(p) * HEAD_DIM ** -0.5).astype(BF16)

    lg = logit_ref[...]
    e = jnp.exp(lg - jnp.max(lg, axis=0, keepdims=True))
    lb_all = jnp.sum(e[0:layer + 1, :], axis=0, keepdims=True) / jnp.sum(e, axis=0, keepdims=True)

    def hf_tile(o):
        s = _sigmoid(proj(bases[1] + o))
        lb = lb_all[:, o:o + tn]
        lf_ref[:, o:o + tn] = jnp.log2(lb + (1.0 - lb) * s)
        kh_ref[:, o:o + tn] = ((1.0 - lb) * (1.0 - s)).astype(BF16)

    def hi_tile(o):
        vh_ref[:, o:o + tn] = proj(bases[2] + o).astype(BF16)

    def act_tile(o):
        p = proj(bases[3] + o)
        act_ref[:, o:o + tn] = (p * _sigmoid(p)).astype(BF16)

    def mg_tile(o):
        mg_ref[:, o:o + tn] = _sigmoid(proj(bases[4] + o)).astype(BF16)

    @pl.when(pl.program_id(0) % tiles_per_seq == 0)
    def _():
        ext_ref[0:SUBLANES, :] = jnp.zeros((SUBLANES, ext_ref.shape[1]), F32)

    def conv_tile(o):
        p = proj(bases[5] + o)
        ext_ref[SUBLANES:, o:o + tn] = p
        acc = p * cw_ref[CONV_K - 1:CONV_K, o:o + tn]
        for d in range(1, CONV_K):
            acc = acc + ext_ref[SUBLANES - d:SUBLANES - d + tm, o:o + tn] * cw_ref[CONV_K - 1 - d:CONV_K - d, o:o + tn]
        y = acc * _sigmoid(acc)
        if o < w_qk:
            scale = HEAD_DIM ** -0.5 if o < w_qk // 2 else 1.0
            for s in range(0, tn, HEAD_DIM):
                ys = y[:, s:s + HEAD_DIM]
                inv = lax.rsqrt(jnp.sum(ys * ys, axis=-1, keepdims=True) + EPS) * scale
                qk_ref[:, o + s:o + s + HEAD_DIM] = (ys * inv).astype(BF16)
        else:
            vg_ref[:, o - w_qk:o - w_qk + tn] = y.astype(BF16)

    light = [(f, o) for f, w in ((hq_tile, w_hq), (hf_tile, w_hf), (hi_tile, w_hi), (act_tile, w_act), (mg_tile, w_mg))
             for o in range(0, w, tn)]
    heavy = [(conv_tile, o) for o in range(0, w_qk + w_gv, tn)]
    per_heavy = -(-len(light) // len(heavy))
    while light or heavy:
        for f, o in light[:per_heavy]:
            f(o)
        light = light[per_heavy:]
        if heavy:
            f, o = heavy.pop(0)
            f(o)
    ext_ref[0:SUBLANES, :] = ext_ref[tm:tm + SUBLANES, :]

    p = proj(bases[7], LANES)
    x = p + gaux_ref[1:2, :]
    softplus = jnp.maximum(x, 0.0) + jnp.log1p(jnp.exp(-jnp.abs(x)))
    g = -jnp.exp(gaux_ref[0:1, :]) * softplus
    lane = lax.broadcasted_iota(jnp.int32, p.shape, 1)
    gab_ref[...] = jnp.where(lane % SUBLANES < 2, g, _sigmoid(p)).T


def _proj(u, w_all, lb_logits, cw, gaux, widths, *, layer, seq, tm=256, tn=256):
    n, d = u.shape
    w_hq, w_hf, w_hi, w_act, w_mg, w_qk, w_gv = widths
    row = lambda width: pl.BlockSpec((tm, width), lambda i: (i, 0))
    out_widths = (w_hq, w_hf, w_hf, w_hi, w_act, w_mg, w_qk, w_gv)
    out_dtypes = (BF16, F32, BF16, BF16, BF16, BF16, BF16, BF16)
    out_shape = tuple(jax.ShapeDtypeStruct((n, w), dt) for w, dt in zip(out_widths, out_dtypes))
    out_shape += (jax.ShapeDtypeStruct((LANES, n), F32),)
    out_specs = tuple(row(w) for w in out_widths) + (pl.BlockSpec((LANES, tm), lambda i: (0, i)),)
    return pl.pallas_call(
        functools.partial(_proj_body, layer=layer, tiles_per_seq=seq // tm, tn=tn, widths=widths),
        out_shape=out_shape, grid=(n // tm,),
        in_specs=[row(d), _resident(w_all.shape), _resident(lb_logits.shape), _resident(cw.shape),
                  _resident(gaux.shape)],
        out_specs=out_specs,
        scratch_shapes=[pltpu.VMEM((tm + SUBLANES, w_qk + w_gv), F32)],
        compiler_params=_params("arbitrary"), name="proj",
    )(u, w_all, lb_logits, cw, gaux)


def _hgrn_body(q_ref, lf_ref, k_ref, v_ref, gate_ref, nw_ref, tri_ref, o_ref, st_ref):
    @pl.when(pl.program_id(2) == 0)
    def _():
        st_ref[...] = jnp.zeros_like(st_ref)

    ts = q_ref.shape[0]
    n_chunks = ts // CHUNK
    lf = lf_ref[...]
    hi = lf.astype(BF16)
    r1 = lf - hi.astype(F32)
    mid = r1.astype(BF16)
    lo = (r1 - mid.astype(F32)).astype(BF16)
    by_chunk = lambda x: jnp.concatenate([x[c * CHUNK:(c + 1) * CHUNK] for c in range(n_chunks)], axis=1)
    b_all = _dot(tri_ref[...], jnp.concatenate([by_chunk(hi), by_chunk(mid), by_chunk(lo)], axis=0))

    row = lax.broadcasted_iota(jnp.int32, (CHUNK, HEAD_DIM), 0)
    t_idx = lax.broadcasted_iota(jnp.int32, (CHUNK, CHUNK), 0)
    s_idx = lax.broadcasted_iota(jnp.int32, (CHUNK, CHUNK), 1)
    levels = []
    half = CHUNK // 2
    while half >= SUB:
        size = 2 * half
        levels.append((half, row % size >= half,
                       (t_idx // size == s_idx // size) & (t_idx % size >= half) & (s_idx % size < half)))
        half //= 2
    causal_bias = [jnp.where(row % SUB >= j, 0.0, NEG_BIG) for j in range(SUB)]
    place = [s_idx == (t_idx // SUB) * SUB + j for j in range(SUB)]
    for c in range(n_chunks):
        r0 = c * CHUNK
        b = b_all[:, c * HEAD_DIM:(c + 1) * HEAD_DIM]
        q = q_ref[r0:r0 + CHUNK, :].astype(F32)
        k = k_ref[r0:r0 + CHUNK, :].astype(F32)
        v = v_ref[r0:r0 + CHUNK, :]
        b_end = b[CHUNK - 1:CHUNK, :]
        st = st_ref[...]
        inter = _dot_nt((q * jnp.exp2(b)).astype(BF16), st.astype(BF16))

        scores = jnp.zeros((CHUNK, CHUNK), F32)
        for half, past_mid, pair in levels:
            size = 2 * half
            b_mid = jnp.concatenate(
                [jnp.broadcast_to(b[m + half - 1:m + half, :], (size, HEAD_DIM)) for m in range(0, CHUNK, size)], axis=0)
            x = (jnp.where(past_mid, q, k) * jnp.exp2(-jnp.abs(b - b_mid))).astype(BF16)
            scores = jnp.where(pair, _dot_nt(x, x), scores)

        blocks = lambda x: x.reshape(CHUNK // SUB, SUB, HEAD_DIM)
        for j in range(SUB):
            pick = lambda x: jnp.broadcast_to(blocks(x)[:, j:j + 1, :], blocks(x).shape).reshape(CHUNK, HEAD_DIM)
            dec = jnp.exp2(b - pick(b) + causal_bias[j])
            col = jnp.sum(q * pick(k) * dec, axis=-1, keepdims=True)
            scores = jnp.where(place[j], col, scores)

        o = inter + _dot(scores.astype(BF16), v)

        k_end = (k * jnp.exp2(b_end - b)).astype(BF16)
        st_ref[...] = st * jnp.exp2(b_end) + _dot_tn(v, k_end)

        o_ref[r0:r0 + CHUNK, :] = (_rms(o, nw_ref[...]) * gate_ref[r0:r0 + CHUNK, :].astype(F32)).astype(BF16)


def _hgrn(q, lf, k, v, gates, gate_col0, norm_w, *, batch, seq, ts=512):
    n = q.shape[0]
    nt = seq // ts
    blk = lambda col0: pl.BlockSpec((ts, HEAD_DIM), lambda b, h, t: (b * nt + t, h + col0))
    tri = (jnp.arange(CHUNK)[:, None] >= jnp.arange(CHUNK)[None, :]).astype(BF16)
    tri3 = jnp.concatenate([tri, tri, tri], axis=1)
    return pl.pallas_call(
        _hgrn_body, out_shape=jax.ShapeDtypeStruct((n, HG_HEADS * HEAD_DIM), BF16),
        grid=(batch, HG_HEADS, nt),
        in_specs=[blk(0), blk(0), blk(0), blk(0), blk(gate_col0), _resident((1, HEAD_DIM)),
                  _resident((CHUNK, 3 * CHUNK))],
        out_specs=blk(0),
        scratch_shapes=[pltpu.VMEM((HEAD_DIM, HEAD_DIM), F32)],
        compiler_params=_params("arbitrary", "arbitrary", "arbitrary"), name="hgrn2",
    )(q, lf, k, v, gates, norm_w.reshape(1, HEAD_DIM), tri3)


def _split3_bf16(x):
    hi = x.astype(BF16)
    r1 = x - hi.astype(F32)
    mid = r1.astype(BF16)
    lo = (r1 - mid.astype(F32)).astype(BF16)
    return hi, mid, lo


def _unit_lower_inverse_pairs(a_pairs):
    shape = a_pairs[0].shape
    row = lax.broadcasted_iota(jnp.int32, shape, 0)
    lane = lax.broadcasted_iota(jnp.int32, shape, 1)
    first = lane < CHUNK
    eye = jnp.where(row == lane % CHUNK, 1.0, 0.0)

    def level(power, partial):
        top = jnp.concatenate([jnp.where(first, power, 0.0), jnp.where(first, partial, 0.0)], axis=1)
        bottom = jnp.concatenate([jnp.where(first, 0.0, power), jnp.where(first, 0.0, partial)], axis=1)
        res = _dot(power.astype(BF16), jnp.concatenate([top, bottom], axis=0).astype(BF16))
        return res[:, :2 * CHUNK], partial + res[:, 2 * CHUNK:]

    state = [level(-a, eye) for a in a_pairs]
    span = 2
    while span < CHUNK:
        state = [level(power, partial) for power, partial in state]
        span *= 2
    return [partial for _, partial in state]


def _gdn_body(q_ref, k_ref, v_ref, z_ref, gb_ref, nw_ref, triu_ref, tril_ref, o_ref,
              s_ref, uw_ref, xt_ref, qg_ref, dec_ab_ref, lhs_ref, ktu_ref, qku_ref, dec_bc_ref):
    ts = q_ref.shape[0]
    n_chunks = ts // CHUNK
    heads = q_ref.shape[1] // HEAD_DIM
    rep = s_ref.shape[0] // heads
    dup = 2 * CHUNK
    pairs = [(h, c) for h in range(heads) for c in range(n_chunks)]
    prob = lambda h, c, i: (h * n_chunks + c) * rep + i

    @pl.when(pl.program_id(2) == 0)
    def _():
        for ref in (s_ref, uw_ref, xt_ref, qg_ref, dec_ab_ref, lhs_ref, ktu_ref, qku_ref, dec_bc_ref):
            ref[...] = jnp.zeros_like(ref)

    g_stack = jnp.concatenate([gb_ref[SUBLANES * h:SUBLANES * (h + 1), c * CHUNK:(c + 1) * CHUNK]
                               for h, c in pairs], axis=0)
    g_cols = jnp.concatenate([g_stack, jnp.zeros_like(g_stack)], axis=1).T[:CHUNK]
    gam_rows = _dot(jnp.concatenate(_split3_bf16(g_stack), axis=1), triu_ref[...])
    gam_cols = _dot(tril_ref[...], jnp.concatenate(_split3_bf16(g_cols), axis=0))
    t_dup = lax.broadcasted_iota(jnp.int32, (CHUNK, dup), 0)
    s_dup_raw = lax.broadcasted_iota(jnp.int32, (CHUNK, dup), 1)
    s_dup = s_dup_raw % CHUNK
    s_idx = lax.broadcasted_iota(jnp.int32, (CHUNK, CHUNK), 0)
    t_idx = lax.broadcasted_iota(jnp.int32, (CHUNK, CHUNK), 1)
    first = s_dup_raw < CHUNK
    a_pairs, rhs_all, xt_all, qg_all, dec_all = [], {}, {}, {}, {}
    for h, c in pairs:
        r0 = c * CHUNK
        q = q_ref[r0:r0 + CHUNK, h * HEAD_DIM:(h + 1) * HEAD_DIM]
        k = k_ref[r0:r0 + CHUNK, h * HEAD_DIM:(h + 1) * HEAD_DIM]
        qf = q.astype(F32)
        kf = k.astype(F32)
        kkq = _dot_nt(k, jnp.concatenate([k, k, q, q], axis=0))
        kk_dup = kkq[:, :dup]
        kq = kkq[:, dup:dup + CHUNK]
        col0 = SUBLANES * (h * n_chunks + c)
        gam_c = [gam_cols[:, col0 + i:col0 + i + 1] for i in range(rep)]
        beta = [g_cols[:, col0 + rep + i:col0 + rep + i + 1] for i in range(rep)]
        gam_r_pair = jnp.where(first[0:1], gam_rows[col0:col0 + 1, :], gam_rows[col0 + 1:col0 + 2, :])
        lmat_pair = jnp.exp(jnp.where(t_dup >= s_dup, jnp.where(first, gam_c[0], gam_c[1]) - gam_r_pair, NEG_BIG))
        a_pairs.append(jnp.where(t_dup > s_dup, jnp.where(first, beta[0], beta[1]) * kk_dup * lmat_pair, 0.0))
        for i in range(rep):
            p = prob(h, c, i)
            vcol = (h * rep + i) * HEAD_DIM
            gam_r = gam_rows[col0 + i:col0 + i + 1, :CHUNK]
            g_end = gam_c[i][CHUNK - 1:CHUNK, :]
            lmat_t = jnp.exp(jnp.where(t_idx >= s_idx, gam_r - gam_c[i], NEG_BIG))
            e_gam = jnp.exp(gam_c[i])
            v = v_ref[r0:r0 + CHUNK, vcol:vcol + HEAD_DIM].astype(F32)
            rhs_all[p] = jnp.concatenate([(v * beta[i]).astype(BF16), (kf * (beta[i] * e_gam)).astype(BF16)], axis=1)
            k_end = kf * jnp.exp(g_end - gam_c[i])
            xt_all[p] = jnp.concatenate([k_end, kq * lmat_t], axis=1).T.astype(BF16)
            qg_all[p] = qf * e_gam
            dec_all[p] = jnp.broadcast_to(jnp.exp(g_end), dec_ab_ref.shape[1:])

    zero_state = jnp.zeros((HEAD_DIM, HEAD_DIM), BF16)
    states = [s_ref[j] for j in range(heads * rep)]
    for c in range(n_chunks):
        r0 = c * CHUNK
        for h in range(heads):
            diag = jnp.concatenate(
                [jnp.concatenate([states[h * rep + i].astype(BF16) if j == i else zero_state for j in range(rep)],
                                 axis=1) for i in range(rep)], axis=0)
            m_all = _dot(lhs_ref[h * n_chunks + c], diag)
            for i in range(rep):
                p = prob(h, c, i)
                vcol = (h * rep + i) * HEAD_DIM
                m = m_all[:, i * HEAD_DIM:(i + 1) * HEAD_DIM]
                states[h * rep + i] = states[h * rep + i] * dec_bc_ref[p][0:1, 0:1] + ktu_ref[p] - m[:HEAD_DIM]
                o = m[HEAD_DIM:] + qku_ref[p]
                z = z_ref[r0:r0 + CHUNK, vcol:vcol + HEAD_DIM].astype(F32)
                o_ref[r0:r0 + CHUNK, vcol:vcol + HEAD_DIM] = (_rms(o, nw_ref[...]) * z).astype(BF16)
    for j in range(heads * rep):
        s_ref[j] = states[j]

    for h, c in pairs:
        lhs = []
        for i in range(rep):
            p = prob(h, c, i)
            r = _dot(xt_ref[p], uw_ref[p])
            ktu_ref[p] = r[:HEAD_DIM, :HEAD_DIM]
            qku_ref[p] = r[HEAD_DIM:, :HEAD_DIM]
            dec_bc_ref[p] = dec_ab_ref[p]
            lhs.append(jnp.concatenate([r[:HEAD_DIM, HEAD_DIM:], qg_ref[p] - r[HEAD_DIM:, HEAD_DIM:]], axis=0))
        lhs_ref[h * n_chunks + c] = jnp.concatenate(lhs, axis=1).astype(BF16)

    for p in range(len(rhs_all)):
        xt_ref[p] = xt_all[p]
        qg_ref[p] = qg_all[p]
        dec_ab_ref[p] = dec_all[p]
    inverses = _unit_lower_inverse_pairs(a_pairs)
    for n, (h, c) in enumerate(pairs):
        inv_pair = inverses[n].astype(BF16)
        for i in range(rep):
            p = prob(h, c, i)
            pad = jnp.zeros_like(rhs_all[p])
            stacked = jnp.concatenate([rhs_all[p], pad] if i == 0 else [pad, rhs_all[p]], axis=0)
            uw_ref[p] = _dot(inv_pair, stacked).astype(BF16)


def _gdn(qk, v, gates, gab_t, norm_w, *, batch, seq, ts=128, heads=8):
    n = qk.shape[0]
    nt = seq // ts
    groups = GDN_QK_HEADS // heads
    rep = GDN_V_HEADS // GDN_QK_HEADS
    n_chunks = ts // CHUNK
    n_pairs = heads * n_chunks
    n_probs = n_pairs * rep
    lag = 2
    fresh = lambda width, col0: pl.BlockSpec(
        (ts, width * HEAD_DIM), lambda b, g, t: (b * nt + jnp.minimum(t, nt - 1), g + col0))
    lagged = lambda width, col0: pl.BlockSpec(
        (ts, width * HEAD_DIM), lambda b, g, t: (b * nt + jnp.maximum(t - lag, 0), g + col0))
    r = jnp.arange(CHUNK)[:, None]
    c = jnp.arange(CHUNK)[None, :]
    upper = (r <= c).astype(BF16)
    upper_dup = jnp.concatenate([upper, upper], axis=1)
    triu3 = jnp.concatenate([upper_dup] * 3, axis=0)
    tril3 = jnp.concatenate([(r >= c).astype(BF16)] * 3, axis=1)
    return pl.pallas_call(
        _gdn_body, out_shape=jax.ShapeDtypeStruct((n, GDN_V_HEADS * HEAD_DIM), BF16),
        grid=(batch, groups, nt + lag),
        in_specs=[fresh(heads, 0), fresh(heads, groups), fresh(heads * rep, 0), lagged(heads * rep, 0),
                  pl.BlockSpec((SUBLANES * heads, ts), lambda b, g, t: (g, b * nt + jnp.minimum(t, nt - 1))),
                  _resident((1, HEAD_DIM)), _resident(triu3.shape), _resident(tril3.shape)],
        out_specs=lagged(heads * rep, 0),
        scratch_shapes=[pltpu.VMEM((heads * rep, HEAD_DIM, HEAD_DIM), F32),
                        pltpu.VMEM((n_probs, CHUNK, 2 * HEAD_DIM), BF16),
                        pltpu.VMEM((n_probs, HEAD_DIM + CHUNK, CHUNK), BF16),
                        pltpu.VMEM((n_probs, CHUNK, HEAD_DIM), F32),
                        pltpu.VMEM((n_probs, SUBLANES, LANES), F32),
                        pltpu.VMEM((n_pairs, HEAD_DIM + CHUNK, rep * HEAD_DIM), BF16),
                        pltpu.VMEM((n_probs, HEAD_DIM, HEAD_DIM), F32),
                        pltpu.VMEM((n_probs, CHUNK, HEAD_DIM), F32),
                        pltpu.VMEM((n_probs, SUBLANES, LANES), F32)],
        compiler_params=_params("arbitrary", "arbitrary", "arbitrary"), name="gated_delta",
    )(qk, qk, v, gates, gab_t, norm_w.reshape(1, HEAD_DIM), triu3, tril3)


def _merge_body(oh_ref, og_ref, gate_ref, h_ref, wbh_ref, wbg_ref, wo_ref, o_ref):
    d = h_ref.shape[1]
    yh = _dot(oh_ref[...], wbh_ref[...])
    yg = _dot(og_ref[...], wbg_ref[...])
    y = gate_ref[:, :d].astype(F32) * yh + gate_ref[:, d:].astype(F32) * yg
    o_ref[...] = h_ref[...] + _dot(y.astype(BF16), wo_ref[...])


def _merge(o_h, o_g, merge_gates, h, wbh, wbg, wo, *, tm=512):
    n, d = h.shape
    row = lambda width: pl.BlockSpec((tm, width), lambda i: (i, 0))
    return pl.pallas_call(
        _merge_body, out_shape=jax.ShapeDtypeStruct((n, d), F32), grid=(n // tm,),
        in_specs=[row(o_h.shape[1]), row(o_g.shape[1]), row(2 * d), row(d),
                  _resident(wbh.shape), _resident(wbg.shape), _resident(wo.shape)],
        out_specs=row(d),
        compiler_params=_params("arbitrary"), name="merge",
    )(o_h, o_g, merge_gates, h, wbh.astype(BF16), wbg.astype(BF16), wo.astype(BF16))


def _mixer(u, h, w_in, lb_logits, hgrn_norm, conv_w, a_log, dt_bias, gdn_norm, wbh, wbg, wo, *, layer, batch, seq):
    n, d = h.shape
    hk = HG_HEADS * HEAD_DIM
    gk = GDN_QK_HEADS * HEAD_DIM
    gv = GDN_V_HEADS * HEAD_DIM
    sizes = (hk, hk, hk, hk, gk, gk, gv, GDN_V_HEADS, GDN_V_HEADS, gv, d, d)
    offs = [0]
    for s in sizes:
        offs.append(offs[-1] + s)
    col = lambda a, b: w_in[:, offs[a]:offs[b]]
    conv_pad = jnp.zeros((SUBLANES - CONV_K, conv_w.shape[1]), F32)
    cw = jnp.concatenate([conv_w, conv_pad], axis=0)

    rep = GDN_V_HEADS // GDN_QK_HEADS
    pad = SUBLANES - 2 * rep
    regroup = lambda a, b, fill: jnp.concatenate(
        [a.reshape(-1, GDN_QK_HEADS, rep), b.reshape(-1, GDN_QK_HEADS, rep),
         jnp.full((a.shape[0], GDN_QK_HEADS, pad), fill, a.dtype)], axis=2).reshape(a.shape[0], -1)
    w_ga = w_in[:, offs[7]:offs[8]]
    w_gb = w_in[:, offs[8]:offs[9]]
    w_gab = regroup(w_ga, w_gb, 0.0)
    w_gab = jnp.pad(w_gab, ((0, 0), (0, LANES - w_gab.shape[1])))
    zeros = jnp.zeros((1, GDN_V_HEADS), F32)
    aux = jnp.concatenate([regroup(a_log.reshape(1, -1).astype(F32), zeros, 0.0),
                           regroup(dt_bias.reshape(1, -1).astype(F32), zeros, 0.0)], axis=0)
    aux = jnp.pad(aux, ((0, 0), (0, LANES - aux.shape[1])))

    w_all = jnp.concatenate([col(0, 1), col(1, 2), col(2, 3), col(9, 10), col(3, 4), col(10, 12), col(4, 6),
                             col(6, 7), w_gab], axis=1).astype(BF16)
    widths = (hk, hk, hk, gv + hk, 2 * d, 2 * gk, gv)
    q_h, lf, k_h, v_h, act_gates, merge_gates, qk_g, v_g, gab_t = _proj(
        u, w_all, lb_logits.astype(F32), cw, aux, widths, layer=layer, seq=seq)

    o_h = _hgrn(q_h, lf, k_h, v_h, act_gates, gv // HEAD_DIM, hgrn_norm, batch=batch, seq=seq)
    o_g = _gdn(qk_g, v_g, act_gates, gab_t, gdn_norm, batch=batch, seq=seq)
    return _merge(o_h, o_g, merge_gates, h, wbh, wbg, wo)


def kernel(x, ffn1_norm, ffn1_w_in, ffn1_w_out, mix_norm, w_in, hgrn_lb_logits, hgrn_out_norm, gdn_conv_w,
           gdn_a_log, gdn_dt_bias, gdn_out_norm, w_branch_hgrn, w_branch_gdn, w_out, ffn2_norm, ffn2_w_in,
           ffn2_w_out, final_norm):
    batch, seq, d = x.shape
    depth = ffn1_norm.shape[0]
    h = x.reshape(batch * seq, d)
    for l in range(depth):
        h, u = _ffn(h, ffn1_norm[l], ffn1_w_in[l], ffn1_w_out[l], mix_norm[l], final=False)
        h = _mixer(u, h, w_in[l], hgrn_lb_logits, hgrn_out_norm[l], gdn_conv_w[l], gdn_a_log[l],
                   gdn_dt_bias[l], gdn_out_norm[l], w_branch_hgrn[l], w_branch_gdn[l], w_out[l],
                   layer=l, batch=batch, seq=seq)
        last = l == depth - 1
        nxt = final_norm if last else ffn1_norm[l + 1]
        out = _ffn(h, ffn2_norm[l], ffn2_w_in[l], ffn2_w_out[l], nxt, final=last)
        h = out if last else out[0]
    return h.reshape(batch, seq, d)
```

```python
import functools

import jax
import jax.numpy as jnp
from jax import lax
from jax.experimental import pallas as pl
from jax.experimental.pallas import tpu as pltpu

F32 = jnp.float32
BF16 = jnp.bfloat16

EPS = 1e-6
CHUNK = 64
SUB = 8
HEAD_DIM = 128
HG_HEADS = 8
GDN_QK_HEADS = 8
GDN_V_HEADS = 16
CONV_K = 4
LANES = 128
SUBLANES = 8
NEG_BIG = -1e30
LOG2E = 1.4426950408889634
VMEM_LIMIT = 56 * 1024 * 1024


def _dot(a, b):
    return jnp.dot(a, b, preferred_element_type=F32)


def _dot_nt(a, b):
    return lax.dot_general(a, b, (((1,), (1,)), ((), ())), preferred_element_type=F32)


def _dot_tn(a, b):
    return lax.dot_general(a, b, (((0,), (0,)), ((), ())), preferred_element_type=F32)


def _sigmoid(x):
    return 1.0 / (1.0 + jnp.exp2(x * -LOG2E))


def _rms(x, w):
    ms = jnp.mean(x * x, axis=-1, keepdims=True)
    return x * lax.rsqrt(ms + EPS) * w


def _params(*sem):
    return pltpu.CompilerParams(dimension_semantics=sem, vmem_limit_bytes=VMEM_LIMIT)


def _resident(shape):
    return pl.BlockSpec(shape, lambda *_: (0,) * len(shape), pipeline_mode=pl.Buffered(1))


def _ffn_body(h_ref, nw_ref, wa_ref, wb_ref, wo_ref, onw_ref, *out_refs, ff_tile, final):
    h = h_ref[...]
    xn = _rms(h, nw_ref[...]).astype(BF16)
    d_ff = wa_ref.shape[1]
    acc = jnp.zeros(h.shape, F32)
    for j in range(0, d_ff, ff_tile):
        w = min(ff_tile, d_ff - j)
        a = _dot(xn, wa_ref[:, j:j + w])
        b = _dot(xn, wb_ref[:, j:j + w])
        g = (a * _sigmoid(a) * b).astype(BF16)
        acc = acc + _dot(g, wo_ref[j:j + w, :])
    hn = h + 0.5 * acc
    if final:
        out_refs[0][...] = _rms(hn, onw_ref[...])
    else:
        out_refs[0][...] = hn
        out_refs[1][...] = _rms(hn, onw_ref[...]).astype(BF16)


def _ffn(h, norm_w, w_in, w_out, next_norm_w, *, final, tm=512, ff_tile=512):
    n, d = h.shape
    d_ff = w_out.shape[0]
    wa = w_in[:, :d_ff].astype(BF16)
    wb = w_in[:, d_ff:].astype(BF16)
    wo = w_out.astype(BF16)
    row = pl.BlockSpec((tm, d), lambda i: (i, 0))
    if final:
        out_shape = jax.ShapeDtypeStruct((n, d), F32)
        out_specs = row
    else:
        out_shape = (jax.ShapeDtypeStruct((n, d), F32), jax.ShapeDtypeStruct((n, d), BF16))
        out_specs = (row, row)
    return pl.pallas_call(
        functools.partial(_ffn_body, ff_tile=ff_tile, final=final),
        out_shape=out_shape,
        grid=(n // tm,),
        in_specs=[row, _resident((1, d)), _resident((d, d_ff)), _resident((d, d_ff)),
                  _resident((d_ff, d)), _resident((1, d))],
        out_specs=out_specs,
        compiler_params=_params("arbitrary"),
        name="ffn_final" if final else "ffn",
    )(h, norm_w.reshape(1, d), wa, wb, wo, next_norm_w.reshape(1, d))


def _proj_body(u_ref, w_ref, logit_ref, cw_ref, gaux_ref,
               qh_ref, lf_ref, kh_ref, vh_ref, act_ref, mg_ref, qg_ref, kg_ref, vg_ref, gab_ref, ext_ref,
               *, layer, tiles_per_seq, tn, widths):
    w_hq, w_hf, w_hi, w_act, w_mg, w_qk, w_gv = widths
    u = u_ref[...]
    tm = u.shape[0]
    proj = lambda c0, width=tn: _dot(u, w_ref[:, c0:c0 + width])
    bases = [0]
    for w in widths:
        bases.append(bases[-1] + w)

    def hq_tile(o):
        p = proj(bases[0] + o)
        qh_ref[:, o:o + tn] = (p * _sigmoid(p) * HEAD_DIM ** -0.5).astype(BF16)

    lg = logit_ref[...]
    e = jnp.exp(lg - jnp.max(lg, axis=0, keepdims=True))
    lb_all = jnp.sum(e[0:layer + 1, :], axis=0, keepdims=True) / jnp.sum(e, axis=0, keepdims=True)

    def hf_tile(o):
        s = _sigmoid(proj(bases[1] + o))
        lb = lb_all[:, o:o + tn]
        lf_ref[:, o:o + tn] = jnp.log2(lb + (1.0 - lb) * s)
        kh_ref[:, o:o + tn] = ((1.0 - lb) * (1.0 - s)).astype(BF16)

    def hi_tile(o):
        vh_ref[:, o:o + tn] = proj(bases[2] + o).astype(BF16)

    def act_tile(o):
        p = proj(bases[3] + o)
        act_ref[:, o:o + tn] = (p * _sigmoid(p)).astype(BF16)

    def mg_tile(o):
        mg_ref[:, o:o + tn] = _sigmoid(proj(bases[4] + o)).astype(BF16)

    @pl.when(pl.program_id(0) % tiles_per_seq == 0)
    def _():
        ext_ref[0:SUBLANES, :] = jnp.zeros((SUBLANES, ext_ref.shape[1]), F32)

    def conv_tile(o):
        p = proj(bases[5] + o)
        ext_ref[SUBLANES:, o:o + tn] = p
        acc = p * cw_ref[CONV_K - 1:CONV_K, o:o + tn]
        for d in range(1, CONV_K):
            acc = acc + ext_ref[SUBLANES - d:SUBLANES - d + tm, o:o + tn] * cw_ref[CONV_K - 1 - d:CONV_K - d, o:o + tn]
        y = acc * _sigmoid(acc)
        if o < w_qk:
            is_q = o < w_qk // 2
            out_ref, o_out = (qg_ref, o) if is_q else (kg_ref, o - w_qk // 2)
            for s in range(0, tn, HEAD_DIM):
                ys = y[:, s:s + HEAD_DIM]
                inv = lax.rsqrt(jnp.sum(ys * ys, axis=-1, keepdims=True) + EPS) * (HEAD_DIM ** -0.5 if is_q else 1.0)
                out_ref[:, o_out + s:o_out + s + HEAD_DIM] = (ys * inv).astype(BF16)
        else:
            vg_ref[:, o - w_qk:o - w_qk + tn] = y.astype(BF16)

    light = [(f, o) for f, w in ((hq_tile, w_hq), (hf_tile, w_hf), (hi_tile, w_hi), (act_tile, w_act), (mg_tile, w_mg))
             for o in range(0, w, tn)]
    heavy = [(conv_tile, o) for o in range(0, w_qk + w_gv, tn)]
    per_heavy = -(-len(light) // len(heavy))
    while light or heavy:
        for f, o in light[:per_heavy]:
            f(o)
        light = light[per_heavy:]
        if heavy:
            f, o = heavy.pop(0)
            f(o)
    ext_ref[0:SUBLANES, :] = ext_ref[tm:tm + SUBLANES, :]

    p = proj(bases[7], LANES)
    x = p + gaux_ref[1:2, :]
    softplus = jnp.maximum(x, 0.0) + jnp.log1p(jnp.exp(-jnp.abs(x)))
    g = -jnp.exp(gaux_ref[0:1, :]) * softplus
    lane = lax.broadcasted_iota(jnp.int32, p.shape, 1)
    gab_ref[...] = jnp.where(lane % SUBLANES < 2, g, _sigmoid(p)).T


def _proj(u, w_all, lb_logits, cw, gaux, widths, *, layer, seq, tm=256, tn=256):
    n, d = u.shape
    w_hq, w_hf, w_hi, w_act, w_mg, w_qk, w_gv = widths
    row = lambda width: pl.BlockSpec((tm, width), lambda i: (i, 0))
    out_widths = (w_hq, w_hf, w_hf, w_hi, w_act, w_mg, w_qk // 2, w_qk // 2, w_gv)
    out_dtypes = (BF16, F32, BF16, BF16, BF16, BF16, BF16, BF16, BF16)
    out_shape = tuple(jax.ShapeDtypeStruct((n, w), dt) for w, dt in zip(out_widths, out_dtypes))
    out_shape += (jax.ShapeDtypeStruct((LANES, n), F32),)
    out_specs = tuple(row(w) for w in out_widths) + (pl.BlockSpec((LANES, tm), lambda i: (0, i)),)
    return pl.pallas_call(
        functools.partial(_proj_body, layer=layer, tiles_per_seq=seq // tm, tn=tn, widths=widths),
        out_shape=out_shape, grid=(n // tm,),
        in_specs=[row(d), _resident(w_all.shape), _resident(lb_logits.shape), _resident(cw.shape),
                  _resident(gaux.shape)],
        out_specs=out_specs,
        scratch_shapes=[pltpu.VMEM((tm + SUBLANES, w_qk + w_gv), F32)],
        compiler_params=_params("arbitrary"), name="proj",
    )(u, w_all, lb_logits, cw, gaux)


def _hgrn_body(q_ref, lf_ref, k_ref, v_ref, gate_ref, nw_ref, tri_ref, o_ref, st_ref, scores_ref, inter_ref, vlag_ref):
    @pl.when(pl.program_id(2) == 0)
    def _():
        for ref in (st_ref, scores_ref, inter_ref, vlag_ref):
            ref[...] = jnp.zeros_like(ref)

    ts = q_ref.shape[0]
    n_chunks = ts // CHUNK

    for c in range(n_chunks):
        r0 = c * CHUNK
        o = inter_ref[c] + _dot(scores_ref[c], vlag_ref[r0:r0 + CHUNK, :])
        o_ref[r0:r0 + CHUNK, :] = (_rms(o, nw_ref[...]) * gate_ref[r0:r0 + CHUNK, :].astype(F32)).astype(BF16)

    lf = lf_ref[...]
    hi = lf.astype(BF16)
    r1 = lf - hi.astype(F32)
    mid = r1.astype(BF16)
    lo = (r1 - mid.astype(F32)).astype(BF16)
    by_chunk = lambda x: jnp.concatenate([x[c * CHUNK:(c + 1) * CHUNK] for c in range(n_chunks)], axis=1)
    b_all = _dot(tri_ref[...], jnp.concatenate([by_chunk(hi), by_chunk(mid), by_chunk(lo)], axis=0))

    row = lax.broadcasted_iota(jnp.int32, (CHUNK, HEAD_DIM), 0)
    t_idx = lax.broadcasted_iota(jnp.int32, (CHUNK, CHUNK), 0)
    s_idx = lax.broadcasted_iota(jnp.int32, (CHUNK, CHUNK), 1)
    levels = []
    half = CHUNK // 2
    while half >= SUB:
        size = 2 * half
        levels.append((half, row % size >= half,
                       (t_idx // size == s_idx // size) & (t_idx % size >= half) & (s_idx % size < half)))
        half //= 2
    causal_bias = [jnp.where(row % SUB >= j, 0.0, NEG_BIG) for j in range(SUB)]
    place = [s_idx == (t_idx // SUB) * SUB + j for j in range(SUB)]
    vlag_ref[...] = v_ref[...]
    for c in range(n_chunks):
        r0 = c * CHUNK
        b = b_all[:, c * HEAD_DIM:(c + 1) * HEAD_DIM]
        q = q_ref[r0:r0 + CHUNK, :].astype(F32)
        k = k_ref[r0:r0 + CHUNK, :].astype(F32)
        v = v_ref[r0:r0 + CHUNK, :]
        b_end = b[CHUNK - 1:CHUNK, :]
        st = st_ref[...]
        inter_ref[c] = _dot_nt((q * jnp.exp2(b)).astype(BF16), st.astype(BF16))

        scores = jnp.zeros((CHUNK, CHUNK), F32)
        for half, past_mid, pair in levels:
            size = 2 * half
            b_mid = jnp.concatenate(
                [jnp.broadcast_to(b[m + half - 1:m + half, :], (size, HEAD_DIM)) for m in range(0, CHUNK, size)], axis=0)
            x = (jnp.where(past_mid, q, k) * jnp.exp2(-jnp.abs(b - b_mid))).astype(BF16)
            scores = jnp.where(pair, _dot_nt(x, x), scores)

        blocks = lambda x: x.reshape(CHUNK // SUB, SUB, HEAD_DIM)
        for j in range(SUB):
            pick = lambda x: jnp.broadcast_to(blocks(x)[:, j:j + 1, :], blocks(x).shape).reshape(CHUNK, HEAD_DIM)
            dec = jnp.exp2(b - pick(b) + causal_bias[j])
            col = jnp.sum(q * pick(k) * dec, axis=-1, keepdims=True)
            scores = jnp.where(place[j], col, scores)
        scores_ref[c] = scores.astype(BF16)

        k_end = (k * jnp.exp2(b_end - b)).astype(BF16)
        st_ref[...] = st * jnp.exp2(b_end) + _dot_tn(v, k_end)


def _hgrn(q, lf, k, v, gates, gate_col0, norm_w, *, batch, seq, ts=512):
    n = q.shape[0]
    nt = seq // ts
    n_chunks = ts // CHUNK
    fresh = pl.BlockSpec((ts, HEAD_DIM), lambda b, h, t: (b * nt + jnp.minimum(t, nt - 1), h))
    lagged = lambda col0: pl.BlockSpec((ts, HEAD_DIM), lambda b, h, t: (b * nt + jnp.maximum(t - 1, 0), h + col0))
    tri = (jnp.arange(CHUNK)[:, None] >= jnp.arange(CHUNK)[None, :]).astype(BF16)
    tri3 = jnp.concatenate([tri, tri, tri], axis=1)
    return pl.pallas_call(
        _hgrn_body, out_shape=jax.ShapeDtypeStruct((n, HG_HEADS * HEAD_DIM), BF16),
        grid=(batch, HG_HEADS, nt + 1),
        in_specs=[fresh, fresh, fresh, fresh, lagged(gate_col0), _resident((1, HEAD_DIM)),
                  _resident((CHUNK, 3 * CHUNK))],
        out_specs=lagged(0),
        scratch_shapes=[pltpu.VMEM((HEAD_DIM, HEAD_DIM), F32),
                        pltpu.VMEM((n_chunks, CHUNK, CHUNK), BF16),
                        pltpu.VMEM((n_chunks, CHUNK, HEAD_DIM), F32),
                        pltpu.VMEM((ts, HEAD_DIM), BF16)],
        compiler_params=_params("arbitrary", "arbitrary", "arbitrary"), name="hgrn2",
    )(q, lf, k, v, gates, norm_w.reshape(1, HEAD_DIM), tri3)


def _split3_bf16(x):
    hi = x.astype(BF16)
    r1 = x - hi.astype(F32)
    mid = r1.astype(BF16)
    lo = (r1 - mid.astype(F32)).astype(BF16)
    return hi, mid, lo


def _unit_lower_inverse_pairs(a_pairs):
    shape = a_pairs[0].shape
    row = lax.broadcasted_iota(jnp.int32, shape, 0)
    lane = lax.broadcasted_iota(jnp.int32, shape, 1)
    first = lane < CHUNK
    eye = jnp.where(row == lane % CHUNK, 1.0, 0.0)

    def level(power, partial):
        top = jnp.concatenate([jnp.where(first, power, 0.0), jnp.where(first, partial, 0.0)], axis=1)
        bottom = jnp.concatenate([jnp.where(first, 0.0, power), jnp.where(first, 0.0, partial)], axis=1)
        res = _dot(power.astype(BF16), jnp.concatenate([top, bottom], axis=0).astype(BF16))
        return res[:, :2 * CHUNK], partial + res[:, 2 * CHUNK:]

    state = [level(-a, eye) for a in a_pairs]
    span = 2
    while span < CHUNK:
        state = [level(power, partial) for power, partial in state]
        span *= 2
    return [partial for _, partial in state]


def _gdn_body(q_ref, k_ref, v_ref, z_ref, gb_ref, nw_ref, triu_ref, tril_ref, o_ref,
              s_ref, uw_ref, xt_ref, qg_ref, dec_ab_ref, lhs_ref, ktu_ref, qku_ref, dec_bc_ref):
    ts = q_ref.shape[0]
    n_chunks = ts // CHUNK
    heads = q_ref.shape[1] // HEAD_DIM
    rep = s_ref.shape[0] // heads
    dup = 2 * CHUNK
    pairs = [(h, c) for h in range(heads) for c in range(n_chunks)]
    prob = lambda h, c, i: (h * n_chunks + c) * rep + i

    @pl.when(pl.program_id(2) == 0)
    def _():
        for ref in (s_ref, uw_ref, xt_ref, qg_ref, dec_ab_ref, lhs_ref, ktu_ref, qku_ref, dec_bc_ref):
            ref[...] = jnp.zeros_like(ref)

    g_stack = jnp.concatenate([gb_ref[SUBLANES * h:SUBLANES * (h + 1), c * CHUNK:(c + 1) * CHUNK]
                               for h, c in pairs], axis=0)
    g_cols = jnp.concatenate([g_stack, jnp.zeros_like(g_stack)], axis=1).T[:CHUNK]
    gam_rows = _dot(jnp.concatenate(_split3_bf16(g_stack), axis=1), triu_ref[...])
    gam_cols = _dot(tril_ref[...], jnp.concatenate(_split3_bf16(g_cols), axis=0))
    t_dup = lax.broadcasted_iota(jnp.int32, (CHUNK, dup), 0)
    s_dup_raw = lax.broadcasted_iota(jnp.int32, (CHUNK, dup), 1)
    s_dup = s_dup_raw % CHUNK
    s_idx = lax.broadcasted_iota(jnp.int32, (CHUNK, CHUNK), 0)
    t_idx = lax.broadcasted_iota(jnp.int32, (CHUNK, CHUNK), 1)
    first = s_dup_raw < CHUNK
    a_pairs, rhs_all, xt_all, qg_all, dec_all = [], {}, {}, {}, {}
    for h, c in pairs:
        r0 = c * CHUNK
        q = q_ref[r0:r0 + CHUNK, h * HEAD_DIM:(h + 1) * HEAD_DIM]
        k = k_ref[r0:r0 + CHUNK, h * HEAD_DIM:(h + 1) * HEAD_DIM]
        qf = q.astype(F32)
        kf = k.astype(F32)
        kkq = _dot_nt(k, jnp.concatenate([k, k, q, q], axis=0))
        kk_dup = kkq[:, :dup]
        kq = kkq[:, dup:dup + CHUNK]
        col0 = SUBLANES * (h * n_chunks + c)
        gam_c = [gam_cols[:, col0 + i:col0 + i + 1] for i in range(rep)]
        beta = [g_cols[:, col0 + rep + i:col0 + rep + i + 1] for i in range(rep)]
        gam_r_pair = jnp.where(first[0:1], gam_rows[col0:col0 + 1, :], gam_rows[col0 + 1:col0 + 2, :])
        lmat_pair = jnp.exp(jnp.where(t_dup >= s_dup, jnp.where(first, gam_c[0], gam_c[1]) - gam_r_pair, NEG_BIG))
        a_pairs.append(jnp.where(t_dup > s_dup, jnp.where(first, beta[0], beta[1]) * kk_dup * lmat_pair, 0.0))
        for i in range(rep):
            p = prob(h, c, i)
            vcol = (h * rep + i) * HEAD_DIM
            gam_r = gam_rows[col0 + i:col0 + i + 1, :CHUNK]
            g_end = gam_c[i][CHUNK - 1:CHUNK, :]
            lmat_t = jnp.exp(jnp.where(t_idx >= s_idx, gam_r - gam_c[i], NEG_BIG))
            e_gam = jnp.exp(gam_c[i])
            v = v_ref[r0:r0 + CHUNK, vcol:vcol + HEAD_DIM].astype(F32)
            rhs_all[p] = jnp.concatenate([(v * beta[i]).astype(BF16), (kf * (beta[i] * e_gam)).astype(BF16)], axis=1)
            k_end = kf * jnp.exp(g_end - gam_c[i])
            xt_all[p] = jnp.concatenate([k_end, kq * lmat_t], axis=1).T.astype(BF16)
            qg_all[p] = qf * e_gam
            dec_all[p] = jnp.broadcast_to(jnp.exp(g_end), dec_ab_ref.shape[1:])

    zero_state = jnp.zeros((HEAD_DIM, HEAD_DIM), BF16)
    states = [s_ref[j] for j in range(heads * rep)]
    for c in range(n_chunks):
        r0 = c * CHUNK
        for h in range(heads):
            diag = jnp.concatenate(
                [jnp.concatenate([states[h * rep + i].astype(BF16) if j == i else zero_state for j in range(rep)],
                                 axis=1) for i in range(rep)], axis=0)
            m_all = _dot(lhs_ref[h * n_chunks + c], diag)
            for i in range(rep):
                p = prob(h, c, i)
                vcol = (h * rep + i) * HEAD_DIM
                m = m_all[:, i * HEAD_DIM:(i + 1) * HEAD_DIM]
                states[h * rep + i] = states[h * rep + i] * dec_bc_ref[p][0:1, 0:1] + ktu_ref[p] - m[:HEAD_DIM]
                o = m[HEAD_DIM:] + qku_ref[p]
                z = z_ref[r0:r0 + CHUNK, vcol:vcol + HEAD_DIM].astype(F32)
                o_ref[r0:r0 + CHUNK, vcol:vcol + HEAD_DIM] = (_rms(o, nw_ref[...]) * z).astype(BF16)
    for j in range(heads * rep):
        s_ref[j] = states[j]

    for h, c in pairs:
        lhs = []
        for i in range(rep):
            p = prob(h, c, i)
            r = _dot(xt_ref[p], uw_ref[p])
            ktu_ref[p] = r[:HEAD_DIM, :HEAD_DIM]
            qku_ref[p] = r[HEAD_DIM:, :HEAD_DIM]
            dec_bc_ref[p] = dec_ab_ref[p]
            lhs.append(jnp.concatenate([r[:HEAD_DIM, HEAD_DIM:], qg_ref[p] - r[HEAD_DIM:, HEAD_DIM:]], axis=0))
        lhs_ref[h * n_chunks + c] = jnp.concatenate(lhs, axis=1).astype(BF16)

    for p in range(len(rhs_all)):
        xt_ref[p] = xt_all[p]
        qg_ref[p] = qg_all[p]
        dec_ab_ref[p] = dec_all[p]
    inverses = _unit_lower_inverse_pairs(a_pairs)
    for n, (h, c) in enumerate(pairs):
        inv_pair = inverses[n].astype(BF16)
        for i in range(rep):
            p = prob(h, c, i)
            pad = jnp.zeros_like(rhs_all[p])
            stacked = jnp.concatenate([rhs_all[p], pad] if i == 0 else [pad, rhs_all[p]], axis=0)
            uw_ref[p] = _dot(inv_pair, stacked).astype(BF16)


def _gdn(q, k, v, gates, gab_t, norm_w, *, batch, seq, ts=128, heads=8):
    n = q.shape[0]
    nt = seq // ts
    groups = GDN_QK_HEADS // heads
    rep = GDN_V_HEADS // GDN_QK_HEADS
    n_chunks = ts // CHUNK
    n_pairs = heads * n_chunks
    n_probs = n_pairs * rep
    lag = 2
    fresh = lambda width, col0: pl.BlockSpec(
        (ts, width * HEAD_DIM), lambda b, g, t: (b * nt + jnp.minimum(t, nt - 1), g + col0))
    lagged = lambda width, col0: pl.BlockSpec(
        (ts, width * HEAD_DIM), lambda b, g, t: (b * nt + jnp.maximum(t - lag, 0), g + col0))
    r = jnp.arange(CHUNK)[:, None]
    c = jnp.arange(CHUNK)[None, :]
    upper = (r <= c).astype(BF16)
    upper_dup = jnp.concatenate([upper, upper], axis=1)
    triu3 = jnp.concatenate([upper_dup] * 3, axis=0)
    tril3 = jnp.concatenate([(r >= c).astype(BF16)] * 3, axis=1)
    return pl.pallas_call(
        _gdn_body, out_shape=jax.ShapeDtypeStruct((n, GDN_V_HEADS * HEAD_DIM), BF16),
        grid=(batch, groups, nt + lag),
        in_specs=[fresh(heads, 0), fresh(heads, 0), fresh(heads * rep, 0), lagged(heads * rep, 0),
                  pl.BlockSpec((SUBLANES * heads, ts), lambda b, g, t: (g, b * nt + jnp.minimum(t, nt - 1))),
                  _resident((1, HEAD_DIM)), _resident(triu3.shape), _resident(tril3.shape)],
        out_specs=lagged(heads * rep, 0),
        scratch_shapes=[pltpu.VMEM((heads * rep, HEAD_DIM, HEAD_DIM), F32),
                        pltpu.VMEM((n_probs, CHUNK, 2 * HEAD_DIM), BF16),
                        pltpu.VMEM((n_probs, HEAD_DIM + CHUNK, CHUNK), BF16),
                        pltpu.VMEM((n_probs, CHUNK, HEAD_DIM), F32),
                        pltpu.VMEM((n_probs, SUBLANES, LANES), F32),
                        pltpu.VMEM((n_pairs, HEAD_DIM + CHUNK, rep * HEAD_DIM), BF16),
                        pltpu.VMEM((n_probs, HEAD_DIM, HEAD_DIM), F32),
                        pltpu.VMEM((n_probs, CHUNK, HEAD_DIM), F32),
                        pltpu.VMEM((n_probs, SUBLANES, LANES), F32)],
        compiler_params=_params("arbitrary", "arbitrary", "arbitrary"), name="gated_delta",
    )(q, k, v, gates, gab_t, norm_w.reshape(1, HEAD_DIM), triu3, tril3)


def _merge_body(oh_ref, og_ref, gate_ref, h_ref, wbh_ref, wbg_ref, wo_ref, o_ref):
    d = h_ref.shape[1]
    yh = _dot(oh_ref[...], wbh_ref[...])
    yg = _dot(og_ref[...], wbg_ref[...])
    y = gate_ref[:, :d].astype(F32) * yh + gate_ref[:, d:].astype(F32) * yg
    o_ref[...] = h_ref[...] + _dot(y.astype(BF16), wo_ref[...])


def _merge(o_h, o_g, merge_gates, h, wbh, wbg, wo, *, tm=512):
    n, d = h.shape
    row = lambda width: pl.BlockSpec((tm, width), lambda i: (i, 0))
    return pl.pallas_call(
        _merge_body, out_shape=jax.ShapeDtypeStruct((n, d), F32), grid=(n // tm,),
        in_specs=[row(o_h.shape[1]), row(o_g.shape[1]), row(2 * d), row(d),
                  _resident(wbh.shape), _resident(wbg.shape), _resident(wo.shape)],
        out_specs=row(d),
        compiler_params=_params("arbitrary"), name="merge",
    )(o_h, o_g, merge_gates, h, wbh.astype(BF16), wbg.astype(BF16), wo.astype(BF16))


def _mixer(u, h, w_in, lb_logits, hgrn_norm, conv_w, a_log, dt_bias, gdn_norm, wbh, wbg, wo, *, layer, batch, seq):
    n, d = h.shape
    hk = HG_HEADS * HEAD_DIM
    gk = GDN_QK_HEADS * HEAD_DIM
    gv = GDN_V_HEADS * HEAD_DIM
    sizes = (hk, hk, hk, hk, gk, gk, gv, GDN_V_HEADS, GDN_V_HEADS, gv, d, d)
    offs = [0]
    for s in sizes:
        offs.append(offs[-1] + s)
    col = lambda a, b: w_in[:, offs[a]:offs[b]]
    conv_pad = jnp.zeros((SUBLANES - CONV_K, conv_w.shape[1]), F32)
    cw = jnp.concatenate([conv_w, conv_pad], axis=0)

    rep = GDN_V_HEADS // GDN_QK_HEADS
    pad = SUBLANES - 2 * rep
    regroup = lambda a, b, fill: jnp.concatenate(
        [a.reshape(-1, GDN_QK_HEADS, rep), b.reshape(-1, GDN_QK_HEADS, rep),
         jnp.full((a.shape[0], GDN_QK_HEADS, pad), fill, a.dtype)], axis=2).reshape(a.shape[0], -1)
    w_ga = w_in[:, offs[7]:offs[8]]
    w_gb = w_in[:, offs[8]:offs[9]]
    w_gab = regroup(w_ga, w_gb, 0.0)
    w_gab = jnp.pad(w_gab, ((0, 0), (0, LANES - w_gab.shape[1])))
    zeros = jnp.zeros((1, GDN_V_HEADS), F32)
    aux = jnp.concatenate([regroup(a_log.reshape(1, -1).astype(F32), zeros, 0.0),
                           regroup(dt_bias.reshape(1, -1).astype(F32), zeros, 0.0)], axis=0)
    aux = jnp.pad(aux, ((0, 0), (0, LANES - aux.shape[1])))

    w_all = jnp.concatenate([col(0, 1), col(1, 2), col(2, 3), col(9, 10), col(3, 4), col(10, 12), col(4, 6),
                             col(6, 7), w_gab], axis=1).astype(BF16)
    widths = (hk, hk, hk, gv + hk, 2 * d, 2 * gk, gv)
    q_h, lf, k_h, v_h, act_gates, merge_gates, q_g, k_g, v_g, gab_t = _proj(
        u, w_all, lb_logits.astype(F32), cw, aux, widths, layer=layer, seq=seq)

    o_h = _hgrn(q_h, lf, k_h, v_h, act_gates, gv // HEAD_DIM, hgrn_norm, batch=batch, seq=seq)
    o_g = _gdn(q_g, k_g, v_g, act_gates, gab_t, gdn_norm, batch=batch, seq=seq)
    return _merge(o_h, o_g, merge_gates, h, wbh, wbg, wo)


def kernel(x, ffn1_norm, ffn1_w_in, ffn1_w_out, mix_norm, w_in, hgrn_lb_logits, hgrn_out_norm, gdn_conv_w,
           gdn_a_log, gdn_dt_bias, gdn_out_norm, w_branch_hgrn, w_branch_gdn, w_out, ffn2_norm, ffn2_w_in,
           ffn2_w_out, final_norm):
    batch, seq, d = x.shape
    depth = ffn1_norm.shape[0]
    h = x.reshape(batch * seq, d)
    for l in range(depth):
        h, u = _ffn(h, ffn1_norm[l], ffn1_w_in[l], ffn1_w_out[l], mix_norm[l], final=False)
        h = _mixer(u, h, w_in[l], hgrn_lb_logits, hgrn_out_norm[l], gdn_conv_w[l], gdn_a_log[l],
                   gdn_dt_bias[l], gdn_out_norm[l], w_branch_hgrn[l], w_branch_gdn[l], w_out[l],
                   layer=l, batch=batch, seq=seq)
        last = l == depth - 1
        nxt = final_norm if last else ffn1_norm[l + 1]
        out = _ffn(h, ffn2_norm[l], ffn2_w_in[l], ffn2_w_out[l], nxt, final=last)
        h = out if last else out[0]
    return h.reshape(batch, seq, d)
```

```python
import functools

import jax
import jax.numpy as jnp
from jax import lax
from jax.experimental import pallas as pl
from jax.experimental.pallas import tpu as pltpu

F32 = jnp.float32
BF16 = jnp.bfloat16

EPS = 1e-6
CHUNK = 64
SUB = 8
HEAD_DIM = 128
HG_HEADS = 8
GDN_QK_HEADS = 8
GDN_V_HEADS = 16
CONV_K = 4
LANES = 128
SUBLANES = 8
NEG_BIG = -1e30
LOG2E = 1.4426950408889634
VMEM_LIMIT = 56 * 1024 * 1024


def _dot(a, b):
    return jnp.dot(a, b, preferred_element_type=F32)


def _dot_nt(a, b):
    return lax.dot_general(a, b, (((1,), (1,)), ((), ())), preferred_element_type=F32)


def _dot_tn(a, b):
    return lax.dot_general(a, b, (((0,), (0,)), ((), ())), preferred_element_type=F32)


def _sigmoid(x):
    return 1.0 / (1.0 + jnp.exp2(x * -LOG2E))


def _rms(x, w):
    ms = jnp.mean(x * x, axis=-1, keepdims=True)
    return x * lax.rsqrt(ms + EPS) * w


def _params(*sem):
    return pltpu.CompilerParams(dimension_semantics=sem, vmem_limit_bytes=VMEM_LIMIT)


def _resident(shape):
    return pl.BlockSpec(shape, lambda *_: (0,) * len(shape), pipeline_mode=pl.Buffered(1))


def _ffn_body(h_ref, nw_ref, wa_ref, wb_ref, wo_ref, onw_ref, *out_refs, ff_tile, final):
    h = h_ref[...]
    xn = _rms(h, nw_ref[...]).astype(BF16)
    d_ff = wa_ref.shape[1]
    acc = jnp.zeros(h.shape, F32)
    for j in range(0, d_ff, ff_tile):
        w = min(ff_tile, d_ff - j)
        a = _dot(xn, wa_ref[:, j:j + w])
        b = _dot(xn, wb_ref[:, j:j + w])
        g = (a * _sigmoid(a) * b).astype(BF16)
        acc = acc + _dot(g, wo_ref[j:j + w, :])
    hn = h + 0.5 * acc
    if final:
        out_refs[0][...] = _rms(hn, onw_ref[...])
    else:
        out_refs[0][...] = hn
        out_refs[1][...] = _rms(hn, onw_ref[...]).astype(BF16)


def _ffn(h, norm_w, w_in, w_out, next_norm_w, *, final, tm=512, ff_tile=512):
    n, d = h.shape
    d_ff = w_out.shape[0]
    wa = w_in[:, :d_ff].astype(BF16)
    wb = w_in[:, d_ff:].astype(BF16)
    wo = w_out.astype(BF16)
    row = pl.BlockSpec((tm, d), lambda i: (i, 0))
    if final:
        out_shape = jax.ShapeDtypeStruct((n, d), F32)
        out_specs = row
    else:
        out_shape = (jax.ShapeDtypeStruct((n, d), F32), jax.ShapeDtypeStruct((n, d), BF16))
        out_specs = (row, row)
    return pl.pallas_call(
        functools.partial(_ffn_body, ff_tile=ff_tile, final=final),
        out_shape=out_shape,
        grid=(n // tm,),
        in_specs=[row, _resident((1, d)), _resident((d, d_ff)), _resident((d, d_ff)),
                  _resident((d_ff, d)), _resident((1, d))],
        out_specs=out_specs,
        compiler_params=_params("arbitrary"),
        name="ffn_final" if final else "ffn",
    )(h, norm_w.reshape(1, d), wa, wb, wo, next_norm_w.reshape(1, d))


def _proj_body(u_ref, w_ref, logit_ref, cw_ref, gaux_ref,
               qh_ref, lf_ref, kh_ref, vh_ref, act_ref, mg_ref, qg_ref, kg_ref, vg_ref, gab_ref, ext_ref,
               *, layer, tiles_per_seq, tn, widths):
    w_hq, w_hf, w_hi, w_act, w_mg, w_qk, w_gv = widths
    u = u_ref[...]
    tm = u.shape[0]
    proj = lambda c0, width=tn: _dot(u, w_ref[:, c0:c0 + width])
    bases = [0]
    for w in widths:
        bases.append(bases[-1] + w)

    def hq_tile(o):
        p = proj(bases[0] + o)
        qh_ref[:, o:o + tn] = (p * _sigmoid(p) * HEAD_DIM ** -0.5).astype(BF16)

    lg = logit_ref[...]
    e = jnp.exp(lg - jnp.max(lg, axis=0, keepdims=True))
    lb_all = jnp.sum(e[0:layer + 1, :], axis=0, keepdims=True) / jnp.sum(e, axis=0, keepdims=True)

    def hf_tile(o):
        s = _sigmoid(proj(bases[1] + o))
        lb = lb_all[:, o:o + tn]
        lf_ref[:, o:o + tn] = jnp.log2(lb + (1.0 - lb) * s)
        kh_ref[:, o:o + tn] = ((1.0 - lb) * (1.0 - s)).astype(BF16)

    def hi_tile(o):
        vh_ref[:, o:o + tn] = proj(bases[2] + o).astype(BF16)

    def act_tile(o):
        p = proj(bases[3] + o)
        act_ref[:, o:o + tn] = (p * _sigmoid(p)).astype(BF16)

    def mg_tile(o):
        mg_ref[:, o:o + tn] = _sigmoid(proj(bases[4] + o)).astype(BF16)

    @pl.when(pl.program_id(0) % tiles_per_seq == 0)
    def _():
        ext_ref[0:SUBLANES, :] = jnp.zeros((SUBLANES, ext_ref.shape[1]), F32)

    def conv_tile(o):
        p = proj(bases[5] + o)
        ext_ref[SUBLANES:, o:o + tn] = p
        acc = p * cw_ref[CONV_K - 1:CONV_K, o:o + tn]
        for d in range(1, CONV_K):
            acc = acc + ext_ref[SUBLANES - d:SUBLANES - d + tm, o:o + tn] * cw_ref[CONV_K - 1 - d:CONV_K - d, o:o + tn]
        y = acc * _sigmoid(acc)
        if o < w_qk:
            is_q = o < w_qk // 2
            out_ref, o_out = (qg_ref, o) if is_q else (kg_ref, o - w_qk // 2)
            for s in range(0, tn, HEAD_DIM):
                ys = y[:, s:s + HEAD_DIM]
                inv = lax.rsqrt(jnp.sum(ys * ys, axis=-1, keepdims=True) + EPS) * (HEAD_DIM ** -0.5 if is_q else 1.0)
                out_ref[:, o_out + s:o_out + s + HEAD_DIM] = (ys * inv).astype(BF16)
        else:
            vg_ref[:, o - w_qk:o - w_qk + tn] = y.astype(BF16)

    light = [(f, o) for f, w in ((hq_tile, w_hq), (hf_tile, w_hf), (hi_tile, w_hi), (act_tile, w_act), (mg_tile, w_mg))
             for o in range(0, w, tn)]
    heavy = [(conv_tile, o) for o in range(0, w_qk + w_gv, tn)]
    per_heavy = -(-len(light) // len(heavy))
    while light or heavy:
        for f, o in light[:per_heavy]:
            f(o)
        light = light[per_heavy:]
        if heavy:
            f, o = heavy.pop(0)
            f(o)
    ext_ref[0:SUBLANES, :] = ext_ref[tm:tm + SUBLANES, :]

    p = proj(bases[7], LANES)
    x = p + gaux_ref[1:2, :]
    softplus = jnp.maximum(x, 0.0) + jnp.log1p(jnp.exp(-jnp.abs(x)))
    g = -jnp.exp(gaux_ref[0:1, :]) * softplus
    lane = lax.broadcasted_iota(jnp.int32, p.shape, 1)
    gab_ref[...] = jnp.where(lane % SUBLANES < 2, g, _sigmoid(p)).T


def _proj(u, w_all, lb_logits, cw, gaux, widths, *, layer, seq, tm=256, tn=256):
    n, d = u.shape
    w_hq, w_hf, w_hi, w_act, w_mg, w_qk, w_gv = widths
    row = lambda width: pl.BlockSpec((tm, width), lambda i: (i, 0))
    out_widths = (w_hq, w_hf, w_hf, w_hi, w_act, w_mg, w_qk // 2, w_qk // 2, w_gv)
    out_dtypes = (BF16, F32, BF16, BF16, BF16, BF16, BF16, BF16, BF16)
    out_shape = tuple(jax.ShapeDtypeStruct((n, w), dt) for w, dt in zip(out_widths, out_dtypes))
    out_shape += (jax.ShapeDtypeStruct((LANES, n), F32),)
    out_specs = tuple(row(w) for w in out_widths) + (pl.BlockSpec((LANES, tm), lambda i: (0, i)),)
    return pl.pallas_call(
        functools.partial(_proj_body, layer=layer, tiles_per_seq=seq // tm, tn=tn, widths=widths),
        out_shape=out_shape, grid=(n // tm,),
        in_specs=[row(d), _resident(w_all.shape), _resident(lb_logits.shape), _resident(cw.shape),
                  _resident(gaux.shape)],
        out_specs=out_specs,
        scratch_shapes=[pltpu.VMEM((tm + SUBLANES, w_qk + w_gv), F32)],
        compiler_params=_params("arbitrary"), name="proj",
    )(u, w_all, lb_logits, cw, gaux)


def _hgrn_body(q_ref, lf_ref, k_ref, v_ref, gate_ref, nw_ref, tri_ref, o_ref, st_ref, scores_ref, inter_ref, vlag_ref,
               *, blocks_per_seq):
    step = pl.program_id(0)

    @pl.when(step == 0)
    def _():
        for ref in (scores_ref, inter_ref, vlag_ref):
            ref[...] = jnp.zeros_like(ref)

    @pl.when(step % blocks_per_seq == 0)
    def _():
        st_ref[...] = jnp.zeros_like(st_ref)

    ts = q_ref.shape[0]
    n_chunks = ts // CHUNK

    for c in range(n_chunks):
        r0 = c * CHUNK
        o = inter_ref[c] + _dot(scores_ref[c], vlag_ref[r0:r0 + CHUNK, :])
        o_ref[r0:r0 + CHUNK, :] = (_rms(o, nw_ref[...]) * gate_ref[r0:r0 + CHUNK, :].astype(F32)).astype(BF16)

    lf = lf_ref[...]
    hi = lf.astype(BF16)
    r1 = lf - hi.astype(F32)
    mid = r1.astype(BF16)
    lo = (r1 - mid.astype(F32)).astype(BF16)
    by_chunk = lambda x: jnp.concatenate([x[c * CHUNK:(c + 1) * CHUNK] for c in range(n_chunks)], axis=1)
    b_all = _dot(tri_ref[...], jnp.concatenate([by_chunk(hi), by_chunk(mid), by_chunk(lo)], axis=0))

    row = lax.broadcasted_iota(jnp.int32, (CHUNK, HEAD_DIM), 0)
    t_idx = lax.broadcasted_iota(jnp.int32, (CHUNK, CHUNK), 0)
    s_idx = lax.broadcasted_iota(jnp.int32, (CHUNK, CHUNK), 1)
    levels = []
    half = CHUNK // 2
    while half >= SUB:
        size = 2 * half
        levels.append((half, row % size >= half,
                       (t_idx // size == s_idx // size) & (t_idx % size >= half) & (s_idx % size < half)))
        half //= 2
    causal_bias = [jnp.where(row % SUB >= j, 0.0, NEG_BIG) for j in range(SUB)]
    place = [s_idx == (t_idx // SUB) * SUB + j for j in range(SUB)]
    vlag_ref[...] = v_ref[...]
    for c in range(n_chunks):
        r0 = c * CHUNK
        b = b_all[:, c * HEAD_DIM:(c + 1) * HEAD_DIM]
        q = q_ref[r0:r0 + CHUNK, :].astype(F32)
        k = k_ref[r0:r0 + CHUNK, :].astype(F32)
        v = v_ref[r0:r0 + CHUNK, :]
        b_end = b[CHUNK - 1:CHUNK, :]
        st = st_ref[...]
        inter_ref[c] = _dot_nt((q * jnp.exp2(b)).astype(BF16), st.astype(BF16))

        scores = jnp.zeros((CHUNK, CHUNK), F32)
        for half, past_mid, pair in levels:
            size = 2 * half
            b_mid = jnp.concatenate(
                [jnp.broadcast_to(b[m + half - 1:m + half, :], (size, HEAD_DIM)) for m in range(0, CHUNK, size)], axis=0)
            x = (jnp.where(past_mid, q, k) * jnp.exp2(-jnp.abs(b - b_mid))).astype(BF16)
            scores = jnp.where(pair, _dot_nt(x, x), scores)

        blocks = lambda x: x.reshape(CHUNK // SUB, SUB, HEAD_DIM)
        for j in range(SUB):
            pick = lambda x: jnp.broadcast_to(blocks(x)[:, j:j + 1, :], blocks(x).shape).reshape(CHUNK, HEAD_DIM)
            dec = jnp.exp2(b - pick(b) + causal_bias[j])
            col = jnp.sum(q * pick(k) * dec, axis=-1, keepdims=True)
            scores = jnp.where(place[j], col, scores)
        scores_ref[c] = scores.astype(BF16)

        k_end = (k * jnp.exp2(b_end - b)).astype(BF16)
        st_ref[...] = st * jnp.exp2(b_end) + _dot_tn(v, k_end)


def _hgrn(q, lf, k, v, gates, gate_col0, norm_w, *, batch, seq, ts=512):
    n = q.shape[0]
    nt = seq // ts
    n_chunks = ts // CHUNK
    total = batch * HG_HEADS * nt

    def block(step, col0):
        step = jnp.clip(step, 0, total - 1)
        return (step // (HG_HEADS * nt)) * nt + step % nt, (step // nt) % HG_HEADS + col0

    fresh = pl.BlockSpec((ts, HEAD_DIM), lambda s: block(s, 0))
    lagged = lambda col0: pl.BlockSpec((ts, HEAD_DIM), lambda s: block(s - 1, col0))
    tri = (jnp.arange(CHUNK)[:, None] >= jnp.arange(CHUNK)[None, :]).astype(BF16)
    tri3 = jnp.concatenate([tri, tri, tri], axis=1)
    return pl.pallas_call(
        functools.partial(_hgrn_body, blocks_per_seq=nt),
        out_shape=jax.ShapeDtypeStruct((n, HG_HEADS * HEAD_DIM), BF16),
        grid=(total + 1,),
        in_specs=[fresh, fresh, fresh, fresh, lagged(gate_col0), _resident((1, HEAD_DIM)),
                  _resident((CHUNK, 3 * CHUNK))],
        out_specs=lagged(0),
        scratch_shapes=[pltpu.VMEM((HEAD_DIM, HEAD_DIM), F32),
                        pltpu.VMEM((n_chunks, CHUNK, CHUNK), BF16),
                        pltpu.VMEM((n_chunks, CHUNK, HEAD_DIM), F32),
                        pltpu.VMEM((ts, HEAD_DIM), BF16)],
        compiler_params=_params("arbitrary"), name="hgrn2",
    )(q, lf, k, v, gates, norm_w.reshape(1, HEAD_DIM), tri3)


def _split3_bf16(x):
    hi = x.astype(BF16)
    r1 = x - hi.astype(F32)
    mid = r1.astype(BF16)
    lo = (r1 - mid.astype(F32)).astype(BF16)
    return hi, mid, lo


def _unit_lower_inverse_pairs(a_pairs):
    shape = a_pairs[0].shape
    row = lax.broadcasted_iota(jnp.int32, shape, 0)
    lane = lax.broadcasted_iota(jnp.int32, shape, 1)
    first = lane < CHUNK
    eye = jnp.where(row == lane % CHUNK, 1.0, 0.0)

    def level(power, partial):
        top = jnp.concatenate([jnp.where(first, power, 0.0), jnp.where(first, partial, 0.0)], axis=1)
        bottom = jnp.concatenate([jnp.where(first, 0.0, power), jnp.where(first, 0.0, partial)], axis=1)
        res = _dot(power.astype(BF16), jnp.concatenate([top, bottom], axis=0).astype(BF16))
        return res[:, :2 * CHUNK], partial + res[:, 2 * CHUNK:]

    state = [level(-a, eye) for a in a_pairs]
    span = 2
    while span < CHUNK:
        state = [level(power, partial) for power, partial in state]
        span *= 2
    return [partial for _, partial in state]


def _gdn_body(q_ref, k_ref, v_ref, z_ref, gb_ref, nw_ref, triu_ref, tril_ref, o_ref,
              s_ref, uw_ref, xt_ref, qg_ref, dec_ab_ref, lhs_ref, ktu_ref, qku_ref, dec_bc_ref,
              *, blocks_per_seq, lag):
    ts = q_ref.shape[0]
    n_chunks = ts // CHUNK
    heads = q_ref.shape[1] // HEAD_DIM
    rep = s_ref.shape[0] // heads
    dup = 2 * CHUNK
    pairs = [(h, c) for h in range(heads) for c in range(n_chunks)]
    prob = lambda h, c, i: (h * n_chunks + c) * rep + i

    step = pl.program_id(0)

    @pl.when(step == 0)
    def _():
        for ref in (uw_ref, xt_ref, qg_ref, dec_ab_ref, lhs_ref, ktu_ref, qku_ref, dec_bc_ref):
            ref[...] = jnp.zeros_like(ref)

    @pl.when((step == 0) | (step % blocks_per_seq == lag % blocks_per_seq))
    def _():
        s_ref[...] = jnp.zeros_like(s_ref)

    g_stack = jnp.concatenate([gb_ref[SUBLANES * h:SUBLANES * (h + 1), c * CHUNK:(c + 1) * CHUNK]
                               for h, c in pairs], axis=0)
    g_cols = jnp.concatenate([g_stack, jnp.zeros_like(g_stack)], axis=1).T[:CHUNK]
    gam_rows = _dot(jnp.concatenate(_split3_bf16(g_stack), axis=1), triu_ref[...])
    gam_cols = _dot(tril_ref[...], jnp.concatenate(_split3_bf16(g_cols), axis=0))
    t_dup = lax.broadcasted_iota(jnp.int32, (CHUNK, dup), 0)
    s_dup_raw = lax.broadcasted_iota(jnp.int32, (CHUNK, dup), 1)
    s_dup = s_dup_raw % CHUNK
    s_idx = lax.broadcasted_iota(jnp.int32, (CHUNK, CHUNK), 0)
    t_idx = lax.broadcasted_iota(jnp.int32, (CHUNK, CHUNK), 1)
    first = s_dup_raw < CHUNK
    a_pairs, rhs_all, xt_all, qg_all, dec_all = [], {}, {}, {}, {}
    for h, c in pairs:
        r0 = c * CHUNK
        q = q_ref[r0:r0 + CHUNK, h * HEAD_DIM:(h + 1) * HEAD_DIM]
        k = k_ref[r0:r0 + CHUNK, h * HEAD_DIM:(h + 1) * HEAD_DIM]
        qf = q.astype(F32)
        kf = k.astype(F32)
        kkq = _dot_nt(k, jnp.concatenate([k, k, q, q], axis=0))
        kk_dup = kkq[:, :dup]
        kq = kkq[:, dup:dup + CHUNK]
        col0 = SUBLANES * (h * n_chunks + c)
        gam_c = [gam_cols[:, col0 + i:col0 + i + 1] for i in range(rep)]
        beta = [g_cols[:, col0 + rep + i:col0 + rep + i + 1] for i in range(rep)]
        gam_r_pair = jnp.where(first[0:1], gam_rows[col0:col0 + 1, :], gam_rows[col0 + 1:col0 + 2, :])
        lmat_pair = jnp.exp(jnp.where(t_dup >= s_dup, jnp.where(first, gam_c[0], gam_c[1]) - gam_r_pair, NEG_BIG))
        a_pairs.append(jnp.where(t_dup > s_dup, jnp.where(first, beta[0], beta[1]) * kk_dup * lmat_pair, 0.0))
        for i in range(rep):
            p = prob(h, c, i)
            vcol = (h * rep + i) * HEAD_DIM
            gam_r = gam_rows[col0 + i:col0 + i + 1, :CHUNK]
            g_end = gam_c[i][CHUNK - 1:CHUNK, :]
            lmat_t = jnp.exp(jnp.where(t_idx >= s_idx, gam_r - gam_c[i], NEG_BIG))
            e_gam = jnp.exp(gam_c[i])
            v = v_ref[r0:r0 + CHUNK, vcol:vcol + HEAD_DIM].astype(F32)
            rhs_all[p] = jnp.concatenate([(v * beta[i]).astype(BF16), (kf * (beta[i] * e_gam)).astype(BF16)], axis=1)
            k_end = kf * jnp.exp(g_end - gam_c[i])
            xt_all[p] = jnp.concatenate([k_end, kq * lmat_t], axis=1).T.astype(BF16)
            qg_all[p] = qf * e_gam
            dec_all[p] = jnp.broadcast_to(jnp.exp(g_end), dec_ab_ref.shape[1:])

    zero_state = jnp.zeros((HEAD_DIM, HEAD_DIM), BF16)
    states = [s_ref[j] for j in range(heads * rep)]
    for c in range(n_chunks):
        r0 = c * CHUNK
        for h in range(heads):
            diag = jnp.concatenate(
                [jnp.concatenate([states[h * rep + i].astype(BF16) if j == i else zero_state for j in range(rep)],
                                 axis=1) for i in range(rep)], axis=0)
            m_all = _dot(lhs_ref[h * n_chunks + c], diag)
            for i in range(rep):
                p = prob(h, c, i)
                vcol = (h * rep + i) * HEAD_DIM
                m = m_all[:, i * HEAD_DIM:(i + 1) * HEAD_DIM]
                states[h * rep + i] = states[h * rep + i] * dec_bc_ref[p][0:1, 0:1] + ktu_ref[p] - m[:HEAD_DIM]
                o = m[HEAD_DIM:] + qku_ref[p]
                z = z_ref[r0:r0 + CHUNK, vcol:vcol + HEAD_DIM].astype(F32)
                o_ref[r0:r0 + CHUNK, vcol:vcol + HEAD_DIM] = (_rms(o, nw_ref[...]) * z).astype(BF16)
    for j in range(heads * rep):
        s_ref[j] = states[j]

    for h, c in pairs:
        lhs = []
        for i in range(rep):
            p = prob(h, c, i)
            r = _dot(xt_ref[p], uw_ref[p])
            ktu_ref[p] = r[:HEAD_DIM, :HEAD_DIM]
            qku_ref[p] = r[HEAD_DIM:, :HEAD_DIM]
            dec_bc_ref[p] = dec_ab_ref[p]
            lhs.append(jnp.concatenate([r[:HEAD_DIM, HEAD_DIM:], qg_ref[p] - r[HEAD_DIM:, HEAD_DIM:]], axis=0))
        lhs_ref[h * n_chunks + c] = jnp.concatenate(lhs, axis=1).astype(BF16)

    for p in range(len(rhs_all)):
        xt_ref[p] = xt_all[p]
        qg_ref[p] = qg_all[p]
        dec_ab_ref[p] = dec_all[p]
    inverses = _unit_lower_inverse_pairs(a_pairs)
    for n, (h, c) in enumerate(pairs):
        inv_pair = inverses[n].astype(BF16)
        for i in range(rep):
            p = prob(h, c, i)
            pad = jnp.zeros_like(rhs_all[p])
            stacked = jnp.concatenate([rhs_all[p], pad] if i == 0 else [pad, rhs_all[p]], axis=0)
            uw_ref[p] = _dot(inv_pair, stacked).astype(BF16)


def _gdn(q, k, v, gates, gab_t, norm_w, *, batch, seq, ts=128, heads=8):
    n = q.shape[0]
    nt = seq // ts
    groups = GDN_QK_HEADS // heads
    rep = GDN_V_HEADS // GDN_QK_HEADS
    n_chunks = ts // CHUNK
    n_pairs = heads * n_chunks
    n_probs = n_pairs * rep
    lag = 2
    total = batch * groups * nt

    def block(step):
        step = jnp.clip(step, 0, total - 1)
        return (step // (groups * nt)) * nt + step % nt, (step // nt) % groups

    fresh = lambda width: pl.BlockSpec((ts, width * HEAD_DIM), lambda s: block(s))
    lagged = lambda width: pl.BlockSpec((ts, width * HEAD_DIM), lambda s: block(s - lag))
    r = jnp.arange(CHUNK)[:, None]
    c = jnp.arange(CHUNK)[None, :]
    upper = (r <= c).astype(BF16)
    upper_dup = jnp.concatenate([upper, upper], axis=1)
    triu3 = jnp.concatenate([upper_dup] * 3, axis=0)
    tril3 = jnp.concatenate([(r >= c).astype(BF16)] * 3, axis=1)
    return pl.pallas_call(
        functools.partial(_gdn_body, blocks_per_seq=nt, lag=lag),
        out_shape=jax.ShapeDtypeStruct((n, GDN_V_HEADS * HEAD_DIM), BF16),
        grid=(total + lag,),
        in_specs=[fresh(heads), fresh(heads), fresh(heads * rep), lagged(heads * rep),
                  pl.BlockSpec((SUBLANES * heads, ts), lambda s: block(s)[::-1]),
                  _resident((1, HEAD_DIM)), _resident(triu3.shape), _resident(tril3.shape)],
        out_specs=lagged(heads * rep),
        scratch_shapes=[pltpu.VMEM((heads * rep, HEAD_DIM, HEAD_DIM), F32),
                        pltpu.VMEM((n_probs, CHUNK, 2 * HEAD_DIM), BF16),
                        pltpu.VMEM((n_probs, HEAD_DIM + CHUNK, CHUNK), BF16),
                        pltpu.VMEM((n_probs, CHUNK, HEAD_DIM), F32),
                        pltpu.VMEM((n_probs, SUBLANES, LANES), F32),
                        pltpu.VMEM((n_pairs, HEAD_DIM + CHUNK, rep * HEAD_DIM), BF16),
                        pltpu.VMEM((n_probs, HEAD_DIM, HEAD_DIM), F32),
                        pltpu.VMEM((n_probs, CHUNK, HEAD_DIM), F32),
                        pltpu.VMEM((n_probs, SUBLANES, LANES), F32)],
        compiler_params=_params("arbitrary"), name="gated_delta",
    )(q, k, v, gates, gab_t, norm_w.reshape(1, HEAD_DIM), triu3, tril3)


def _merge_body(oh_ref, og_ref, gate_ref, h_ref, wbh_ref, wbg_ref, wo_ref, o_ref):
    d = h_ref.shape[1]
    yh = _dot(oh_ref[...], wbh_ref[...])
    yg = _dot(og_ref[...], wbg_ref[...])
    y = gate_ref[:, :d].astype(F32) * yh + gate_ref[:, d:].astype(F32) * yg
    o_ref[...] = h_ref[...] + _dot(y.astype(BF16), wo_ref[...])


def _merge(o_h, o_g, merge_gates, h, wbh, wbg, wo, *, tm=512):
    n, d = h.shape
    row = lambda width: pl.BlockSpec((tm, width), lambda i: (i, 0))
    return pl.pallas_call(
        _merge_body, out_shape=jax.ShapeDtypeStruct((n, d), F32), grid=(n // tm,),
        in_specs=[row(o_h.shape[1]), row(o_g.shape[1]), row(2 * d), row(d),
                  _resident(wbh.shape), _resident(wbg.shape), _resident(wo.shape)],
        out_specs=row(d),
        compiler_params=_params("arbitrary"), name="merge",
    )(o_h, o_g, merge_gates, h, wbh.astype(BF16), wbg.astype(BF16), wo.astype(BF16))


def _mixer(u, h, w_in, lb_logits, hgrn_norm, conv_w, a_log, dt_bias, gdn_norm, wbh, wbg, wo, *, layer, batch, seq):
    n, d = h.shape
    hk = HG_HEADS * HEAD_DIM
    gk = GDN_QK_HEADS * HEAD_DIM
    gv = GDN_V_HEADS * HEAD_DIM
    sizes = (hk, hk, hk, hk, gk, gk, gv, GDN_V_HEADS, GDN_V_HEADS, gv, d, d)
    offs = [0]
    for s in sizes:
        offs.append(offs[-1] + s)
    w_bf = w_in.astype(BF16)
    col = lambda a, b: w_bf[:, offs[a]:offs[b]]
    conv_pad = jnp.zeros((SUBLANES - CONV_K, conv_w.shape[1]), F32)
    cw = jnp.concatenate([conv_w, conv_pad], axis=0)

    rep = GDN_V_HEADS // GDN_QK_HEADS
    pad = SUBLANES - 2 * rep
    regroup = lambda a, b, fill: jnp.concatenate(
        [a.reshape(-1, GDN_QK_HEADS, rep), b.reshape(-1, GDN_QK_HEADS, rep),
         jnp.full((a.shape[0], GDN_QK_HEADS, pad), fill, a.dtype)], axis=2).reshape(a.shape[0], -1)
    w_gab = regroup(col(7, 8), col(8, 9), 0.0)
    w_gab = jnp.pad(w_gab, ((0, 0), (0, LANES - w_gab.shape[1])))
    zeros = jnp.zeros((1, GDN_V_HEADS), F32)
    aux = jnp.concatenate([regroup(a_log.reshape(1, -1).astype(F32), zeros, 0.0),
                           regroup(dt_bias.reshape(1, -1).astype(F32), zeros, 0.0)], axis=0)
    aux = jnp.pad(aux, ((0, 0), (0, LANES - aux.shape[1])))

    w_all = jnp.concatenate([col(0, 1), col(1, 2), col(2, 3), col(9, 10), col(3, 4), col(10, 12), col(4, 6),
                             col(6, 7), w_gab], axis=1)
    widths = (hk, hk, hk, gv + hk, 2 * d, 2 * gk, gv)
    q_h, lf, k_h, v_h, act_gates, merge_gates, q_g, k_g, v_g, gab_t = _proj(
        u, w_all, lb_logits.astype(F32), cw, aux, widths, layer=layer, seq=seq)

    o_h = _hgrn(q_h, lf, k_h, v_h, act_gates, gv // HEAD_DIM, hgrn_norm, batch=batch, seq=seq)
    o_g = _gdn(q_g, k_g, v_g, act_gates, gab_t, gdn_norm, batch=batch, seq=seq)
    return _merge(o_h, o_g, merge_gates, h, wbh, wbg, wo)


def kernel(x, ffn1_norm, ffn1_w_in, ffn1_w_out, mix_norm, w_in, hgrn_lb_logits, hgrn_out_norm, gdn_conv_w,
           gdn_a_log, gdn_dt_bias, gdn_out_norm, w_branch_hgrn, w_branch_gdn, w_out, ffn2_norm, ffn2_w_in,
           ffn2_w_out, final_norm):
    batch, seq, d = x.shape
    depth = ffn1_norm.shape[0]
    h = x.reshape(batch * seq, d)
    for l in range(depth):
        h, u = _ffn(h, ffn1_norm[l], ffn1_w_in[l], ffn1_w_out[l], mix_norm[l], final=False)
        h = _mixer(u, h, w_in[l], hgrn_lb_logits, hgrn_out_norm[l], gdn_conv_w[l], gdn_a_log[l],
                   gdn_dt_bias[l], gdn_out_norm[l], w_branch_hgrn[l], w_branch_gdn[l], w_out[l],
                   layer=l, batch=batch, seq=seq)
        last = l == depth - 1
        nxt = final_norm if last else ffn1_norm[l + 1]
        out = _ffn(h, ffn2_norm[l], ffn2_w_in[l], ffn2_w_out[l], nxt, final=last)
        h = out if last else out[0]
    return h.reshape(batch, seq, d)
```

```python
import functools

import jax
import jax.numpy as jnp
from jax import lax
from jax.experimental import pallas as pl
from jax.experimental.pallas import tpu as pltpu

F32 = jnp.float32
BF16 = jnp.bfloat16

EPS = 1e-6
CHUNK = 64
SUB = 8
HEAD_DIM = 128
HG_HEADS = 8
GDN_QK_HEADS = 8
GDN_V_HEADS = 16
CONV_K = 4
LANES = 128
SUBLANES = 8
NEG_BIG = -1e30
LOG2E = 1.4426950408889634
VMEM_LIMIT = 56 * 1024 * 1024


def _dot(a, b):
    return jnp.dot(a, b, preferred_element_type=F32)


def _dot_nt(a, b):
    return lax.dot_general(a, b, (((1,), (1,)), ((), ())), preferred_element_type=F32)


def _dot_tn(a, b):
    return lax.dot_general(a, b, (((0,), (0,)), ((), ())), preferred_element_type=F32)


def _sigmoid(x):
    return 1.0 / (1.0 + jnp.exp2(x * -LOG2E))


def _rms(x, w):
    ms = jnp.mean(x * x, axis=-1, keepdims=True)
    return x * lax.rsqrt(ms + EPS) * w


def _params(*sem):
    return pltpu.CompilerParams(dimension_semantics=sem, vmem_limit_bytes=VMEM_LIMIT)


def _resident(shape):
    return pl.BlockSpec(shape, lambda *_: (0,) * len(shape), pipeline_mode=pl.Buffered(1))


def _ffn_body(h_ref, nw_ref, wa_ref, wb_ref, wo_ref, onw_ref, *out_refs, ff_tile, final):
    h = h_ref[...]
    xn = _rms(h, nw_ref[...]).astype(BF16)
    d_ff = wa_ref.shape[1]
    acc = jnp.zeros(h.shape, F32)
    for j in range(0, d_ff, ff_tile):
        w = min(ff_tile, d_ff - j)
        a = _dot(xn, wa_ref[:, j:j + w])
        b = _dot(xn, wb_ref[:, j:j + w])
        g = (a * _sigmoid(a) * b).astype(BF16)
        acc = acc + _dot(g, wo_ref[j:j + w, :])
    hn = h + 0.5 * acc
    if final:
        out_refs[0][...] = _rms(hn, onw_ref[...])
    else:
        out_refs[0][...] = hn
        out_refs[1][...] = _rms(hn, onw_ref[...]).astype(BF16)


def _ffn(h, norm_w, w_in, w_out, next_norm_w, *, final, tm=512, ff_tile=512):
    n, d = h.shape
    d_ff = w_out.shape[0]
    wa = w_in[:, :d_ff].astype(BF16)
    wb = w_in[:, d_ff:].astype(BF16)
    wo = w_out.astype(BF16)
    row = pl.BlockSpec((tm, d), lambda i: (i, 0))
    if final:
        out_shape = jax.ShapeDtypeStruct((n, d), F32)
        out_specs = row
    else:
        out_shape = (jax.ShapeDtypeStruct((n, d), F32), jax.ShapeDtypeStruct((n, d), BF16))
        out_specs = (row, row)
    return pl.pallas_call(
        functools.partial(_ffn_body, ff_tile=ff_tile, final=final),
        out_shape=out_shape,
        grid=(n // tm,),
        in_specs=[row, _resident((1, d)), _resident((d, d_ff)), _resident((d, d_ff)),
                  _resident((d_ff, d)), _resident((1, d))],
        out_specs=out_specs,
        compiler_params=_params("arbitrary"),
        name="ffn_final" if final else "ffn",
    )(h, norm_w.reshape(1, d), wa, wb, wo, next_norm_w.reshape(1, d))


def _proj_body(u_ref, w_ref, logit_ref, cw_ref, gaux_ref,
               qh_ref, lf_ref, kh_ref, vh_ref, act_ref, mg_ref, qg_ref, kg_ref, vg_ref, gab_ref, ext_ref,
               *, layer, tiles_per_seq, tn, widths):
    w_hq, w_hf, w_hi, w_act, w_mg, w_qk, w_gv = widths
    u = u_ref[...]
    tm = u.shape[0]
    proj = lambda c0, width=tn: _dot(u, w_ref[:, c0:c0 + width])
    bases = [0]
    for w in widths:
        bases.append(bases[-1] + w)

    def hq_tile(o):
        p = proj(bases[0] + o)
        qh_ref[:, o:o + tn] = (p * _sigmoid(p) * HEAD_DIM ** -0.5).astype(BF16)

    lg = logit_ref[...]
    e = jnp.exp(lg - jnp.max(lg, axis=0, keepdims=True))
    lb_all = jnp.sum(e[0:layer + 1, :], axis=0, keepdims=True) / jnp.sum(e, axis=0, keepdims=True)

    def hf_tile(o):
        s = _sigmoid(proj(bases[1] + o))
        lb = lb_all[:, o:o + tn]
        lf_ref[:, o:o + tn] = jnp.log2(lb + (1.0 - lb) * s)
        kh_ref[:, o:o + tn] = ((1.0 - lb) * (1.0 - s)).astype(BF16)

    def hi_tile(o):
        vh_ref[:, o:o + tn] = proj(bases[2] + o).astype(BF16)

    def act_tile(o):
        p = proj(bases[3] + o)
        act_ref[:, o:o + tn] = (p * _sigmoid(p)).astype(BF16)

    def mg_tile(o):
        mg_ref[:, o:o + tn] = _sigmoid(proj(bases[4] + o)).astype(BF16)

    @pl.when(pl.program_id(0) % tiles_per_seq == 0)
    def _():
        ext_ref[0:SUBLANES, :] = jnp.zeros((SUBLANES, ext_ref.shape[1]), F32)

    def conv_tile(o):
        p = proj(bases[5] + o)
        ext_ref[SUBLANES:, o:o + tn] = p
        acc = p * cw_ref[CONV_K - 1:CONV_K, o:o + tn]
        for d in range(1, CONV_K):
            acc = acc + ext_ref[SUBLANES - d:SUBLANES - d + tm, o:o + tn] * cw_ref[CONV_K - 1 - d:CONV_K - d, o:o + tn]
        y = acc * _sigmoid(acc)
        if o < w_qk:
            is_q = o < w_qk // 2
            out_ref, o_out = (qg_ref, o) if is_q else (kg_ref, o - w_qk // 2)
            for s in range(0, tn, HEAD_DIM):
                ys = y[:, s:s + HEAD_DIM]
                inv = lax.rsqrt(jnp.sum(ys * ys, axis=-1, keepdims=True) + EPS) * (HEAD_DIM ** -0.5 if is_q else 1.0)
                out_ref[:, o_out + s:o_out + s + HEAD_DIM] = (ys * inv).astype(BF16)
        else:
            vg_ref[:, o - w_qk:o - w_qk + tn] = y.astype(BF16)

    light = [(f, o) for f, w in ((hq_tile, w_hq), (hf_tile, w_hf), (hi_tile, w_hi), (act_tile, w_act), (mg_tile, w_mg))
             for o in range(0, w, tn)]
    heavy = [(conv_tile, o) for o in range(0, w_qk + w_gv, tn)]
    per_heavy = -(-len(light) // len(heavy))
    while light or heavy:
        for f, o in light[:per_heavy]:
            f(o)
        light = light[per_heavy:]
        if heavy:
            f, o = heavy.pop(0)
            f(o)
    ext_ref[0:SUBLANES, :] = ext_ref[tm:tm + SUBLANES, :]

    p = proj(bases[7], LANES)
    x = p + gaux_ref[1:2, :]
    softplus = jnp.maximum(x, 0.0) + jnp.log1p(jnp.exp(-jnp.abs(x)))
    g = -jnp.exp(gaux_ref[0:1, :]) * softplus
    lane = lax.broadcasted_iota(jnp.int32, p.shape, 1)
    gab_ref[...] = jnp.where(lane % SUBLANES < 2, g, _sigmoid(p)).T


def _proj(u, w_all, lb_logits, cw, gaux, widths, *, layer, seq, tm=256, tn=256):
    n, d = u.shape
    w_hq, w_hf, w_hi, w_act, w_mg, w_qk, w_gv = widths
    row = lambda width: pl.BlockSpec((tm, width), lambda i: (i, 0))
    out_widths = (w_hq, w_hf, w_hf, w_hi, w_act, w_mg, w_qk // 2, w_qk // 2, w_gv)
    out_dtypes = (BF16, F32, BF16, BF16, BF16, BF16, BF16, BF16, BF16)
    out_shape = tuple(jax.ShapeDtypeStruct((n, w), dt) for w, dt in zip(out_widths, out_dtypes))
    out_shape += (jax.ShapeDtypeStruct((LANES, n), F32),)
    out_specs = tuple(row(w) for w in out_widths) + (pl.BlockSpec((LANES, tm), lambda i: (0, i)),)
    return pl.pallas_call(
        functools.partial(_proj_body, layer=layer, tiles_per_seq=seq // tm, tn=tn, widths=widths),
        out_shape=out_shape, grid=(n // tm,),
        in_specs=[row(d), _resident(w_all.shape), _resident(lb_logits.shape), _resident(cw.shape),
                  _resident(gaux.shape)],
        out_specs=out_specs,
        scratch_shapes=[pltpu.VMEM((tm + SUBLANES, w_qk + w_gv), F32)],
        compiler_params=_params("arbitrary"), name="proj",
    )(u, w_all, lb_logits, cw, gaux)


def _hgrn_body(q_ref, lf_ref, k_ref, v_ref, gate_ref, nw_ref, tri_ref, o_ref, st_ref, scores_ref, inter_ref, vlag_ref,
               bk_ref, *, blocks_per_seq):
    step = pl.program_id(0)

    @pl.when(step == 0)
    def _():
        for ref in (scores_ref, inter_ref, vlag_ref):
            ref[...] = jnp.zeros_like(ref)

    @pl.when(step % blocks_per_seq == 0)
    def _():
        st_ref[...] = jnp.zeros_like(st_ref)

    ts = q_ref.shape[0]
    n_chunks = ts // CHUNK

    for c in range(n_chunks):
        r0 = c * CHUNK
        o = inter_ref[c] + _dot(scores_ref[c], vlag_ref[r0:r0 + CHUNK, :])
        o_ref[r0:r0 + CHUNK, :] = (_rms(o, nw_ref[...]) * gate_ref[r0:r0 + CHUNK, :].astype(F32)).astype(BF16)

    lf = lf_ref[...]
    hi = lf.astype(BF16)
    r1 = lf - hi.astype(F32)
    mid = r1.astype(BF16)
    lo = (r1 - mid.astype(F32)).astype(BF16)
    by_chunk = lambda x: jnp.concatenate([x[c * CHUNK:(c + 1) * CHUNK] for c in range(n_chunks)], axis=1)
    b_all = _dot(tri_ref[...], jnp.concatenate([by_chunk(hi), by_chunk(mid), by_chunk(lo)], axis=0))

    row = lax.broadcasted_iota(jnp.int32, (CHUNK, HEAD_DIM), 0)
    t_idx = lax.broadcasted_iota(jnp.int32, (CHUNK, CHUNK), 0)
    s_idx = lax.broadcasted_iota(jnp.int32, (CHUNK, CHUNK), 1)
    levels = []
    half = CHUNK // 2
    while half >= SUB:
        size = 2 * half
        levels.append((half, row % size >= half,
                       (t_idx // size == s_idx // size) & (t_idx % size >= half) & (s_idx % size < half)))
        half //= 2
    causal_bias = [jnp.where(row % SUB >= j, 0.0, NEG_BIG) for j in range(SUB)]
    place = [s_idx == (t_idx // SUB) * SUB + j for j in range(SUB)]
    vlag_ref[...] = v_ref[...]
    for c in range(n_chunks):
        r0 = c * CHUNK
        b = b_all[:, c * HEAD_DIM:(c + 1) * HEAD_DIM]
        q = q_ref[r0:r0 + CHUNK, :].astype(F32)
        k = k_ref[r0:r0 + CHUNK, :].astype(F32)
        v = v_ref[r0:r0 + CHUNK, :]
        b_end = b[CHUNK - 1:CHUNK, :]
        st = st_ref[...]
        inter_ref[c] = _dot_nt((q * jnp.exp2(b)).astype(BF16), st.astype(BF16))

        scores = jnp.zeros((CHUNK, CHUNK), F32)
        for half, past_mid, pair in levels:
            size = 2 * half
            b_mid = jnp.concatenate(
                [jnp.broadcast_to(b[m + half - 1:m + half, :], (size, HEAD_DIM)) for m in range(0, CHUNK, size)], axis=0)
            x = (jnp.where(past_mid, q, k) * jnp.exp2(-jnp.abs(b - b_mid))).astype(BF16)
            scores = jnp.where(pair, _dot_nt(x, x), scores)

        bk_ref[0] = b
        bk_ref[1] = k
        for j in range(SUB):
            pick = lambda a: jnp.concatenate(
                [jnp.broadcast_to(bk_ref[a, m * SUB + j:m * SUB + j + 1, :], (SUB, HEAD_DIM))
                 for m in range(CHUNK // SUB)], axis=0)
            dec = jnp.exp2(b - pick(0) + causal_bias[j])
            col = jnp.sum(q * pick(1) * dec, axis=-1, keepdims=True)
            scores = jnp.where(place[j], col, scores)
        scores_ref[c] = scores.astype(BF16)

        k_end = (k * jnp.exp2(b_end - b)).astype(BF16)
        st_ref[...] = st * jnp.exp2(b_end) + _dot_tn(v, k_end)


def _hgrn(q, lf, k, v, gates, gate_col0, norm_w, *, batch, seq, ts=1024):
    n = q.shape[0]
    nt = seq // ts
    n_chunks = ts // CHUNK
    total = batch * HG_HEADS * nt

    def block(step, col0):
        step = jnp.clip(step, 0, total - 1)
        return (step // (HG_HEADS * nt)) * nt + step % nt, (step // nt) % HG_HEADS + col0

    fresh = pl.BlockSpec((ts, HEAD_DIM), lambda s: block(s, 0))
    lagged = lambda col0: pl.BlockSpec((ts, HEAD_DIM), lambda s: block(s - 1, col0))
    tri = (jnp.arange(CHUNK)[:, None] >= jnp.arange(CHUNK)[None, :]).astype(BF16)
    tri3 = jnp.concatenate([tri, tri, tri], axis=1)
    return pl.pallas_call(
        functools.partial(_hgrn_body, blocks_per_seq=nt),
        out_shape=jax.ShapeDtypeStruct((n, HG_HEADS * HEAD_DIM), BF16),
        grid=(total + 1,),
        in_specs=[fresh, fresh, fresh, fresh, lagged(gate_col0), _resident((1, HEAD_DIM)),
                  _resident((CHUNK, 3 * CHUNK))],
        out_specs=lagged(0),
        scratch_shapes=[pltpu.VMEM((HEAD_DIM, HEAD_DIM), F32),
                        pltpu.VMEM((n_chunks, CHUNK, CHUNK), BF16),
                        pltpu.VMEM((n_chunks, CHUNK, HEAD_DIM), F32),
                        pltpu.VMEM((ts, HEAD_DIM), BF16),
                        pltpu.VMEM((2, CHUNK, HEAD_DIM), F32)],
        compiler_params=_params("arbitrary"), name="hgrn2",
    )(q, lf, k, v, gates, norm_w.reshape(1, HEAD_DIM), tri3)


def _split3_bf16(x):
    hi = x.astype(BF16)
    r1 = x - hi.astype(F32)
    mid = r1.astype(BF16)
    lo = (r1 - mid.astype(F32)).astype(BF16)
    return hi, mid, lo


def _unit_lower_inverse_pairs(a_pairs):
    shape = a_pairs[0].shape
    row = lax.broadcasted_iota(jnp.int32, shape, 0)
    lane = lax.broadcasted_iota(jnp.int32, shape, 1)
    first = lane < CHUNK
    eye = jnp.where(row == lane % CHUNK, 1.0, 0.0)

    def level(power, partial):
        top = jnp.concatenate([jnp.where(first, power, 0.0), jnp.where(first, partial, 0.0)], axis=1)
        bottom = jnp.concatenate([jnp.where(first, 0.0, power), jnp.where(first, 0.0, partial)], axis=1)
        res = _dot(power.astype(BF16), jnp.concatenate([top, bottom], axis=0).astype(BF16))
        return res[:, :2 * CHUNK], partial + res[:, 2 * CHUNK:]

    state = [level(-a, eye) for a in a_pairs]
    span = 2
    while span < CHUNK:
        state = [level(power, partial) for power, partial in state]
        span *= 2
    return [partial for _, partial in state]


def _gdn_body(q_ref, k_ref, v_ref, z_ref, gb_ref, nw_ref, triu_ref, tril_ref, o_ref,
              s_ref, uw_ref, xt_ref, qg_ref, dec_ab_ref, lhs_ref, ktu_ref, qku_ref, dec_bc_ref,
              *, blocks_per_seq, lag):
    ts = q_ref.shape[0]
    n_chunks = ts // CHUNK
    heads = q_ref.shape[1] // HEAD_DIM
    rep = s_ref.shape[0] // heads
    dup = 2 * CHUNK
    pairs = [(h, c) for h in range(heads) for c in range(n_chunks)]
    prob = lambda h, c, i: (h * n_chunks + c) * rep + i

    step = pl.program_id(0)

    @pl.when(step == 0)
    def _():
        for ref in (uw_ref, xt_ref, qg_ref, dec_ab_ref, lhs_ref, ktu_ref, qku_ref, dec_bc_ref):
            ref[...] = jnp.zeros_like(ref)

    @pl.when((step == 0) | (step % blocks_per_seq == lag % blocks_per_seq))
    def _():
        s_ref[...] = jnp.zeros_like(s_ref)

    g_stack = jnp.concatenate([gb_ref[SUBLANES * h:SUBLANES * (h + 1), c * CHUNK:(c + 1) * CHUNK]
                               for h, c in pairs], axis=0)
    g_cols = jnp.concatenate([g_stack, jnp.zeros_like(g_stack)], axis=1).T[:CHUNK]
    gam_rows = _dot(jnp.concatenate(_split3_bf16(g_stack), axis=1), triu_ref[...])
    gam_cols = _dot(tril_ref[...], jnp.concatenate(_split3_bf16(g_cols), axis=0))
    t_dup = lax.broadcasted_iota(jnp.int32, (CHUNK, dup), 0)
    s_dup_raw = lax.broadcasted_iota(jnp.int32, (CHUNK, dup), 1)
    s_dup = s_dup_raw % CHUNK
    s_idx = lax.broadcasted_iota(jnp.int32, (CHUNK, CHUNK), 0)
    t_idx = lax.broadcasted_iota(jnp.int32, (CHUNK, CHUNK), 1)
    first = s_dup_raw < CHUNK
    a_pairs, rhs_all, xt_all, qg_all, dec_all = [], {}, {}, {}, {}
    for h, c in pairs:
        r0 = c * CHUNK
        q = q_ref[r0:r0 + CHUNK, h * HEAD_DIM:(h + 1) * HEAD_DIM]
        k = k_ref[r0:r0 + CHUNK, h * HEAD_DIM:(h + 1) * HEAD_DIM]
        qf = q.astype(F32)
        kf = k.astype(F32)
        kkq = _dot_nt(k, jnp.concatenate([k, k, q, q], axis=0))
        kk_dup = kkq[:, :dup]
        kq = kkq[:, dup:dup + CHUNK]
        col0 = SUBLANES * (h * n_chunks + c)
        gam_c = [gam_cols[:, col0 + i:col0 + i + 1] for i in range(rep)]
        beta = [g_cols[:, col0 + rep + i:col0 + rep + i + 1] for i in range(rep)]
        gam_r_pair = jnp.where(first[0:1], gam_rows[col0:col0 + 1, :], gam_rows[col0 + 1:col0 + 2, :])
        lmat_pair = jnp.exp(jnp.where(t_dup >= s_dup, jnp.where(first, gam_c[0], gam_c[1]) - gam_r_pair, NEG_BIG))
        a_pairs.append(jnp.where(t_dup > s_dup, jnp.where(first, beta[0], beta[1]) * kk_dup * lmat_pair, 0.0))
        for i in range(rep):
            p = prob(h, c, i)
            vcol = (h * rep + i) * HEAD_DIM
            gam_r = gam_rows[col0 + i:col0 + i + 1, :CHUNK]
            g_end = gam_c[i][CHUNK - 1:CHUNK, :]
            lmat_t = jnp.exp(jnp.where(t_idx >= s_idx, gam_r - gam_c[i], NEG_BIG))
            e_gam = jnp.exp(gam_c[i])
            v = v_ref[r0:r0 + CHUNK, vcol:vcol + HEAD_DIM].astype(F32)
            rhs_all[p] = jnp.concatenate([(v * beta[i]).astype(BF16), (kf * (beta[i] * e_gam)).astype(BF16)], axis=1)
            k_end = kf * jnp.exp(g_end - gam_c[i])
            xt_all[p] = jnp.concatenate([k_end, kq * lmat_t], axis=1).T.astype(BF16)
            qg_all[p] = qf * e_gam
            dec_all[p] = jnp.broadcast_to(jnp.exp(g_end), dec_ab_ref.shape[1:])

    zero_state = jnp.zeros((HEAD_DIM, HEAD_DIM), BF16)
    states = [s_ref[j] for j in range(heads * rep)]
    for c in range(n_chunks):
        r0 = c * CHUNK
        for h in range(heads):
            diag = jnp.concatenate(
                [jnp.concatenate([states[h * rep + i].astype(BF16) if j == i else zero_state for j in range(rep)],
                                 axis=1) for i in range(rep)], axis=0)
            m_all = _dot(lhs_ref[h * n_chunks + c], diag)
            for i in range(rep):
                p = prob(h, c, i)
                vcol = (h * rep + i) * HEAD_DIM
                m = m_all[:, i * HEAD_DIM:(i + 1) * HEAD_DIM]
                states[h * rep + i] = states[h * rep + i] * dec_bc_ref[p][0:1, 0:1] + ktu_ref[p] - m[:HEAD_DIM]
                o = m[HEAD_DIM:] + qku_ref[p]
                z = z_ref[r0:r0 + CHUNK, vcol:vcol + HEAD_DIM].astype(F32)
                o_ref[r0:r0 + CHUNK, vcol:vcol + HEAD_DIM] = (_rms(o, nw_ref[...]) * z).astype(BF16)
    for j in range(heads * rep):
        s_ref[j] = states[j]

    for h, c in pairs:
        lhs = []
        for i in range(rep):
            p = prob(h, c, i)
            r = _dot(xt_ref[p], uw_ref[p])
            ktu_ref[p] = r[:HEAD_DIM, :HEAD_DIM]
            qku_ref[p] = r[HEAD_DIM:, :HEAD_DIM]
            dec_bc_ref[p] = dec_ab_ref[p]
            lhs.append(jnp.concatenate([r[:HEAD_DIM, HEAD_DIM:], qg_ref[p] - r[HEAD_DIM:, HEAD_DIM:]], axis=0))
        lhs_ref[h * n_chunks + c] = jnp.concatenate(lhs, axis=1).astype(BF16)

    for p in range(len(rhs_all)):
        xt_ref[p] = xt_all[p]
        qg_ref[p] = qg_all[p]
        dec_ab_ref[p] = dec_all[p]
    inverses = _unit_lower_inverse_pairs(a_pairs)
    for n, (h, c) in enumerate(pairs):
        inv_pair = inverses[n].astype(BF16)
        for i in range(rep):
            p = prob(h, c, i)
            pad = jnp.zeros_like(rhs_all[p])
            stacked = jnp.concatenate([rhs_all[p], pad] if i == 0 else [pad, rhs_all[p]], axis=0)
            uw_ref[p] = _dot(inv_pair, stacked).astype(BF16)


def _gdn(q, k, v, gates, gab_t, norm_w, *, batch, seq, ts=128, heads=8):
    n = q.shape[0]
    nt = seq // ts
    groups = GDN_QK_HEADS // heads
    rep = GDN_V_HEADS // GDN_QK_HEADS
    n_chunks = ts // CHUNK
    n_pairs = heads * n_chunks
    n_probs = n_pairs * rep
    lag = 2
    total = batch * groups * nt

    def block(step):
        step = jnp.clip(step, 0, total - 1)
        return (step // (groups * nt)) * nt + step % nt, (step // nt) % groups

    fresh = lambda width: pl.BlockSpec((ts, width * HEAD_DIM), lambda s: block(s))
    lagged = lambda width: pl.BlockSpec((ts, width * HEAD_DIM), lambda s: block(s - lag))
    r = jnp.arange(CHUNK)[:, None]
    c = jnp.arange(CHUNK)[None, :]
    upper = (r <= c).astype(BF16)
    upper_dup = jnp.concatenate([upper, upper], axis=1)
    triu3 = jnp.concatenate([upper_dup] * 3, axis=0)
    tril3 = jnp.concatenate([(r >= c).astype(BF16)] * 3, axis=1)
    return pl.pallas_call(
        functools.partial(_gdn_body, blocks_per_seq=nt, lag=lag),
        out_shape=jax.ShapeDtypeStruct((n, GDN_V_HEADS * HEAD_DIM), BF16),
        grid=(total + lag,),
        in_specs=[fresh(heads), fresh(heads), fresh(heads * rep), lagged(heads * rep),
                  pl.BlockSpec((SUBLANES * heads, ts), lambda s: block(s)[::-1]),
                  _resident((1, HEAD_DIM)), _resident(triu3.shape), _resident(tril3.shape)],
        out_specs=lagged(heads * rep),
        scratch_shapes=[pltpu.VMEM((heads * rep, HEAD_DIM, HEAD_DIM), F32),
                        pltpu.VMEM((n_probs, CHUNK, 2 * HEAD_DIM), BF16),
                        pltpu.VMEM((n_probs, HEAD_DIM + CHUNK, CHUNK), BF16),
                        pltpu.VMEM((n_probs, CHUNK, HEAD_DIM), F32),
                        pltpu.VMEM((n_probs, SUBLANES, LANES), F32),
                        pltpu.VMEM((n_pairs, HEAD_DIM + CHUNK, rep * HEAD_DIM), BF16),
                        pltpu.VMEM((n_probs, HEAD_DIM, HEAD_DIM), F32),
                        pltpu.VMEM((n_probs, CHUNK, HEAD_DIM), F32),
                        pltpu.VMEM((n_probs, SUBLANES, LANES), F32)],
        compiler_params=_params("arbitrary"), name="gated_delta",
    )(q, k, v, gates, gab_t, norm_w.reshape(1, HEAD_DIM), triu3, tril3)


def _merge_body(oh_ref, og_ref, gate_ref, h_ref, wbh_ref, wbg_ref, wo_ref, o_ref):
    d = h_ref.shape[1]
    yh = _dot(oh_ref[...], wbh_ref[...])
    yg = _dot(og_ref[...], wbg_ref[...])
    y = gate_ref[:, :d].astype(F32) * yh + gate_ref[:, d:].astype(F32) * yg
    o_ref[...] = h_ref[...] + _dot(y.astype(BF16), wo_ref[...])


def _merge(o_h, o_g, merge_gates, h, wbh, wbg, wo, *, tm=512):
    n, d = h.shape
    row = lambda width: pl.BlockSpec((tm, width), lambda i: (i, 0))
    return pl.pallas_call(
        _merge_body, out_shape=jax.ShapeDtypeStruct((n, d), F32), grid=(n // tm,),
        in_specs=[row(o_h.shape[1]), row(o_g.shape[1]), row(2 * d), row(d),
                  _resident(wbh.shape), _resident(wbg.shape), _resident(wo.shape)],
        out_specs=row(d),
        compiler_params=_params("arbitrary"), name="merge",
    )(o_h, o_g, merge_gates, h, wbh.astype(BF16), wbg.astype(BF16), wo.astype(BF16))


def _mixer(u, h, w_in, lb_logits, hgrn_norm, conv_w, a_log, dt_bias, gdn_norm, wbh, wbg, wo, *, layer, batch, seq):
    n, d = h.shape
    hk = HG_HEADS * HEAD_DIM
    gk = GDN_QK_HEADS * HEAD_DIM
    gv = GDN_V_HEADS * HEAD_DIM
    sizes = (hk, hk, hk, hk, gk, gk, gv, GDN_V_HEADS, GDN_V_HEADS, gv, d, d)
    offs = [0]
    for s in sizes:
        offs.append(offs[-1] + s)
    w_bf = w_in.astype(BF16)
    col = lambda a, b: w_bf[:, offs[a]:offs[b]]
    conv_pad = jnp.zeros((SUBLANES - CONV_K, conv_w.shape[1]), F32)
    cw = jnp.concatenate([conv_w, conv_pad], axis=0)

    rep = GDN_V_HEADS // GDN_QK_HEADS
    pad = SUBLANES - 2 * rep
    regroup = lambda a, b, fill: jnp.concatenate(
        [a.reshape(-1, GDN_QK_HEADS, rep), b.reshape(-1, GDN_QK_HEADS, rep),
         jnp.full((a.shape[0], GDN_QK_HEADS, pad), fill, a.dtype)], axis=2).reshape(a.shape[0], -1)
    w_gab = regroup(col(7, 8), col(8, 9), 0.0)
    w_gab = jnp.pad(w_gab, ((0, 0), (0, LANES - w_gab.shape[1])))
    zeros = jnp.zeros((1, GDN_V_HEADS), F32)
    aux = jnp.concatenate([regroup(a_log.reshape(1, -1).astype(F32), zeros, 0.0),
                           regroup(dt_bias.reshape(1, -1).astype(F32), zeros, 0.0)], axis=0)
    aux = jnp.pad(aux, ((0, 0), (0, LANES - aux.shape[1])))

    w_all = jnp.concatenate([col(0, 1), col(1, 2), col(2, 3), col(9, 10), col(3, 4), col(10, 12), col(4, 6),
                             col(6, 7), w_gab], axis=1)
    widths = (hk, hk, hk, gv + hk, 2 * d, 2 * gk, gv)
    q_h, lf, k_h, v_h, act_gates, merge_gates, q_g, k_g, v_g, gab_t = _proj(
        u, w_all, lb_logits.astype(F32), cw, aux, widths, layer=layer, seq=seq)

    o_h = _hgrn(q_h, lf, k_h, v_h, act_gates, gv // HEAD_DIM, hgrn_norm, batch=batch, seq=seq)
    o_g = _gdn(q_g, k_g, v_g, act_gates, gab_t, gdn_norm, batch=batch, seq=seq)
    return _merge(o_h, o_g, merge_gates, h, wbh, wbg, wo)


def kernel(x, ffn1_norm, ffn1_w_in, ffn1_w_out, mix_norm, w_in, hgrn_lb_logits, hgrn_out_norm, gdn_conv_w,
           gdn_a_log, gdn_dt_bias, gdn_out_norm, w_branch_hgrn, w_branch_gdn, w_out, ffn2_norm, ffn2_w_in,
           ffn2_w_out, final_norm):
    batch, seq, d = x.shape
    depth = ffn1_norm.shape[0]
    h = x.reshape(batch * seq, d)
    for l in range(depth):
        h, u = _ffn(h, ffn1_norm[l], ffn1_w_in[l], ffn1_w_out[l], mix_norm[l], final=False)
        h = _mixer(u, h, w_in[l], hgrn_lb_logits, hgrn_out_norm[l], gdn_conv_w[l], gdn_a_log[l],
                   gdn_dt_bias[l], gdn_out_norm[l], w_branch_hgrn[l], w_branch_gdn[l], w_out[l],
                   layer=l, batch=batch, seq=seq)
        last = l == depth - 1
        nxt = final_norm if last else ffn1_norm[l + 1]
        out = _ffn(h, ffn2_norm[l], ffn2_w_in[l], ffn2_w_out[l], nxt, final=last)
        h = out if last else out[0]
    return h.reshape(batch, seq, d)
```

```python
import functools

import jax
import jax.numpy as jnp
from jax import lax
from jax.experimental import pallas as pl
from jax.experimental.pallas import tpu as pltpu

F32 = jnp.float32
BF16 = jnp.bfloat16

EPS = 1e-6
CHUNK = 64
SUB = 8
HEAD_DIM = 128
HG_HEADS = 8
GDN_QK_HEADS = 8
GDN_V_HEADS = 16
CONV_K = 4
LANES = 128
SUBLANES = 8
NEG_BIG = -1e30
LOG2E = 1.4426950408889634
VMEM_LIMIT = 56 * 1024 * 1024


def _dot(a, b):
    return jnp.dot(a, b, preferred_element_type=F32)


def _dot_nt(a, b):
    return lax.dot_general(a, b, (((1,), (1,)), ((), ())), preferred_element_type=F32)


def _dot_tn(a, b):
    return lax.dot_general(a, b, (((0,), (0,)), ((), ())), preferred_element_type=F32)


def _sigmoid(x):
    return 1.0 / (1.0 + jnp.exp2(x * -LOG2E))


def _rms(x, w):
    ms = jnp.mean(x * x, axis=-1, keepdims=True)
    return x * lax.rsqrt(ms + EPS) * w


def _params(*sem):
    return pltpu.CompilerParams(dimension_semantics=sem, vmem_limit_bytes=VMEM_LIMIT)


def _resident(shape):
    return pl.BlockSpec(shape, lambda *_: (0,) * len(shape), pipeline_mode=pl.Buffered(1))


def _ffn_body(h_ref, nw_ref, win_ref, wo_ref, onw_ref, *out_refs, ff_tile, final):
    h = h_ref[...]
    xn = _rms(h, nw_ref[...]).astype(BF16)
    d_ff = wo_ref.shape[0]
    acc = jnp.zeros(h.shape, F32)
    for j in range(0, d_ff, ff_tile):
        w = min(ff_tile, d_ff - j)
        a = _dot(xn, win_ref[:, j:j + w].astype(BF16))
        b = _dot(xn, win_ref[:, d_ff + j:d_ff + j + w].astype(BF16))
        g = (a * _sigmoid(a) * b).astype(BF16)
        acc = acc + _dot(g, wo_ref[j:j + w, :].astype(BF16))
    hn = h + 0.5 * acc
    if final:
        out_refs[0][...] = _rms(hn, onw_ref[...])
    else:
        out_refs[0][...] = hn
        out_refs[1][...] = _rms(hn, onw_ref[...]).astype(BF16)


def _ffn(h, norm_w, w_in, w_out, next_norm_w, *, final, tm=512, ff_tile=512):
    n, d = h.shape
    d_ff = w_out.shape[0]
    row = pl.BlockSpec((tm, d), lambda i: (i, 0))
    if final:
        out_shape = jax.ShapeDtypeStruct((n, d), F32)
        out_specs = row
    else:
        out_shape = (jax.ShapeDtypeStruct((n, d), F32), jax.ShapeDtypeStruct((n, d), BF16))
        out_specs = (row, row)
    return pl.pallas_call(
        functools.partial(_ffn_body, ff_tile=ff_tile, final=final),
        out_shape=out_shape,
        grid=(n // tm,),
        in_specs=[row, _resident((1, d)), _resident((d, 2 * d_ff)), _resident((d_ff, d)), _resident((1, d))],
        out_specs=out_specs,
        compiler_params=_params("arbitrary"),
        name="ffn_final" if final else "ffn",
    )(h, norm_w.reshape(1, d), w_in, w_out, next_norm_w.reshape(1, d))


def _proj_body(u_ref, w_ref, logit_ref, cw_ref, gaux_ref,
               qh_ref, lf_ref, kh_ref, vh_ref, act_ref, mg_ref, qg_ref, kg_ref, vg_ref, gab_ref, ext_ref,
               *, layer, tiles_per_seq, tn, widths):
    w_hq, w_hf, w_hi, w_act, w_mg, w_qk, w_gv = widths
    u = u_ref[...]
    tm = u.shape[0]
    proj = lambda c0, width=tn: _dot(u, w_ref[:, c0:c0 + width])
    bases = [0]
    for w in widths:
        bases.append(bases[-1] + w)

    def hq_tile(o):
        p = proj(bases[0] + o)
        qh_ref[:, o:o + tn] = (p * _sigmoid(p) * HEAD_DIM ** -0.5).astype(BF16)

    lg = logit_ref[...]
    e = jnp.exp(lg - jnp.max(lg, axis=0, keepdims=True))
    lb_all = jnp.sum(e[0:layer + 1, :], axis=0, keepdims=True) / jnp.sum(e, axis=0, keepdims=True)

    def hf_tile(o):
        s = _sigmoid(proj(bases[1] + o))
        lb = lb_all[:, o:o + tn]
        lf_ref[:, o:o + tn] = jnp.log2(lb + (1.0 - lb) * s)
        kh_ref[:, o:o + tn] = ((1.0 - lb) * (1.0 - s)).astype(BF16)

    def hi_tile(o):
        vh_ref[:, o:o + tn] = proj(bases[2] + o).astype(BF16)

    def act_tile(o):
        p = proj(bases[3] + o)
        act_ref[:, o:o + tn] = (p * _sigmoid(p)).astype(BF16)

    def mg_tile(o):
        mg_ref[:, o:o + tn] = _sigmoid(proj(bases[4] + o)).astype(BF16)

    @pl.when(pl.program_id(0) % tiles_per_seq == 0)
    def _():
        ext_ref[0:SUBLANES, :] = jnp.zeros((SUBLANES, ext_ref.shape[1]), F32)

    def conv_tile(o):
        p = proj(bases[5] + o)
        ext_ref[SUBLANES:, o:o + tn] = p
        acc = p * cw_ref[CONV_K - 1:CONV_K, o:o + tn]
        for d in range(1, CONV_K):
            acc = acc + ext_ref[SUBLANES - d:SUBLANES - d + tm, o:o + tn] * cw_ref[CONV_K - 1 - d:CONV_K - d, o:o + tn]
        y = acc * _sigmoid(acc)
        if o < w_qk:
            is_q = o < w_qk // 2
            out_ref, o_out = (qg_ref, o) if is_q else (kg_ref, o - w_qk // 2)
            for s in range(0, tn, HEAD_DIM):
                ys = y[:, s:s + HEAD_DIM]
                inv = lax.rsqrt(jnp.sum(ys * ys, axis=-1, keepdims=True) + EPS) * (HEAD_DIM ** -0.5 if is_q else 1.0)
                out_ref[:, o_out + s:o_out + s + HEAD_DIM] = (ys * inv).astype(BF16)
        else:
            vg_ref[:, o - w_qk:o - w_qk + tn] = y.astype(BF16)

    light = [(f, o) for f, w in ((hq_tile, w_hq), (hf_tile, w_hf), (hi_tile, w_hi), (act_tile, w_act), (mg_tile, w_mg))
             for o in range(0, w, tn)]
    heavy = [(conv_tile, o) for o in range(0, w_qk + w_gv, tn)]
    per_heavy = -(-len(light) // len(heavy))
    while light or heavy:
        for f, o in light[:per_heavy]:
            f(o)
        light = light[per_heavy:]
        if heavy:
            f, o = heavy.pop(0)
            f(o)
    ext_ref[0:SUBLANES, :] = ext_ref[tm:tm + SUBLANES, :]

    p = proj(bases[7], LANES)
    x = p + gaux_ref[1:2, :]
    softplus = jnp.maximum(x, 0.0) + jnp.log1p(jnp.exp(-jnp.abs(x)))
    g = -jnp.exp(gaux_ref[0:1, :]) * softplus
    lane = lax.broadcasted_iota(jnp.int32, p.shape, 1)
    gab_ref[...] = jnp.where(lane % SUBLANES < 2, g, _sigmoid(p)).T


def _proj(u, w_all, lb_logits, cw, gaux, widths, *, layer, seq, tm=256, tn=256):
    n, d = u.shape
    w_hq, w_hf, w_hi, w_act, w_mg, w_qk, w_gv = widths
    row = lambda width: pl.BlockSpec((tm, width), lambda i: (i, 0))
    out_widths = (w_hq, w_hf, w_hf, w_hi, w_act, w_mg, w_qk // 2, w_qk // 2, w_gv)
    out_dtypes = (BF16, F32, BF16, BF16, BF16, BF16, BF16, BF16, BF16)
    out_shape = tuple(jax.ShapeDtypeStruct((n, w), dt) for w, dt in zip(out_widths, out_dtypes))
    out_shape += (jax.ShapeDtypeStruct((LANES, n), F32),)
    out_specs = tuple(row(w) for w in out_widths) + (pl.BlockSpec((LANES, tm), lambda i: (0, i)),)
    return pl.pallas_call(
        functools.partial(_proj_body, layer=layer, tiles_per_seq=seq // tm, tn=tn, widths=widths),
        out_shape=out_shape, grid=(n // tm,),
        in_specs=[row(d), _resident(w_all.shape), _resident(lb_logits.shape), _resident(cw.shape),
                  _resident(gaux.shape)],
        out_specs=out_specs,
        scratch_shapes=[pltpu.VMEM((tm + SUBLANES, w_qk + w_gv), F32)],
        compiler_params=_params("arbitrary"), name="proj",
    )(u, w_all, lb_logits, cw, gaux)


def _hgrn_body(q_ref, lf_ref, k_ref, v_ref, gate_ref, nw_ref, tri_ref, o_ref, st_ref, scores_ref, inter_ref, vlag_ref,
               bk_ref, *, blocks_per_seq):
    step = pl.program_id(0)

    @pl.when(step == 0)
    def _():
        for ref in (scores_ref, inter_ref, vlag_ref):
            ref[...] = jnp.zeros_like(ref)

    @pl.when(step % blocks_per_seq == 0)
    def _():
        st_ref[...] = jnp.zeros_like(st_ref)

    ts = q_ref.shape[0]
    n_chunks = ts // CHUNK

    for c in range(n_chunks):
        r0 = c * CHUNK
        o = inter_ref[c] + _dot(scores_ref[c], vlag_ref[r0:r0 + CHUNK, :])
        o_ref[r0:r0 + CHUNK, :] = (_rms(o, nw_ref[...]) * gate_ref[r0:r0 + CHUNK, :].astype(F32)).astype(BF16)

    lf = lf_ref[...]
    hi = lf.astype(BF16)
    r1 = lf - hi.astype(F32)
    mid = r1.astype(BF16)
    lo = (r1 - mid.astype(F32)).astype(BF16)
    by_chunk = lambda x: jnp.concatenate([x[c * CHUNK:(c + 1) * CHUNK] for c in range(n_chunks)], axis=1)
    b_all = _dot(tri_ref[...], jnp.concatenate([by_chunk(hi), by_chunk(mid), by_chunk(lo)], axis=0))

    row = lax.broadcasted_iota(jnp.int32, (CHUNK, HEAD_DIM), 0)
    t_idx = lax.broadcasted_iota(jnp.int32, (CHUNK, CHUNK), 0)
    s_idx = lax.broadcasted_iota(jnp.int32, (CHUNK, CHUNK), 1)
    levels = []
    half = CHUNK // 2
    while half >= SUB:
        size = 2 * half
        levels.append((half, row % size >= half,
                       (t_idx // size == s_idx // size) & (t_idx % size >= half) & (s_idx % size < half)))
        half //= 2
    causal_bias = [jnp.where(row % SUB >= j, 0.0, NEG_BIG) for j in range(SUB)]
    place = [s_idx == (t_idx // SUB) * SUB + j for j in range(SUB)]
    vlag_ref[...] = v_ref[...]
    for c in range(n_chunks):
        r0 = c * CHUNK
        b = b_all[:, c * HEAD_DIM:(c + 1) * HEAD_DIM]
        q = q_ref[r0:r0 + CHUNK, :].astype(F32)
        k = k_ref[r0:r0 + CHUNK, :].astype(F32)
        v = v_ref[r0:r0 + CHUNK, :]
        b_end = b[CHUNK - 1:CHUNK, :]
        st = st_ref[...]
        inter_ref[c] = _dot_nt((q * jnp.exp2(b)).astype(BF16), st.astype(BF16))

        scores = jnp.zeros((CHUNK, CHUNK), F32)
        for half, past_mid, pair in levels:
            size = 2 * half
            b_mid = jnp.concatenate(
                [jnp.broadcast_to(b[m + half - 1:m + half, :], (size, HEAD_DIM)) for m in range(0, CHUNK, size)], axis=0)
            x = (jnp.where(past_mid, q, k) * jnp.exp2(-jnp.abs(b - b_mid))).astype(BF16)
            scores = jnp.where(pair, _dot_nt(x, x), scores)

        bk_ref[0] = b
        bk_ref[1] = k
        for j in range(SUB):
            pick = lambda a: jnp.concatenate(
                [jnp.broadcast_to(bk_ref[a, m * SUB + j:m * SUB + j + 1, :], (SUB, HEAD_DIM))
                 for m in range(CHUNK // SUB)], axis=0)
            dec = jnp.exp2(b - pick(0) + causal_bias[j])
            col = jnp.sum(q * pick(1) * dec, axis=-1, keepdims=True)
            scores = jnp.where(place[j], col, scores)
        scores_ref[c] = scores.astype(BF16)

        k_end = (k * jnp.exp2(b_end - b)).astype(BF16)
        st_ref[...] = st * jnp.exp2(b_end) + _dot_tn(v, k_end)


def _hgrn(q, lf, k, v, gates, gate_col0, norm_w, *, batch, seq, ts=1024):
    n = q.shape[0]
    nt = seq // ts
    n_chunks = ts // CHUNK
    total = batch * HG_HEADS * nt

    def block(step, col0):
        step = jnp.clip(step, 0, total - 1)
        return (step // (HG_HEADS * nt)) * nt + step % nt, (step // nt) % HG_HEADS + col0

    fresh = pl.BlockSpec((ts, HEAD_DIM), lambda s: block(s, 0))
    lagged = lambda col0: pl.BlockSpec((ts, HEAD_DIM), lambda s: block(s - 1, col0))
    tri = (jnp.arange(CHUNK)[:, None] >= jnp.arange(CHUNK)[None, :]).astype(BF16)
    tri3 = jnp.concatenate([tri, tri, tri], axis=1)
    return pl.pallas_call(
        functools.partial(_hgrn_body, blocks_per_seq=nt),
        out_shape=jax.ShapeDtypeStruct((n, HG_HEADS * HEAD_DIM), BF16),
        grid=(total + 1,),
        in_specs=[fresh, fresh, fresh, fresh, lagged(gate_col0), _resident((1, HEAD_DIM)),
                  _resident((CHUNK, 3 * CHUNK))],
        out_specs=lagged(0),
        scratch_shapes=[pltpu.VMEM((HEAD_DIM, HEAD_DIM), F32),
                        pltpu.VMEM((n_chunks, CHUNK, CHUNK), BF16),
                        pltpu.VMEM((n_chunks, CHUNK, HEAD_DIM), F32),
                        pltpu.VMEM((ts, HEAD_DIM), BF16),
                        pltpu.VMEM((2, CHUNK, HEAD_DIM), F32)],
        compiler_params=_params("arbitrary"), name="hgrn2",
    )(q, lf, k, v, gates, norm_w.reshape(1, HEAD_DIM), tri3)


def _split3_bf16(x):
    hi = x.astype(BF16)
    r1 = x - hi.astype(F32)
    mid = r1.astype(BF16)
    lo = (r1 - mid.astype(F32)).astype(BF16)
    return hi, mid, lo


def _unit_lower_inverse_pairs(a_pairs):
    shape = a_pairs[0].shape
    row = lax.broadcasted_iota(jnp.int32, shape, 0)
    lane = lax.broadcasted_iota(jnp.int32, shape, 1)
    first = lane < CHUNK
    eye = jnp.where(row == lane % CHUNK, 1.0, 0.0)

    def level(power, partial):
        top = jnp.concatenate([jnp.where(first, power, 0.0), jnp.where(first, partial, 0.0)], axis=1)
        bottom = jnp.concatenate([jnp.where(first, 0.0, power), jnp.where(first, 0.0, partial)], axis=1)
        res = _dot(power.astype(BF16), jnp.concatenate([top, bottom], axis=0).astype(BF16))
        return res[:, :2 * CHUNK], partial + res[:, 2 * CHUNK:]

    state = [level(-a, eye) for a in a_pairs]
    span = 2
    while span < CHUNK:
        state = [level(power, partial) for power, partial in state]
        span *= 2
    return [partial for _, partial in state]


def _gdn_body(q_ref, k_ref, v_ref, z_ref, gb_ref, nw_ref, triu_ref, tril_ref, o_ref,
              s_ref, uw_ref, xt_ref, qg_ref, dec_ab_ref, lhs_ref, ktu_ref, qku_ref, dec_bc_ref,
              *, blocks_per_seq, lag):
    ts = q_ref.shape[0]
    n_chunks = ts // CHUNK
    heads = q_ref.shape[1] // HEAD_DIM
    rep = s_ref.shape[0] // heads
    dup = 2 * CHUNK
    pairs = [(h, c) for h in range(heads) for c in range(n_chunks)]
    prob = lambda h, c, i: (h * n_chunks + c) * rep + i

    step = pl.program_id(0)

    @pl.when(step == 0)
    def _():
        for ref in (uw_ref, xt_ref, qg_ref, dec_ab_ref, lhs_ref, ktu_ref, qku_ref, dec_bc_ref):
            ref[...] = jnp.zeros_like(ref)

    @pl.when((step == 0) | (step % blocks_per_seq == lag % blocks_per_seq))
    def _():
        s_ref[...] = jnp.zeros_like(s_ref)

    g_stack = jnp.concatenate([gb_ref[SUBLANES * h:SUBLANES * (h + 1), c * CHUNK:(c + 1) * CHUNK]
                               for h, c in pairs], axis=0)
    g_cols = jnp.concatenate([g_stack, jnp.zeros_like(g_stack)], axis=1).T[:CHUNK]
    gam_rows = _dot(jnp.concatenate(_split3_bf16(g_stack), axis=1), triu_ref[...])
    gam_cols = _dot(tril_ref[...], jnp.concatenate(_split3_bf16(g_cols), axis=0))
    t_dup = lax.broadcasted_iota(jnp.int32, (CHUNK, dup), 0)
    s_dup_raw = lax.broadcasted_iota(jnp.int32, (CHUNK, dup), 1)
    s_dup = s_dup_raw % CHUNK
    s_idx = lax.broadcasted_iota(jnp.int32, (CHUNK, CHUNK), 0)
    t_idx = lax.broadcasted_iota(jnp.int32, (CHUNK, CHUNK), 1)
    first = s_dup_raw < CHUNK
    a_pairs, rhs_all, xt_all, qg_all, dec_all = [], {}, {}, {}, {}
    for h, c in pairs:
        r0 = c * CHUNK
        q = q_ref[r0:r0 + CHUNK, h * HEAD_DIM:(h + 1) * HEAD_DIM]
        k = k_ref[r0:r0 + CHUNK, h * HEAD_DIM:(h + 1) * HEAD_DIM]
        qf = q.astype(F32)
        kf = k.astype(F32)
        kkq = _dot_nt(k, jnp.concatenate([k, k, q, q], axis=0))
        kk_dup = kkq[:, :dup]
        kq = kkq[:, dup:dup + CHUNK]
        col0 = SUBLANES * (h * n_chunks + c)
        gam_c = [gam_cols[:, col0 + i:col0 + i + 1] for i in range(rep)]
        beta = [g_cols[:, col0 + rep + i:col0 + rep + i + 1] for i in range(rep)]
        gam_r_pair = jnp.where(first[0:1], gam_rows[col0:col0 + 1, :], gam_rows[col0 + 1:col0 + 2, :])
        lmat_pair = jnp.exp(jnp.where(t_dup >= s_dup, jnp.where(first, gam_c[0], gam_c[1]) - gam_r_pair, NEG_BIG))
        a_pairs.append(jnp.where(t_dup > s_dup, jnp.where(first, beta[0], beta[1]) * kk_dup * lmat_pair, 0.0))
        for i in range(rep):
            p = prob(h, c, i)
            vcol = (h * rep + i) * HEAD_DIM
            gam_r = gam_rows[col0 + i:col0 + i + 1, :CHUNK]
            g_end = gam_c[i][CHUNK - 1:CHUNK, :]
            lmat_t = jnp.exp(jnp.where(t_idx >= s_idx, gam_r - gam_c[i], NEG_BIG))
            e_gam = jnp.exp(gam_c[i])
            v = v_ref[r0:r0 + CHUNK, vcol:vcol + HEAD_DIM].astype(F32)
            rhs_all[p] = jnp.concatenate([(v * beta[i]).astype(BF16), (kf * (beta[i] * e_gam)).astype(BF16)], axis=1)
            k_end = kf * jnp.exp(g_end - gam_c[i])
            xt_all[p] = jnp.concatenate([k_end, kq * lmat_t], axis=1).T.astype(BF16)
            qg_all[p] = qf * e_gam
            dec_all[p] = jnp.broadcast_to(jnp.exp(g_end), dec_ab_ref.shape[1:])

    zero_state = jnp.zeros((HEAD_DIM, HEAD_DIM), BF16)
    states = [s_ref[j] for j in range(heads * rep)]
    for c in range(n_chunks):
        r0 = c * CHUNK
        for h in range(heads):
            diag = jnp.concatenate(
                [jnp.concatenate([states[h * rep + i].astype(BF16) if j == i else zero_state for j in range(rep)],
                                 axis=1) for i in range(rep)], axis=0)
            m_all = _dot(lhs_ref[h * n_chunks + c], diag)
            for i in range(rep):
                p = prob(h, c, i)
                vcol = (h * rep + i) * HEAD_DIM
                m = m_all[:, i * HEAD_DIM:(i + 1) * HEAD_DIM]
                states[h * rep + i] = states[h * rep + i] * dec_bc_ref[p][0:1, 0:1] + ktu_ref[p] - m[:HEAD_DIM]
                o = m[HEAD_DIM:] + qku_ref[p]
                z = z_ref[r0:r0 + CHUNK, vcol:vcol + HEAD_DIM].astype(F32)
                o_ref[r0:r0 + CHUNK, vcol:vcol + HEAD_DIM] = (_rms(o, nw_ref[...]) * z).astype(BF16)
    for j in range(heads * rep):
        s_ref[j] = states[j]

    for h, c in pairs:
        lhs = []
        for i in range(rep):
            p = prob(h, c, i)
            r = _dot(xt_ref[p], uw_ref[p])
            ktu_ref[p] = r[:HEAD_DIM, :HEAD_DIM]
            qku_ref[p] = r[HEAD_DIM:, :HEAD_DIM]
            dec_bc_ref[p] = dec_ab_ref[p]
            lhs.append(jnp.concatenate([r[:HEAD_DIM, HEAD_DIM:], qg_ref[p] - r[HEAD_DIM:, HEAD_DIM:]], axis=0))
        lhs_ref[h * n_chunks + c] = jnp.concatenate(lhs, axis=1).astype(BF16)

    for p in range(len(rhs_all)):
        xt_ref[p] = xt_all[p]
        qg_ref[p] = qg_all[p]
        dec_ab_ref[p] = dec_all[p]
    inverses = _unit_lower_inverse_pairs(a_pairs)
    for n, (h, c) in enumerate(pairs):
        inv_pair = inverses[n].astype(BF16)
        for i in range(rep):
            p = prob(h, c, i)
            pad = jnp.zeros_like(rhs_all[p])
            stacked = jnp.concatenate([rhs_all[p], pad] if i == 0 else [pad, rhs_all[p]], axis=0)
            uw_ref[p] = _dot(inv_pair, stacked).astype(BF16)


def _gdn(q, k, v, gates, gab_t, norm_w, *, batch, seq, ts=128, heads=8):
    n = q.shape[0]
    nt = seq // ts
    groups = GDN_QK_HEADS // heads
    rep = GDN_V_HEADS // GDN_QK_HEADS
    n_chunks = ts // CHUNK
    n_pairs = heads * n_chunks
    n_probs = n_pairs * rep
    lag = 2
    total = batch * groups * nt

    def block(step):
        step = jnp.clip(step, 0, total - 1)
        return (step // (groups * nt)) * nt + step % nt, (step // nt) % groups

    fresh = lambda width: pl.BlockSpec((ts, width * HEAD_DIM), lambda s: block(s))
    lagged = lambda width: pl.BlockSpec((ts, width * HEAD_DIM), lambda s: block(s - lag))
    r = jnp.arange(CHUNK)[:, None]
    c = jnp.arange(CHUNK)[None, :]
    upper = (r <= c).astype(BF16)
    upper_dup = jnp.concatenate([upper, upper], axis=1)
    triu3 = jnp.concatenate([upper_dup] * 3, axis=0)
    tril3 = jnp.concatenate([(r >= c).astype(BF16)] * 3, axis=1)
    return pl.pallas_call(
        functools.partial(_gdn_body, blocks_per_seq=nt, lag=lag),
        out_shape=jax.ShapeDtypeStruct((n, GDN_V_HEADS * HEAD_DIM), BF16),
        grid=(total + lag,),
        in_specs=[fresh(heads), fresh(heads), fresh(heads * rep), lagged(heads * rep),
                  pl.BlockSpec((SUBLANES * heads, ts), lambda s: block(s)[::-1]),
                  _resident((1, HEAD_DIM)), _resident(triu3.shape), _resident(tril3.shape)],
        out_specs=lagged(heads * rep),
        scratch_shapes=[pltpu.VMEM((heads * rep, HEAD_DIM, HEAD_DIM), F32),
                        pltpu.VMEM((n_probs, CHUNK, 2 * HEAD_DIM), BF16),
                        pltpu.VMEM((n_probs, HEAD_DIM + CHUNK, CHUNK), BF16),
                        pltpu.VMEM((n_probs, CHUNK, HEAD_DIM), F32),
                        pltpu.VMEM((n_probs, SUBLANES, LANES), F32),
                        pltpu.VMEM((n_pairs, HEAD_DIM + CHUNK, rep * HEAD_DIM), BF16),
                        pltpu.VMEM((n_probs, HEAD_DIM, HEAD_DIM), F32),
                        pltpu.VMEM((n_probs, CHUNK, HEAD_DIM), F32),
                        pltpu.VMEM((n_probs, SUBLANES, LANES), F32)],
        compiler_params=_params("arbitrary"), name="gated_delta",
    )(q, k, v, gates, gab_t, norm_w.reshape(1, HEAD_DIM), triu3, tril3)


def _merge_body(oh_ref, og_ref, gate_ref, h_ref, wbh_ref, wbg_ref, wo_ref, o_ref):
    d = h_ref.shape[1]
    yh = _dot(oh_ref[...], wbh_ref[...].astype(BF16))
    yg = _dot(og_ref[...], wbg_ref[...].astype(BF16))
    y = gate_ref[:, :d].astype(F32) * yh + gate_ref[:, d:].astype(F32) * yg
    o_ref[...] = h_ref[...] + _dot(y.astype(BF16), wo_ref[...].astype(BF16))


def _merge(o_h, o_g, merge_gates, h, wbh, wbg, wo, *, tm=512):
    n, d = h.shape
    row = lambda width: pl.BlockSpec((tm, width), lambda i: (i, 0))
    return pl.pallas_call(
        _merge_body, out_shape=jax.ShapeDtypeStruct((n, d), F32), grid=(n // tm,),
        in_specs=[row(o_h.shape[1]), row(o_g.shape[1]), row(2 * d), row(d),
                  _resident(wbh.shape), _resident(wbg.shape), _resident(wo.shape)],
        out_specs=row(d),
        compiler_params=_params("arbitrary"), name="merge",
    )(o_h, o_g, merge_gates, h, wbh, wbg, wo)


def _mixer(u, h, w_in, lb_logits, hgrn_norm, conv_w, a_log, dt_bias, gdn_norm, wbh, wbg, wo, *, layer, batch, seq):
    n, d = h.shape
    hk = HG_HEADS * HEAD_DIM
    gk = GDN_QK_HEADS * HEAD_DIM
    gv = GDN_V_HEADS * HEAD_DIM
    sizes = (hk, hk, hk, hk, gk, gk, gv, GDN_V_HEADS, GDN_V_HEADS, gv, d, d)
    offs = [0]
    for s in sizes:
        offs.append(offs[-1] + s)
    w_bf = w_in.astype(BF16)
    col = lambda a, b: w_bf[:, offs[a]:offs[b]]
    conv_pad = jnp.zeros((SUBLANES - CONV_K, conv_w.shape[1]), F32)
    cw = jnp.concatenate([conv_w, conv_pad], axis=0)

    rep = GDN_V_HEADS // GDN_QK_HEADS
    pad = SUBLANES - 2 * rep
    regroup = lambda a, b, fill: jnp.concatenate(
        [a.reshape(-1, GDN_QK_HEADS, rep), b.reshape(-1, GDN_QK_HEADS, rep),
         jnp.full((a.shape[0], GDN_QK_HEADS, pad), fill, a.dtype)], axis=2).reshape(a.shape[0], -1)
    w_gab = regroup(col(7, 8), col(8, 9), 0.0)
    w_gab = jnp.pad(w_gab, ((0, 0), (0, LANES - w_gab.shape[1])))
    zeros = jnp.zeros((1, GDN_V_HEADS), F32)
    aux = jnp.concatenate([regroup(a_log.reshape(1, -1).astype(F32), zeros, 0.0),
                           regroup(dt_bias.reshape(1, -1).astype(F32), zeros, 0.0)], axis=0)
    aux = jnp.pad(aux, ((0, 0), (0, LANES - aux.shape[1])))

    w_all = jnp.concatenate([col(0, 1), col(1, 2), col(2, 3), col(9, 10), col(3, 4), col(10, 12), col(4, 6),
                             col(6, 7), w_gab], axis=1)
    widths = (hk, hk, hk, gv + hk, 2 * d, 2 * gk, gv)
    q_h, lf, k_h, v_h, act_gates, merge_gates, q_g, k_g, v_g, gab_t = _proj(
        u, w_all, lb_logits.astype(F32), cw, aux, widths, layer=layer, seq=seq)

    o_h = _hgrn(q_h, lf, k_h, v_h, act_gates, gv // HEAD_DIM, hgrn_norm, batch=batch, seq=seq)
    o_g = _gdn(q_g, k_g, v_g, act_gates, gab_t, gdn_norm, batch=batch, seq=seq)
    return _merge(o_h, o_g, merge_gates, h, wbh, wbg, wo)


def kernel(x, ffn1_norm, ffn1_w_in, ffn1_w_out, mix_norm, w_in, hgrn_lb_logits, hgrn_out_norm, gdn_conv_w,
           gdn_a_log, gdn_dt_bias, gdn_out_norm, w_branch_hgrn, w_branch_gdn, w_out, ffn2_norm, ffn2_w_in,
           ffn2_w_out, final_norm):
    batch, seq, d = x.shape
    depth = ffn1_norm.shape[0]
    h = x.reshape(batch * seq, d)
    for l in range(depth):
        h, u = _ffn(h, ffn1_norm[l], ffn1_w_in[l], ffn1_w_out[l], mix_norm[l], final=False)
        h = _mixer(u, h, w_in[l], hgrn_lb_logits, hgrn_out_norm[l], gdn_conv_w[l], gdn_a_log[l],
                   gdn_dt_bias[l], gdn_out_norm[l], w_branch_hgrn[l], w_branch_gdn[l], w_out[l],
                   layer=l, batch=batch, seq=seq)
        last = l == depth - 1
        nxt = final_norm if last else ffn1_norm[l + 1]
        out = _ffn(h, ffn2_norm[l], ffn2_w_in[l], ffn2_w_out[l], nxt, final=last)
        h = out if last else out[0]
    return h.reshape(batch, seq, d)
```

```python
import functools

import jax
import jax.numpy as jnp
from jax import lax
from jax.experimental import pallas as pl
from jax.experimental.pallas import tpu as pltpu

F32 = jnp.float32
BF16 = jnp.bfloat16

EPS = 1e-6
CHUNK = 64
SUB = 8
HEAD_DIM = 128
HG_HEADS = 8
GDN_QK_HEADS = 8
GDN_V_HEADS = 16
CONV_K = 4
LANES = 128
SUBLANES = 8
NEG_BIG = -1e30
LOG2E = 1.4426950408889634
VMEM_LIMIT = 56 * 1024 * 1024


def _dot(a, b):
    return jnp.dot(a, b, preferred_element_type=F32)


def _dot_nt(a, b):
    return lax.dot_general(a, b, (((1,), (1,)), ((), ())), preferred_element_type=F32)


def _dot_tn(a, b):
    return lax.dot_general(a, b, (((0,), (0,)), ((), ())), preferred_element_type=F32)


def _sigmoid(x):
    return 1.0 / (1.0 + jnp.exp2(x * -LOG2E))


def _rms(x, w):
    ms = jnp.mean(x * x, axis=-1, keepdims=True)
    return x * lax.rsqrt(ms + EPS) * w


def _params(*sem):
    return pltpu.CompilerParams(dimension_semantics=sem, vmem_limit_bytes=VMEM_LIMIT)


def _resident(shape):
    return pl.BlockSpec(shape, lambda *_: (0,) * len(shape), pipeline_mode=pl.Buffered(1))


def _ffn_body(h_ref, nw_ref, win_ref, wo_ref, onw_ref, *out_refs, ff_tile, final):
    h = h_ref[...]
    xn = _rms(h, nw_ref[...]).astype(BF16)
    d_ff = wo_ref.shape[0]
    acc = jnp.zeros(h.shape, F32)
    for j in range(0, d_ff, ff_tile):
        w = min(ff_tile, d_ff - j)
        a = _dot(xn, win_ref[:, j:j + w].astype(BF16))
        b = _dot(xn, win_ref[:, d_ff + j:d_ff + j + w].astype(BF16))
        g = (a * _sigmoid(a) * b).astype(BF16)
        acc = acc + _dot(g, wo_ref[j:j + w, :].astype(BF16))
    hn = h + 0.5 * acc
    if final:
        out_refs[0][...] = _rms(hn, onw_ref[...])
    else:
        out_refs[0][...] = hn
        out_refs[1][...] = _rms(hn, onw_ref[...]).astype(BF16)


def _ffn(h, norm_w, w_in, w_out, next_norm_w, *, final, tm=512, ff_tile=512):
    n, d = h.shape
    d_ff = w_out.shape[0]
    row = pl.BlockSpec((tm, d), lambda i: (i, 0))
    if final:
        out_shape = jax.ShapeDtypeStruct((n, d), F32)
        out_specs = row
    else:
        out_shape = (jax.ShapeDtypeStruct((n, d), F32), jax.ShapeDtypeStruct((n, d), BF16))
        out_specs = (row, row)
    return pl.pallas_call(
        functools.partial(_ffn_body, ff_tile=ff_tile, final=final),
        out_shape=out_shape,
        grid=(n // tm,),
        in_specs=[row, _resident((1, d)), _resident((d, 2 * d_ff)), _resident((d_ff, d)), _resident((1, d))],
        out_specs=out_specs,
        compiler_params=_params("arbitrary"),
        name="ffn_final" if final else "ffn",
    )(h, norm_w.reshape(1, d), w_in, w_out, next_norm_w.reshape(1, d))


def _proj_body(u_ref, wm_ref, wt_ref, logit_ref, cw_ref, gaux_ref,
               qh_ref, lf_ref, kh_ref, vh_ref, act_ref, mg_ref, qg_ref, kg_ref, vg_ref, gab_ref, ext_ref,
               *, layer, tiles_per_seq, tn, hk, gk, gv):
    u = u_ref[...]
    tm = u.shape[0]
    d = mg_ref.shape[1] // 2
    proj = lambda w_ref, c0, width=tn: _dot(u, w_ref[:, c0:c0 + width])

    def hq_tile(o):
        p = proj(wm_ref, o)
        qh_ref[:, o:o + tn] = (p * _sigmoid(p) * HEAD_DIM ** -0.5).astype(BF16)

    lg = logit_ref[...]
    e = jnp.exp(lg - jnp.max(lg, axis=0, keepdims=True))
    lb_all = jnp.sum(e[0:layer + 1, :], axis=0, keepdims=True) / jnp.sum(e, axis=0, keepdims=True)

    def hf_tile(o):
        s = _sigmoid(proj(wm_ref, hk + o))
        lb = lb_all[:, o:o + tn]
        lf_ref[:, o:o + tn] = jnp.log2(lb + (1.0 - lb) * s)
        kh_ref[:, o:o + tn] = ((1.0 - lb) * (1.0 - s)).astype(BF16)

    def hi_tile(o):
        vh_ref[:, o:o + tn] = proj(wm_ref, 2 * hk + o).astype(BF16)

    def act_tile(o):
        p = proj(wt_ref, o) if o < gv else proj(wm_ref, 3 * hk + o - gv)
        act_ref[:, o:o + tn] = (p * _sigmoid(p)).astype(BF16)

    def mg_tile(o):
        mg_ref[:, o:o + tn] = _sigmoid(proj(wt_ref, gv + o)).astype(BF16)

    @pl.when(pl.program_id(0) % tiles_per_seq == 0)
    def _():
        ext_ref[0:SUBLANES, :] = jnp.zeros((SUBLANES, ext_ref.shape[1]), F32)

    def conv_tile(o):
        p = proj(wm_ref, 4 * hk + o)
        ext_ref[SUBLANES:, o:o + tn] = p
        acc = p * cw_ref[CONV_K - 1:CONV_K, o:o + tn]
        for dd in range(1, CONV_K):
            acc = acc + (ext_ref[SUBLANES - dd:SUBLANES - dd + tm, o:o + tn]
                         * cw_ref[CONV_K - 1 - dd:CONV_K - dd, o:o + tn])
        y = acc * _sigmoid(acc)
        if o < 2 * gk:
            is_q = o < gk
            out_ref, o_out = (qg_ref, o) if is_q else (kg_ref, o - gk)
            for s in range(0, tn, HEAD_DIM):
                ys = y[:, s:s + HEAD_DIM]
                inv = lax.rsqrt(jnp.sum(ys * ys, axis=-1, keepdims=True) + EPS) * (HEAD_DIM ** -0.5 if is_q else 1.0)
                out_ref[:, o_out + s:o_out + s + HEAD_DIM] = (ys * inv).astype(BF16)
        else:
            vg_ref[:, o - 2 * gk:o - 2 * gk + tn] = y.astype(BF16)

    light = [(f, o) for f, w in ((hq_tile, hk), (hf_tile, hk), (hi_tile, hk), (act_tile, gv + hk), (mg_tile, 2 * d))
             for o in range(0, w, tn)]
    heavy = [(conv_tile, o) for o in range(0, 2 * gk + gv, tn)]
    per_heavy = -(-len(light) // len(heavy))
    while light or heavy:
        for f, o in light[:per_heavy]:
            f(o)
        light = light[per_heavy:]
        if heavy:
            f, o = heavy.pop(0)
            f(o)
    ext_ref[0:SUBLANES, :] = ext_ref[tm:tm + SUBLANES, :]

    p = proj(wt_ref, gv + 2 * d, LANES)
    x = p + gaux_ref[1:2, :]
    softplus = jnp.maximum(x, 0.0) + jnp.log1p(jnp.exp(-jnp.abs(x)))
    g = -jnp.exp(gaux_ref[0:1, :]) * softplus
    lane = lax.broadcasted_iota(jnp.int32, p.shape, 1)
    gab_ref[...] = jnp.where(lane % SUBLANES < 2, g, _sigmoid(p)).T


def _proj(u, w_main, w_tail, lb_logits, cw, gaux, *, layer, seq, hk, gk, gv, tm=256, tn=256):
    n, d = u.shape
    row = lambda width: pl.BlockSpec((tm, width), lambda i: (i, 0))
    out_widths = (hk, hk, hk, hk, gv + hk, 2 * d, gk, gk, gv)
    out_dtypes = (BF16, F32, BF16, BF16, BF16, BF16, BF16, BF16, BF16)
    out_shape = tuple(jax.ShapeDtypeStruct((n, w), dt) for w, dt in zip(out_widths, out_dtypes))
    out_shape += (jax.ShapeDtypeStruct((LANES, n), F32),)
    out_specs = tuple(row(w) for w in out_widths) + (pl.BlockSpec((LANES, tm), lambda i: (0, i)),)
    return pl.pallas_call(
        functools.partial(_proj_body, layer=layer, tiles_per_seq=seq // tm, tn=tn, hk=hk, gk=gk, gv=gv),
        out_shape=out_shape, grid=(n // tm,),
        in_specs=[row(d), _resident(w_main.shape), _resident(w_tail.shape), _resident(lb_logits.shape),
                  _resident(cw.shape), _resident(gaux.shape)],
        out_specs=out_specs,
        scratch_shapes=[pltpu.VMEM((tm + SUBLANES, 2 * gk + gv), F32)],
        compiler_params=_params("arbitrary"), name="proj",
    )(u, w_main, w_tail, lb_logits, cw, gaux)


def _hgrn_body(q_ref, lf_ref, k_ref, v_ref, gate_ref, nw_ref, tri_ref, o_ref, st_ref, scores_ref, inter_ref, vlag_ref,
               bk_ref, *, blocks_per_seq):
    step = pl.program_id(0)

    @pl.when(step == 0)
    def _():
        for ref in (scores_ref, inter_ref, vlag_ref):
            ref[...] = jnp.zeros_like(ref)

    @pl.when(step % blocks_per_seq == 0)
    def _():
        st_ref[...] = jnp.zeros_like(st_ref)

    ts = q_ref.shape[0]
    n_chunks = ts // CHUNK

    for c in range(n_chunks):
        r0 = c * CHUNK
        o = inter_ref[c] + _dot(scores_ref[c], vlag_ref[r0:r0 + CHUNK, :])
        o_ref[r0:r0 + CHUNK, :] = (_rms(o, nw_ref[...]) * gate_ref[r0:r0 + CHUNK, :].astype(F32)).astype(BF16)

    lf = lf_ref[...]
    hi = lf.astype(BF16)
    r1 = lf - hi.astype(F32)
    mid = r1.astype(BF16)
    lo = (r1 - mid.astype(F32)).astype(BF16)
    by_chunk = lambda x: jnp.concatenate([x[c * CHUNK:(c + 1) * CHUNK] for c in range(n_chunks)], axis=1)
    b_all = _dot(tri_ref[...], jnp.concatenate([by_chunk(hi), by_chunk(mid), by_chunk(lo)], axis=0))

    row = lax.broadcasted_iota(jnp.int32, (CHUNK, HEAD_DIM), 0)
    t_idx = lax.broadcasted_iota(jnp.int32, (CHUNK, CHUNK), 0)
    s_idx = lax.broadcasted_iota(jnp.int32, (CHUNK, CHUNK), 1)
    levels = []
    half = CHUNK // 2
    while half >= SUB:
        size = 2 * half
        levels.append((half, row % size >= half,
                       (t_idx // size == s_idx // size) & (t_idx % size >= half) & (s_idx % size < half)))
        half //= 2
    causal_bias = [jnp.where(row % SUB >= j, 0.0, NEG_BIG) for j in range(SUB)]
    place = [s_idx == (t_idx // SUB) * SUB + j for j in range(SUB)]
    vlag_ref[...] = v_ref[...]
    for c in range(n_chunks):
        r0 = c * CHUNK
        b = b_all[:, c * HEAD_DIM:(c + 1) * HEAD_DIM]
        q = q_ref[r0:r0 + CHUNK, :].astype(F32)
        k = k_ref[r0:r0 + CHUNK, :].astype(F32)
        v = v_ref[r0:r0 + CHUNK, :]
        b_end = b[CHUNK - 1:CHUNK, :]
        st = st_ref[...]
        inter_ref[c] = _dot_nt((q * jnp.exp2(b)).astype(BF16), st.astype(BF16))

        scores = jnp.zeros((CHUNK, CHUNK), F32)
        for half, past_mid, pair in levels:
            size = 2 * half
            b_mid = jnp.concatenate(
                [jnp.broadcast_to(b[m + half - 1:m + half, :], (size, HEAD_DIM)) for m in range(0, CHUNK, size)], axis=0)
            x = (jnp.where(past_mid, q, k) * jnp.exp2(-jnp.abs(b - b_mid))).astype(BF16)
            scores = jnp.where(pair, _dot_nt(x, x), scores)

        bk_ref[0] = b
        bk_ref[1] = k
        for j in range(SUB):
            pick = lambda a: jnp.concatenate(
                [jnp.broadcast_to(bk_ref[a, m * SUB + j:m * SUB + j + 1, :], (SUB, HEAD_DIM))
                 for m in range(CHUNK // SUB)], axis=0)
            dec = jnp.exp2(b - pick(0) + causal_bias[j])
            col = jnp.sum(q * pick(1) * dec, axis=-1, keepdims=True)
            scores = jnp.where(place[j], col, scores)
        scores_ref[c] = scores.astype(BF16)

        k_end = (k * jnp.exp2(b_end - b)).astype(BF16)
        st_ref[...] = st * jnp.exp2(b_end) + _dot_tn(v, k_end)


def _hgrn(q, lf, k, v, gates, gate_col0, norm_w, *, batch, seq, ts=1024):
    n = q.shape[0]
    nt = seq // ts
    n_chunks = ts // CHUNK
    total = batch * HG_HEADS * nt

    def block(step, col0):
        step = jnp.clip(step, 0, total - 1)
        return (step // (HG_HEADS * nt)) * nt + step % nt, (step // nt) % HG_HEADS + col0

    fresh = pl.BlockSpec((ts, HEAD_DIM), lambda s: block(s, 0))
    lagged = lambda col0: pl.BlockSpec((ts, HEAD_DIM), lambda s: block(s - 1, col0))
    tri = (jnp.arange(CHUNK)[:, None] >= jnp.arange(CHUNK)[None, :]).astype(BF16)
    tri3 = jnp.concatenate([tri, tri, tri], axis=1)
    return pl.pallas_call(
        functools.partial(_hgrn_body, blocks_per_seq=nt),
        out_shape=jax.ShapeDtypeStruct((n, HG_HEADS * HEAD_DIM), BF16),
        grid=(total + 1,),
        in_specs=[fresh, fresh, fresh, fresh, lagged(gate_col0), _resident((1, HEAD_DIM)),
                  _resident((CHUNK, 3 * CHUNK))],
        out_specs=lagged(0),
        scratch_shapes=[pltpu.VMEM((HEAD_DIM, HEAD_DIM), F32),
                        pltpu.VMEM((n_chunks, CHUNK, CHUNK), BF16),
                        pltpu.VMEM((n_chunks, CHUNK, HEAD_DIM), F32),
                        pltpu.VMEM((ts, HEAD_DIM), BF16),
                        pltpu.VMEM((2, CHUNK, HEAD_DIM), F32)],
        compiler_params=_params("arbitrary"), name="hgrn2",
    )(q, lf, k, v, gates, norm_w.reshape(1, HEAD_DIM), tri3)


def _split3_bf16(x):
    hi = x.astype(BF16)
    r1 = x - hi.astype(F32)
    mid = r1.astype(BF16)
    lo = (r1 - mid.astype(F32)).astype(BF16)
    return hi, mid, lo


def _unit_lower_inverse_pairs(a_pairs):
    shape = a_pairs[0].shape
    row = lax.broadcasted_iota(jnp.int32, shape, 0)
    lane = lax.broadcasted_iota(jnp.int32, shape, 1)
    first = lane < CHUNK
    eye = jnp.where(row == lane % CHUNK, 1.0, 0.0)

    def level(power, partial):
        top = jnp.concatenate([jnp.where(first, power, 0.0), jnp.where(first, partial, 0.0)], axis=1)
        bottom = jnp.concatenate([jnp.where(first, 0.0, power), jnp.where(first, 0.0, partial)], axis=1)
        res = _dot(power.astype(BF16), jnp.concatenate([top, bottom], axis=0).astype(BF16))
        return res[:, :2 * CHUNK], partial + res[:, 2 * CHUNK:]

    state = [level(-a, eye) for a in a_pairs]
    span = 2
    while span < CHUNK:
        state = [level(power, partial) for power, partial in state]
        span *= 2
    return [partial for _, partial in state]


def _gdn_body(q_ref, k_ref, v_ref, z_ref, gb_ref, nw_ref, triu_ref, tril_ref, o_ref,
              s_ref, uw_ref, xt_ref, qg_ref, dec_ab_ref, lhs_ref, ktu_ref, qku_ref, dec_bc_ref,
              *, blocks_per_seq, lag):
    ts = q_ref.shape[0]
    n_chunks = ts // CHUNK
    heads = q_ref.shape[1] // HEAD_DIM
    rep = s_ref.shape[0] // heads
    dup = 2 * CHUNK
    pairs = [(h, c) for h in range(heads) for c in range(n_chunks)]
    prob = lambda h, c, i: (h * n_chunks + c) * rep + i

    step = pl.program_id(0)

    @pl.when(step == 0)
    def _():
        for ref in (uw_ref, xt_ref, qg_ref, dec_ab_ref, lhs_ref, ktu_ref, qku_ref, dec_bc_ref):
            ref[...] = jnp.zeros_like(ref)

    @pl.when((step == 0) | (step % blocks_per_seq == lag % blocks_per_seq))
    def _():
        s_ref[...] = jnp.zeros_like(s_ref)

    g_stack = jnp.concatenate([gb_ref[SUBLANES * h:SUBLANES * (h + 1), c * CHUNK:(c + 1) * CHUNK]
                               for h, c in pairs], axis=0)
    g_cols = jnp.concatenate([g_stack, jnp.zeros_like(g_stack)], axis=1).T[:CHUNK]
    gam_rows = _dot(jnp.concatenate(_split3_bf16(g_stack), axis=1), triu_ref[...])
    gam_cols = _dot(tril_ref[...], jnp.concatenate(_split3_bf16(g_cols), axis=0))
    t_dup = lax.broadcasted_iota(jnp.int32, (CHUNK, dup), 0)
    s_dup_raw = lax.broadcasted_iota(jnp.int32, (CHUNK, dup), 1)
    s_dup = s_dup_raw % CHUNK
    s_idx = lax.broadcasted_iota(jnp.int32, (CHUNK, CHUNK), 0)
    t_idx = lax.broadcasted_iota(jnp.int32, (CHUNK, CHUNK), 1)
    first = s_dup_raw < CHUNK
    a_pairs, rhs_all, xt_all, qg_all, dec_all = [], {}, {}, {}, {}
    for h, c in pairs:
        r0 = c * CHUNK
        q = q_ref[r0:r0 + CHUNK, h * HEAD_DIM:(h + 1) * HEAD_DIM]
        k = k_ref[r0:r0 + CHUNK, h * HEAD_DIM:(h + 1) * HEAD_DIM]
        qf = q.astype(F32)
        kf = k.astype(F32)
        kkq = _dot_nt(k, jnp.concatenate([k, k, q, q], axis=0))
        kk_dup = kkq[:, :dup]
        kq = kkq[:, dup:dup + CHUNK]
        col0 = SUBLANES * (h * n_chunks + c)
        gam_c = [gam_cols[:, col0 + i:col0 + i + 1] for i in range(rep)]
        beta = [g_cols[:, col0 + rep + i:col0 + rep + i + 1] for i in range(rep)]
        gam_r_pair = jnp.where(first[0:1], gam_rows[col0:col0 + 1, :], gam_rows[col0 + 1:col0 + 2, :])
        lmat_pair = jnp.exp(jnp.where(t_dup >= s_dup, jnp.where(first, gam_c[0], gam_c[1]) - gam_r_pair, NEG_BIG))
        a_pairs.append(jnp.where(t_dup > s_dup, jnp.where(first, beta[0], beta[1]) * kk_dup * lmat_pair, 0.0))
        for i in range(rep):
            p = prob(h, c, i)
            vcol = (h * rep + i) * HEAD_DIM
            gam_r = gam_rows[col0 + i:col0 + i + 1, :CHUNK]
            g_end = gam_c[i][CHUNK - 1:CHUNK, :]
            lmat_t = jnp.exp(jnp.where(t_idx >= s_idx, gam_r - gam_c[i], NEG_BIG))
            e_gam = jnp.exp(gam_c[i])
            v = v_ref[r0:r0 + CHUNK, vcol:vcol + HEAD_DIM].astype(F32)
            rhs_all[p] = jnp.concatenate([(v * beta[i]).astype(BF16), (kf * (beta[i] * e_gam)).astype(BF16)], axis=1)
            k_end = kf * jnp.exp(g_end - gam_c[i])
            xt_all[p] = jnp.concatenate([k_end, kq * lmat_t], axis=1).T.astype(BF16)
            qg_all[p] = qf * e_gam
            dec_all[p] = jnp.broadcast_to(jnp.exp(g_end), dec_ab_ref.shape[1:])

    zero_state = jnp.zeros((HEAD_DIM, HEAD_DIM), BF16)
    states = [s_ref[j] for j in range(heads * rep)]
    for c in range(n_chunks):
        r0 = c * CHUNK
        for h in range(heads):
            diag = jnp.concatenate(
                [jnp.concatenate([states[h * rep + i].astype(BF16) if j == i else zero_state for j in range(rep)],
                                 axis=1) for i in range(rep)], axis=0)
            m_all = _dot(lhs_ref[h * n_chunks + c], diag)
            for i in range(rep):
                p = prob(h, c, i)
                vcol = (h * rep + i) * HEAD_DIM
                m = m_all[:, i * HEAD_DIM:(i + 1) * HEAD_DIM]
                states[h * rep + i] = states[h * rep + i] * dec_bc_ref[p][0:1, 0:1] + ktu_ref[p] - m[:HEAD_DIM]
                o = m[HEAD_DIM:] + qku_ref[p]
                z = z_ref[r0:r0 + CHUNK, vcol:vcol + HEAD_DIM].astype(F32)
                o_ref[r0:r0 + CHUNK, vcol:vcol + HEAD_DIM] = (_rms(o, nw_ref[...]) * z).astype(BF16)
    for j in range(heads * rep):
        s_ref[j] = states[j]

    for h, c in pairs:
        lhs = []
        for i in range(rep):
            p = prob(h, c, i)
            r = _dot(xt_ref[p], uw_ref[p])
            ktu_ref[p] = r[:HEAD_DIM, :HEAD_DIM]
            qku_ref[p] = r[HEAD_DIM:, :HEAD_DIM]
            dec_bc_ref[p] = dec_ab_ref[p]
            lhs.append(jnp.concatenate([r[:HEAD_DIM, HEAD_DIM:], qg_ref[p] - r[HEAD_DIM:, HEAD_DIM:]], axis=0))
        lhs_ref[h * n_chunks + c] = jnp.concatenate(lhs, axis=1).astype(BF16)

    for p in range(len(rhs_all)):
        xt_ref[p] = xt_all[p]
        qg_ref[p] = qg_all[p]
        dec_ab_ref[p] = dec_all[p]
    inverses = _unit_lower_inverse_pairs(a_pairs)
    for n, (h, c) in enumerate(pairs):
        inv_pair = inverses[n].astype(BF16)
        for i in range(rep):
            p = prob(h, c, i)
            pad = jnp.zeros_like(rhs_all[p])
            stacked = jnp.concatenate([rhs_all[p], pad] if i == 0 else [pad, rhs_all[p]], axis=0)
            uw_ref[p] = _dot(inv_pair, stacked).astype(BF16)


def _gdn(q, k, v, gates, gab_t, norm_w, *, batch, seq, ts=128, heads=8):
    n = q.shape[0]
    nt = seq // ts
    groups = GDN_QK_HEADS // heads
    rep = GDN_V_HEADS // GDN_QK_HEADS
    n_chunks = ts // CHUNK
    n_pairs = heads * n_chunks
    n_probs = n_pairs * rep
    lag = 2
    total = batch * groups * nt

    def block(step):
        step = jnp.clip(step, 0, total - 1)
        return (step // (groups * nt)) * nt + step % nt, (step // nt) % groups

    fresh = lambda width: pl.BlockSpec((ts, width * HEAD_DIM), lambda s: block(s))
    lagged = lambda width: pl.BlockSpec((ts, width * HEAD_DIM), lambda s: block(s - lag))
    r = jnp.arange(CHUNK)[:, None]
    c = jnp.arange(CHUNK)[None, :]
    upper = (r <= c).astype(BF16)
    upper_dup = jnp.concatenate([upper, upper], axis=1)
    triu3 = jnp.concatenate([upper_dup] * 3, axis=0)
    tril3 = jnp.concatenate([(r >= c).astype(BF16)] * 3, axis=1)
    return pl.pallas_call(
        functools.partial(_gdn_body, blocks_per_seq=nt, lag=lag),
        out_shape=jax.ShapeDtypeStruct((n, GDN_V_HEADS * HEAD_DIM), BF16),
        grid=(total + lag,),
        in_specs=[fresh(heads), fresh(heads), fresh(heads * rep), lagged(heads * rep),
                  pl.BlockSpec((SUBLANES * heads, ts), lambda s: block(s)[::-1]),
                  _resident((1, HEAD_DIM)), _resident(triu3.shape), _resident(tril3.shape)],
        out_specs=lagged(heads * rep),
        scratch_shapes=[pltpu.VMEM((heads * rep, HEAD_DIM, HEAD_DIM), F32),
                        pltpu.VMEM((n_probs, CHUNK, 2 * HEAD_DIM), BF16),
                        pltpu.VMEM((n_probs, HEAD_DIM + CHUNK, CHUNK), BF16),
                        pltpu.VMEM((n_probs, CHUNK, HEAD_DIM), F32),
                        pltpu.VMEM((n_probs, SUBLANES, LANES), F32),
                        pltpu.VMEM((n_pairs, HEAD_DIM + CHUNK, rep * HEAD_DIM), BF16),
                        pltpu.VMEM((n_probs, HEAD_DIM, HEAD_DIM), F32),
                        pltpu.VMEM((n_probs, CHUNK, HEAD_DIM), F32),
                        pltpu.VMEM((n_probs, SUBLANES, LANES), F32)],
        compiler_params=_params("arbitrary"), name="gated_delta",
    )(q, k, v, gates, gab_t, norm_w.reshape(1, HEAD_DIM), triu3, tril3)


def _merge_body(oh_ref, og_ref, gate_ref, h_ref, wbh_ref, wbg_ref, wo_ref, o_ref):
    d = h_ref.shape[1]
    yh = _dot(oh_ref[...], wbh_ref[...].astype(BF16))
    yg = _dot(og_ref[...], wbg_ref[...].astype(BF16))
    y = gate_ref[:, :d].astype(F32) * yh + gate_ref[:, d:].astype(F32) * yg
    o_ref[...] = h_ref[...] + _dot(y.astype(BF16), wo_ref[...].astype(BF16))


def _merge(o_h, o_g, merge_gates, h, wbh, wbg, wo, *, tm=512):
    n, d = h.shape
    row = lambda width: pl.BlockSpec((tm, width), lambda i: (i, 0))
    return pl.pallas_call(
        _merge_body, out_shape=jax.ShapeDtypeStruct((n, d), F32), grid=(n // tm,),
        in_specs=[row(o_h.shape[1]), row(o_g.shape[1]), row(2 * d), row(d),
                  _resident(wbh.shape), _resident(wbg.shape), _resident(wo.shape)],
        out_specs=row(d),
        compiler_params=_params("arbitrary"), name="merge",
    )(o_h, o_g, merge_gates, h, wbh, wbg, wo)


def _mixer(u, h, w_in, lb_logits, hgrn_norm, conv_w, a_log, dt_bias, gdn_norm, wbh, wbg, wo, *, layer, batch, seq):
    n, d = h.shape
    hk = HG_HEADS * HEAD_DIM
    gk = GDN_QK_HEADS * HEAD_DIM
    gv = GDN_V_HEADS * HEAD_DIM
    main = 4 * hk + 2 * gk + gv
    w_main = w_in[:, :main].astype(BF16)
    tail = w_in[:, main:]
    w_ga, w_gb = tail[:, :GDN_V_HEADS], tail[:, GDN_V_HEADS:2 * GDN_V_HEADS]
    conv_pad = jnp.zeros((SUBLANES - CONV_K, conv_w.shape[1]), F32)
    cw = jnp.concatenate([conv_w, conv_pad], axis=0)

    rep = GDN_V_HEADS // GDN_QK_HEADS
    pad = SUBLANES - 2 * rep
    regroup = lambda a, b, fill: jnp.concatenate(
        [a.reshape(-1, GDN_QK_HEADS, rep), b.reshape(-1, GDN_QK_HEADS, rep),
         jnp.full((a.shape[0], GDN_QK_HEADS, pad), fill, a.dtype)], axis=2).reshape(a.shape[0], -1)
    w_gab = regroup(w_ga, w_gb, 0.0)
    w_gab = jnp.pad(w_gab, ((0, 0), (0, LANES - w_gab.shape[1])))
    zeros = jnp.zeros((1, GDN_V_HEADS), F32)
    aux = jnp.concatenate([regroup(a_log.reshape(1, -1).astype(F32), zeros, 0.0),
                           regroup(dt_bias.reshape(1, -1).astype(F32), zeros, 0.0)], axis=0)
    aux = jnp.pad(aux, ((0, 0), (0, LANES - aux.shape[1])))
    w_tail = jnp.concatenate([tail[:, 2 * GDN_V_HEADS:], w_gab], axis=1).astype(BF16)

    q_h, lf, k_h, v_h, act_gates, merge_gates, q_g, k_g, v_g, gab_t = _proj(
        u, w_main, w_tail, lb_logits.astype(F32), cw, aux, layer=layer, seq=seq, hk=hk, gk=gk, gv=gv)

    o_h = _hgrn(q_h, lf, k_h, v_h, act_gates, gv // HEAD_DIM, hgrn_norm, batch=batch, seq=seq)
    o_g = _gdn(q_g, k_g, v_g, act_gates, gab_t, gdn_norm, batch=batch, seq=seq)
    return _merge(o_h, o_g, merge_gates, h, wbh, wbg, wo)


def kernel(x, ffn1_norm, ffn1_w_in, ffn1_w_out, mix_norm, w_in, hgrn_lb_logits, hgrn_out_norm, gdn_conv_w,
           gdn_a_log, gdn_dt_bias, gdn_out_norm, w_branch_hgrn, w_branch_gdn, w_out, ffn2_norm, ffn2_w_in,
           ffn2_w_out, final_norm):
    batch, seq, d = x.shape
    depth = ffn1_norm.shape[0]
    h = x.reshape(batch * seq, d)
    for l in range(depth):
        h, u = _ffn(h, ffn1_norm[l], ffn1_w_in[l], ffn1_w_out[l], mix_norm[l], final=False)
        h = _mixer(u, h, w_in[l], hgrn_lb_logits, hgrn_out_norm[l], gdn_conv_w[l], gdn_a_log[l],
                   gdn_dt_bias[l], gdn_out_norm[l], w_branch_hgrn[l], w_branch_gdn[l], w_out[l],
                   layer=l, batch=batch, seq=seq)
        last = l == depth - 1
        nxt = final_norm if last else ffn1_norm[l + 1]
        out = _ffn(h, ffn2_norm[l], ffn2_w_in[l], ffn2_w_out[l], nxt, final=last)
        h = out if last else out[0]
    return h.reshape(batch, seq, d)
```

```python
import functools

import jax
import jax.numpy as jnp
from jax import lax
from jax.experimental import pallas as pl
from jax.experimental.pallas import tpu as pltpu

F32 = jnp.float32
BF16 = jnp.bfloat16

EPS = 1e-6
CHUNK = 64
SUB = 8
HEAD_DIM = 128
HG_HEADS = 8
GDN_QK_HEADS = 8
GDN_V_HEADS = 16
CONV_K = 4
LANES = 128
SUBLANES = 8
NEG_BIG = -1e30
LOG2E = 1.4426950408889634
VMEM_LIMIT = 56 * 1024 * 1024


def _dot(a, b):
    return jnp.dot(a, b, preferred_element_type=F32)


def _dot_nt(a, b):
    return lax.dot_general(a, b, (((1,), (1,)), ((), ())), preferred_element_type=F32)


def _dot_tn(a, b):
    return lax.dot_general(a, b, (((0,), (0,)), ((), ())), preferred_element_type=F32)


def _sigmoid(x):
    return 1.0 / (1.0 + jnp.exp2(x * -LOG2E))


def _rms(x, w):
    ms = jnp.mean(x * x, axis=-1, keepdims=True)
    return x * lax.rsqrt(ms + EPS) * w


def _params(*sem):
    return pltpu.CompilerParams(dimension_semantics=sem, vmem_limit_bytes=VMEM_LIMIT)


def _resident(shape):
    return pl.BlockSpec(shape, lambda *_: (0,) * len(shape), pipeline_mode=pl.Buffered(1))


def _ffn_body(h_ref, nw_ref, win_ref, wo_ref, onw_ref, *out_refs, ff_tile, final):
    h = h_ref[...]
    xn = _rms(h, nw_ref[...]).astype(BF16)
    d_ff = wo_ref.shape[0]
    acc = jnp.zeros(h.shape, F32)
    for j in range(0, d_ff, ff_tile):
        w = min(ff_tile, d_ff - j)
        a = _dot(xn, win_ref[:, j:j + w].astype(BF16))
        b = _dot(xn, win_ref[:, d_ff + j:d_ff + j + w].astype(BF16))
        g = (a * _sigmoid(a) * b).astype(BF16)
        acc = acc + _dot(g, wo_ref[j:j + w, :].astype(BF16))
    hn = h + 0.5 * acc
    if final:
        out_refs[0][...] = _rms(hn, onw_ref[...])
    else:
        out_refs[0][...] = hn
        out_refs[1][...] = _rms(hn, onw_ref[...]).astype(BF16)


def _ffn(h, norm_w, w_in, w_out, next_norm_w, *, final, tm=512, ff_tile=512):
    n, d = h.shape
    d_ff = w_out.shape[0]
    row = pl.BlockSpec((tm, d), lambda i: (i, 0))
    if final:
        out_shape = jax.ShapeDtypeStruct((n, d), F32)
        out_specs = row
    else:
        out_shape = (jax.ShapeDtypeStruct((n, d), F32), jax.ShapeDtypeStruct((n, d), BF16))
        out_specs = (row, row)
    return pl.pallas_call(
        functools.partial(_ffn_body, ff_tile=ff_tile, final=final),
        out_shape=out_shape,
        grid=(n // tm,),
        in_specs=[row, _resident((1, d)), _resident((d, 2 * d_ff)), _resident((d_ff, d)), _resident((1, d))],
        out_specs=out_specs,
        compiler_params=_params("arbitrary"),
        name="ffn_final" if final else "ffn",
    )(h, norm_w.reshape(1, d), w_in, w_out, next_norm_w.reshape(1, d))


def _regroup_body(w_ref, perm_ref, o_ref, *, moves, small):
    w = w_ref[...]
    for src_col, dst_col, width in moves:
        o_ref[:, dst_col:dst_col + width] = w[:, src_col:src_col + width].astype(BF16)
    narrow = w[:, small[0]:small[0] + small[1]].astype(BF16)
    o_ref[:, o_ref.shape[1] - LANES:] = _dot(narrow, perm_ref[...]).astype(BF16)


def _regroup(w_in, perm, moves, small, *, rows=128):
    d, cols = w_in.shape
    out_cols = sum(m[2] for m in moves) + LANES
    return pl.pallas_call(
        functools.partial(_regroup_body, moves=moves, small=small),
        out_shape=jax.ShapeDtypeStruct((d, out_cols), BF16), grid=(d // rows,),
        in_specs=[pl.BlockSpec((rows, cols), lambda i: (i, 0)), _resident(perm.shape)],
        out_specs=pl.BlockSpec((rows, out_cols), lambda i: (i, 0)),
        compiler_params=_params("arbitrary"), name="regroup",
    )(w_in, perm)


def _proj_body(u_ref, w_ref, logit_ref, cw_ref, gaux_ref,
               qh_ref, lf_ref, kh_ref, vh_ref, act_ref, mg_ref, qg_ref, kg_ref, vg_ref, gab_ref, ext_ref,
               *, layer, tiles_per_seq, tn, widths):
    w_hq, w_hf, w_hi, w_act, w_mg, w_qk, w_gv = widths
    u = u_ref[...]
    tm = u.shape[0]
    proj = lambda c0, width=tn: _dot(u, w_ref[:, c0:c0 + width])
    bases = [0]
    for w in widths:
        bases.append(bases[-1] + w)

    def hq_tile(o):
        p = proj(bases[0] + o)
        qh_ref[:, o:o + tn] = (p * _sigmoid(p) * HEAD_DIM ** -0.5).astype(BF16)

    lg = logit_ref[...]
    e = jnp.exp(lg - jnp.max(lg, axis=0, keepdims=True))
    lb_all = jnp.sum(e[0:layer + 1, :], axis=0, keepdims=True) / jnp.sum(e, axis=0, keepdims=True)

    def hf_tile(o):
        s = _sigmoid(proj(bases[1] + o))
        lb = lb_all[:, o:o + tn]
        lf_ref[:, o:o + tn] = jnp.log2(lb + (1.0 - lb) * s)
        kh_ref[:, o:o + tn] = ((1.0 - lb) * (1.0 - s)).astype(BF16)

    def hi_tile(o):
        vh_ref[:, o:o + tn] = proj(bases[2] + o).astype(BF16)

    def act_tile(o):
        p = proj(bases[3] + o)
        act_ref[:, o:o + tn] = (p * _sigmoid(p)).astype(BF16)

    def mg_tile(o):
        mg_ref[:, o:o + tn] = _sigmoid(proj(bases[4] + o)).astype(BF16)

    @pl.when(pl.program_id(0) % tiles_per_seq == 0)
    def _():
        ext_ref[0:SUBLANES, :] = jnp.zeros((SUBLANES, ext_ref.shape[1]), F32)

    def conv_tile(o):
        p = proj(bases[5] + o)
        ext_ref[SUBLANES:, o:o + tn] = p
        acc = p * cw_ref[CONV_K - 1:CONV_K, o:o + tn]
        for d in range(1, CONV_K):
            acc = acc + ext_ref[SUBLANES - d:SUBLANES - d + tm, o:o + tn] * cw_ref[CONV_K - 1 - d:CONV_K - d, o:o + tn]
        y = acc * _sigmoid(acc)
        if o < w_qk:
            is_q = o < w_qk // 2
            out_ref, o_out = (qg_ref, o) if is_q else (kg_ref, o - w_qk // 2)
            for s in range(0, tn, HEAD_DIM):
                ys = y[:, s:s + HEAD_DIM]
                inv = lax.rsqrt(jnp.sum(ys * ys, axis=-1, keepdims=True) + EPS) * (HEAD_DIM ** -0.5 if is_q else 1.0)
                out_ref[:, o_out + s:o_out + s + HEAD_DIM] = (ys * inv).astype(BF16)
        else:
            vg_ref[:, o - w_qk:o - w_qk + tn] = y.astype(BF16)

    light = [(f, o) for f, w in ((hq_tile, w_hq), (hf_tile, w_hf), (hi_tile, w_hi), (act_tile, w_act), (mg_tile, w_mg))
             for o in range(0, w, tn)]
    heavy = [(conv_tile, o) for o in range(0, w_qk + w_gv, tn)]
    per_heavy = -(-len(light) // len(heavy))
    while light or heavy:
        for f, o in light[:per_heavy]:
            f(o)
        light = light[per_heavy:]
        if heavy:
            f, o = heavy.pop(0)
            f(o)
    ext_ref[0:SUBLANES, :] = ext_ref[tm:tm + SUBLANES, :]

    p = proj(bases[7], LANES)
    x = p + gaux_ref[1:2, :]
    softplus = jnp.maximum(x, 0.0) + jnp.log1p(jnp.exp(-jnp.abs(x)))
    g = -jnp.exp(gaux_ref[0:1, :]) * softplus
    lane = lax.broadcasted_iota(jnp.int32, p.shape, 1)
    gab_ref[...] = jnp.where(lane % SUBLANES < 2, g, _sigmoid(p)).T


def _proj(u, w_all, lb_logits, cw, gaux, widths, *, layer, seq, tm=256, tn=256):
    n, d = u.shape
    w_hq, w_hf, w_hi, w_act, w_mg, w_qk, w_gv = widths
    row = lambda width: pl.BlockSpec((tm, width), lambda i: (i, 0))
    out_widths = (w_hq, w_hf, w_hf, w_hi, w_act, w_mg, w_qk // 2, w_qk // 2, w_gv)
    out_dtypes = (BF16, F32, BF16, BF16, BF16, BF16, BF16, BF16, BF16)
    out_shape = tuple(jax.ShapeDtypeStruct((n, w), dt) for w, dt in zip(out_widths, out_dtypes))
    out_shape += (jax.ShapeDtypeStruct((LANES, n), F32),)
    out_specs = tuple(row(w) for w in out_widths) + (pl.BlockSpec((LANES, tm), lambda i: (0, i)),)
    return pl.pallas_call(
        functools.partial(_proj_body, layer=layer, tiles_per_seq=seq // tm, tn=tn, widths=widths),
        out_shape=out_shape, grid=(n // tm,),
        in_specs=[row(d), _resident(w_all.shape), _resident(lb_logits.shape), _resident(cw.shape),
                  _resident(gaux.shape)],
        out_specs=out_specs,
        scratch_shapes=[pltpu.VMEM((tm + SUBLANES, w_qk + w_gv), F32)],
        compiler_params=_params("arbitrary"), name="proj",
    )(u, w_all, lb_logits, cw, gaux)


def _hgrn_body(q_ref, lf_ref, k_ref, v_ref, gate_ref, nw_ref, tri_ref, o_ref, st_ref, scores_ref, inter_ref, vlag_ref,
               bk_ref, *, blocks_per_seq):
    step = pl.program_id(0)

    @pl.when(step == 0)
    def _():
        for ref in (scores_ref, inter_ref, vlag_ref):
            ref[...] = jnp.zeros_like(ref)

    @pl.when(step % blocks_per_seq == 0)
    def _():
        st_ref[...] = jnp.zeros_like(st_ref)

    ts = q_ref.shape[0]
    n_chunks = ts // CHUNK

    for c in range(n_chunks):
        r0 = c * CHUNK
        o = inter_ref[c] + _dot(scores_ref[c], vlag_ref[r0:r0 + CHUNK, :])
        o_ref[r0:r0 + CHUNK, :] = (_rms(o, nw_ref[...]) * gate_ref[r0:r0 + CHUNK, :].astype(F32)).astype(BF16)

    lf = lf_ref[...]
    hi = lf.astype(BF16)
    r1 = lf - hi.astype(F32)
    mid = r1.astype(BF16)
    lo = (r1 - mid.astype(F32)).astype(BF16)
    by_chunk = lambda x: jnp.concatenate([x[c * CHUNK:(c + 1) * CHUNK] for c in range(n_chunks)], axis=1)
    b_all = _dot(tri_ref[...], jnp.concatenate([by_chunk(hi), by_chunk(mid), by_chunk(lo)], axis=0))

    row = lax.broadcasted_iota(jnp.int32, (CHUNK, HEAD_DIM), 0)
    t_idx = lax.broadcasted_iota(jnp.int32, (CHUNK, CHUNK), 0)
    s_idx = lax.broadcasted_iota(jnp.int32, (CHUNK, CHUNK), 1)
    levels = []
    half = CHUNK // 2
    while half >= SUB:
        size = 2 * half
        levels.append((half, row % size >= half,
                       (t_idx // size == s_idx // size) & (t_idx % size >= half) & (s_idx % size < half)))
        half //= 2
    causal_bias = [jnp.where(row % SUB >= j, 0.0, NEG_BIG) for j in range(SUB)]
    place = [s_idx == (t_idx // SUB) * SUB + j for j in range(SUB)]
    vlag_ref[...] = v_ref[...]
    for c in range(n_chunks):
        r0 = c * CHUNK
        b = b_all[:, c * HEAD_DIM:(c + 1) * HEAD_DIM]
        q = q_ref[r0:r0 + CHUNK, :].astype(F32)
        k = k_ref[r0:r0 + CHUNK, :].astype(F32)
        v = v_ref[r0:r0 + CHUNK, :]
        b_end = b[CHUNK - 1:CHUNK, :]
        st = st_ref[...]
        inter_ref[c] = _dot_nt((q * jnp.exp2(b)).astype(BF16), st.astype(BF16))

        scores = jnp.zeros((CHUNK, CHUNK), F32)
        for half, past_mid, pair in levels:
            size = 2 * half
            b_mid = jnp.concatenate(
                [jnp.broadcast_to(b[m + half - 1:m + half, :], (size, HEAD_DIM)) for m in range(0, CHUNK, size)], axis=0)
            x = (jnp.where(past_mid, q, k) * jnp.exp2(-jnp.abs(b - b_mid))).astype(BF16)
            scores = jnp.where(pair, _dot_nt(x, x), scores)

        bk_ref[0] = b
        bk_ref[1] = k
        for j in range(SUB):
            pick = lambda a: jnp.concatenate(
                [jnp.broadcast_to(bk_ref[a, m * SUB + j:m * SUB + j + 1, :], (SUB, HEAD_DIM))
                 for m in range(CHUNK // SUB)], axis=0)
            dec = jnp.exp2(b - pick(0) + causal_bias[j])
            col = jnp.sum(q * pick(1) * dec, axis=-1, keepdims=True)
            scores = jnp.where(place[j], col, scores)
        scores_ref[c] = scores.astype(BF16)

        k_end = (k * jnp.exp2(b_end - b)).astype(BF16)
        st_ref[...] = st * jnp.exp2(b_end) + _dot_tn(v, k_end)


def _hgrn(q, lf, k, v, gates, gate_col0, norm_w, *, batch, seq, ts=1024):
    n = q.shape[0]
    nt = seq // ts
    n_chunks = ts // CHUNK
    total = batch * HG_HEADS * nt

    def block(step, col0):
        step = jnp.clip(step, 0, total - 1)
        return (step // (HG_HEADS * nt)) * nt + step % nt, (step // nt) % HG_HEADS + col0

    fresh = pl.BlockSpec((ts, HEAD_DIM), lambda s: block(s, 0))
    lagged = lambda col0: pl.BlockSpec((ts, HEAD_DIM), lambda s: block(s - 1, col0))
    tri = (jnp.arange(CHUNK)[:, None] >= jnp.arange(CHUNK)[None, :]).astype(BF16)
    tri3 = jnp.concatenate([tri, tri, tri], axis=1)
    return pl.pallas_call(
        functools.partial(_hgrn_body, blocks_per_seq=nt),
        out_shape=jax.ShapeDtypeStruct((n, HG_HEADS * HEAD_DIM), BF16),
        grid=(total + 1,),
        in_specs=[fresh, fresh, fresh, fresh, lagged(gate_col0), _resident((1, HEAD_DIM)),
                  _resident((CHUNK, 3 * CHUNK))],
        out_specs=lagged(0),
        scratch_shapes=[pltpu.VMEM((HEAD_DIM, HEAD_DIM), F32),
                        pltpu.VMEM((n_chunks, CHUNK, CHUNK), BF16),
                        pltpu.VMEM((n_chunks, CHUNK, HEAD_DIM), F32),
                        pltpu.VMEM((ts, HEAD_DIM), BF16),
                        pltpu.VMEM((2, CHUNK, HEAD_DIM), F32)],
        compiler_params=_params("arbitrary"), name="hgrn2",
    )(q, lf, k, v, gates, norm_w.reshape(1, HEAD_DIM), tri3)


def _split3_bf16(x):
    hi = x.astype(BF16)
    r1 = x - hi.astype(F32)
    mid = r1.astype(BF16)
    lo = (r1 - mid.astype(F32)).astype(BF16)
    return hi, mid, lo


def _unit_lower_inverse_pairs(a_pairs):
    shape = a_pairs[0].shape
    row = lax.broadcasted_iota(jnp.int32, shape, 0)
    lane = lax.broadcasted_iota(jnp.int32, shape, 1)
    first = lane < CHUNK
    eye = jnp.where(row == lane % CHUNK, 1.0, 0.0)

    def level(power, partial):
        top = jnp.concatenate([jnp.where(first, power, 0.0), jnp.where(first, partial, 0.0)], axis=1)
        bottom = jnp.concatenate([jnp.where(first, 0.0, power), jnp.where(first, 0.0, partial)], axis=1)
        res = _dot(power.astype(BF16), jnp.concatenate([top, bottom], axis=0).astype(BF16))
        return res[:, :2 * CHUNK], partial + res[:, 2 * CHUNK:]

    state = [level(-a, eye) for a in a_pairs]
    span = 2
    while span < CHUNK:
        state = [level(power, partial) for power, partial in state]
        span *= 2
    return [partial for _, partial in state]


def _gdn_body(q_ref, k_ref, v_ref, z_ref, gb_ref, nw_ref, triu_ref, tril_ref, o_ref,
              s_ref, uw_ref, xt_ref, qg_ref, dec_ab_ref, lhs_ref, ktu_ref, qku_ref, dec_bc_ref,
              *, blocks_per_seq, lag):
    ts = q_ref.shape[0]
    n_chunks = ts // CHUNK
    heads = q_ref.shape[1] // HEAD_DIM
    rep = s_ref.shape[0] // heads
    dup = 2 * CHUNK
    pairs = [(h, c) for h in range(heads) for c in range(n_chunks)]
    prob = lambda h, c, i: (h * n_chunks + c) * rep + i

    step = pl.program_id(0)

    @pl.when(step == 0)
    def _():
        for ref in (uw_ref, xt_ref, qg_ref, dec_ab_ref, lhs_ref, ktu_ref, qku_ref, dec_bc_ref):
            ref[...] = jnp.zeros_like(ref)

    @pl.when((step == 0) | (step % blocks_per_seq == lag % blocks_per_seq))
    def _():
        s_ref[...] = jnp.zeros_like(s_ref)

    g_stack = jnp.concatenate([gb_ref[SUBLANES * h:SUBLANES * (h + 1), c * CHUNK:(c + 1) * CHUNK]
                               for h, c in pairs], axis=0)
    g_cols = jnp.concatenate([g_stack, jnp.zeros_like(g_stack)], axis=1).T[:CHUNK]
    gam_rows = _dot(jnp.concatenate(_split3_bf16(g_stack), axis=1), triu_ref[...])
    gam_cols = _dot(tril_ref[...], jnp.concatenate(_split3_bf16(g_cols), axis=0))
    t_dup = lax.broadcasted_iota(jnp.int32, (CHUNK, dup), 0)
    s_dup_raw = lax.broadcasted_iota(jnp.int32, (CHUNK, dup), 1)
    s_dup = s_dup_raw % CHUNK
    s_idx = lax.broadcasted_iota(jnp.int32, (CHUNK, CHUNK), 0)
    t_idx = lax.broadcasted_iota(jnp.int32, (CHUNK, CHUNK), 1)
    first = s_dup_raw < CHUNK
    a_pairs, rhs_all, xt_all, qg_all, dec_all = [], {}, {}, {}, {}
    for h, c in pairs:
        r0 = c * CHUNK
        q = q_ref[r0:r0 + CHUNK, h * HEAD_DIM:(h + 1) * HEAD_DIM]
        k = k_ref[r0:r0 + CHUNK, h * HEAD_DIM:(h + 1) * HEAD_DIM]
        qf = q.astype(F32)
        kf = k.astype(F32)
        kkq = _dot_nt(k, jnp.concatenate([k, k, q, q], axis=0))
        kk_dup = kkq[:, :dup]
        kq = kkq[:, dup:dup + CHUNK]
        col0 = SUBLANES * (h * n_chunks + c)
        gam_c = [gam_cols[:, col0 + i:col0 + i + 1] for i in range(rep)]
        beta = [g_cols[:, col0 + rep + i:col0 + rep + i + 1] for i in range(rep)]
        gam_r_pair = jnp.where(first[0:1], gam_rows[col0:col0 + 1, :], gam_rows[col0 + 1:col0 + 2, :])
        lmat_pair = jnp.exp(jnp.where(t_dup >= s_dup, jnp.where(first, gam_c[0], gam_c[1]) - gam_r_pair, NEG_BIG))
        a_pairs.append(jnp.where(t_dup > s_dup, jnp.where(first, beta[0], beta[1]) * kk_dup * lmat_pair, 0.0))
        for i in range(rep):
            p = prob(h, c, i)
            vcol = (h * rep + i) * HEAD_DIM
            gam_r = gam_rows[col0 + i:col0 + i + 1, :CHUNK]
            g_end = gam_c[i][CHUNK - 1:CHUNK, :]
            lmat_t = jnp.exp(jnp.where(t_idx >= s_idx, gam_r - gam_c[i], NEG_BIG))
            e_gam = jnp.exp(gam_c[i])
            v = v_ref[r0:r0 + CHUNK, vcol:vcol + HEAD_DIM].astype(F32)
            rhs_all[p] = jnp.concatenate([(v * beta[i]).astype(BF16), (kf * (beta[i] * e_gam)).astype(BF16)], axis=1)
            k_end = kf * jnp.exp(g_end - gam_c[i])
            xt_all[p] = jnp.concatenate([k_end, kq * lmat_t], axis=1).T.astype(BF16)
            qg_all[p] = qf * e_gam
            dec_all[p] = jnp.broadcast_to(jnp.exp(g_end), dec_ab_ref.shape[1:])

    zero_state = jnp.zeros((HEAD_DIM, HEAD_DIM), BF16)
    states = [s_ref[j] for j in range(heads * rep)]
    for c in range(n_chunks):
        r0 = c * CHUNK
        for h in range(heads):
            diag = jnp.concatenate(
                [jnp.concatenate([states[h * rep + i].astype(BF16) if j == i else zero_state for j in range(rep)],
                                 axis=1) for i in range(rep)], axis=0)
            m_all = _dot(lhs_ref[h * n_chunks + c], diag)
            for i in range(rep):
                p = prob(h, c, i)
                vcol = (h * rep + i) * HEAD_DIM
                m = m_all[:, i * HEAD_DIM:(i + 1) * HEAD_DIM]
                states[h * rep + i] = states[h * rep + i] * dec_bc_ref[p][0:1, 0:1] + ktu_ref[p] - m[:HEAD_DIM]
                o = m[HEAD_DIM:] + qku_ref[p]
                z = z_ref[r0:r0 + CHUNK, vcol:vcol + HEAD_DIM].astype(F32)
                o_ref[r0:r0 + CHUNK, vcol:vcol + HEAD_DIM] = (_rms(o, nw_ref[...]) * z).astype(BF16)
    for j in range(heads * rep):
        s_ref[j] = states[j]

    for h, c in pairs:
        lhs = []
        for i in range(rep):
            p = prob(h, c, i)
            r = _dot(xt_ref[p], uw_ref[p])
            ktu_ref[p] = r[:HEAD_DIM, :HEAD_DIM]
            qku_ref[p] = r[HEAD_DIM:, :HEAD_DIM]
            dec_bc_ref[p] = dec_ab_ref[p]
            lhs.append(jnp.concatenate([r[:HEAD_DIM, HEAD_DIM:], qg_ref[p] - r[HEAD_DIM:, HEAD_DIM:]], axis=0))
        lhs_ref[h * n_chunks + c] = jnp.concatenate(lhs, axis=1).astype(BF16)

    for p in range(len(rhs_all)):
        xt_ref[p] = xt_all[p]
        qg_ref[p] = qg_all[p]
        dec_ab_ref[p] = dec_all[p]
    inverses = _unit_lower_inverse_pairs(a_pairs)
    for n, (h, c) in enumerate(pairs):
        inv_pair = inverses[n].astype(BF16)
        for i in range(rep):
            p = prob(h, c, i)
            pad = jnp.zeros_like(rhs_all[p])
            stacked = jnp.concatenate([rhs_all[p], pad] if i == 0 else [pad, rhs_all[p]], axis=0)
            uw_ref[p] = _dot(inv_pair, stacked).astype(BF16)


def _gdn(q, k, v, gates, gab_t, norm_w, *, batch, seq, ts=128, heads=8):
    n = q.shape[0]
    nt = seq // ts
    groups = GDN_QK_HEADS // heads
    rep = GDN_V_HEADS // GDN_QK_HEADS
    n_chunks = ts // CHUNK
    n_pairs = heads * n_chunks
    n_probs = n_pairs * rep
    lag = 2
    total = batch * groups * nt

    def block(step):
        step = jnp.clip(step, 0, total - 1)
        return (step // (groups * nt)) * nt + step % nt, (step // nt) % groups

    fresh = lambda width: pl.BlockSpec((ts, width * HEAD_DIM), lambda s: block(s))
    lagged = lambda width: pl.BlockSpec((ts, width * HEAD_DIM), lambda s: block(s - lag))
    r = jnp.arange(CHUNK)[:, None]
    c = jnp.arange(CHUNK)[None, :]
    upper = (r <= c).astype(BF16)
    upper_dup = jnp.concatenate([upper, upper], axis=1)
    triu3 = jnp.concatenate([upper_dup] * 3, axis=0)
    tril3 = jnp.concatenate([(r >= c).astype(BF16)] * 3, axis=1)
    return pl.pallas_call(
        functools.partial(_gdn_body, blocks_per_seq=nt, lag=lag),
        out_shape=jax.ShapeDtypeStruct((n, GDN_V_HEADS * HEAD_DIM), BF16),
        grid=(total + lag,),
        in_specs=[fresh(heads), fresh(heads), fresh(heads * rep), lagged(heads * rep),
                  pl.BlockSpec((SUBLANES * heads, ts), lambda s: block(s)[::-1]),
                  _resident((1, HEAD_DIM)), _resident(triu3.shape), _resident(tril3.shape)],
        out_specs=lagged(heads * rep),
        scratch_shapes=[pltpu.VMEM((heads * rep, HEAD_DIM, HEAD_DIM), F32),
                        pltpu.VMEM((n_probs, CHUNK, 2 * HEAD_DIM), BF16),
                        pltpu.VMEM((n_probs, HEAD_DIM + CHUNK, CHUNK), BF16),
                        pltpu.VMEM((n_probs, CHUNK, HEAD_DIM), F32),
                        pltpu.VMEM((n_probs, SUBLANES, LANES), F32),
                        pltpu.VMEM((n_pairs, HEAD_DIM + CHUNK, rep * HEAD_DIM), BF16),
                        pltpu.VMEM((n_probs, HEAD_DIM, HEAD_DIM), F32),
                        pltpu.VMEM((n_probs, CHUNK, HEAD_DIM), F32),
                        pltpu.VMEM((n_probs, SUBLANES, LANES), F32)],
        compiler_params=_params("arbitrary"), name="gated_delta",
    )(q, k, v, gates, gab_t, norm_w.reshape(1, HEAD_DIM), triu3, tril3)


def _merge_body(oh_ref, og_ref, gate_ref, h_ref, wbh_ref, wbg_ref, wo_ref, o_ref):
    d = h_ref.shape[1]
    yh = _dot(oh_ref[...], wbh_ref[...].astype(BF16))
    yg = _dot(og_ref[...], wbg_ref[...].astype(BF16))
    y = gate_ref[:, :d].astype(F32) * yh + gate_ref[:, d:].astype(F32) * yg
    o_ref[...] = h_ref[...] + _dot(y.astype(BF16), wo_ref[...].astype(BF16))


def _merge(o_h, o_g, merge_gates, h, wbh, wbg, wo, *, tm=512):
    n, d = h.shape
    row = lambda width: pl.BlockSpec((tm, width), lambda i: (i, 0))
    return pl.pallas_call(
        _merge_body, out_shape=jax.ShapeDtypeStruct((n, d), F32), grid=(n // tm,),
        in_specs=[row(o_h.shape[1]), row(o_g.shape[1]), row(2 * d), row(d),
                  _resident(wbh.shape), _resident(wbg.shape), _resident(wo.shape)],
        out_specs=row(d),
        compiler_params=_params("arbitrary"), name="merge",
    )(o_h, o_g, merge_gates, h, wbh, wbg, wo)


def _mixer(u, h, w_in, lb_logits, hgrn_norm, conv_w, a_log, dt_bias, gdn_norm, wbh, wbg, wo, *, layer, batch, seq):
    n, d = h.shape
    hk = HG_HEADS * HEAD_DIM
    gk = GDN_QK_HEADS * HEAD_DIM
    gv = GDN_V_HEADS * HEAD_DIM
    sizes = (hk, hk, hk, hk, gk, gk, gv, GDN_V_HEADS, GDN_V_HEADS, gv, d, d)
    offs = [0]
    for s in sizes:
        offs.append(offs[-1] + s)
    conv_pad = jnp.zeros((SUBLANES - CONV_K, conv_w.shape[1]), F32)
    cw = jnp.concatenate([conv_w, conv_pad], axis=0)

    rep = GDN_V_HEADS // GDN_QK_HEADS
    pad = SUBLANES - 2 * rep
    regroup = lambda a, b, fill: jnp.concatenate(
        [a.reshape(-1, GDN_QK_HEADS, rep), b.reshape(-1, GDN_QK_HEADS, rep),
         jnp.full((a.shape[0], GDN_QK_HEADS, pad), fill, a.dtype)], axis=2).reshape(a.shape[0], -1)
    eye = jnp.eye(GDN_V_HEADS, dtype=F32)
    zero = jnp.zeros_like(eye)
    perm = jnp.concatenate([regroup(eye, zero, 0.0), regroup(zero, eye, 0.0)], axis=0)
    perm = jnp.pad(perm, ((0, 0), (0, LANES - perm.shape[1]))).astype(BF16)
    zeros = jnp.zeros((1, GDN_V_HEADS), F32)
    aux = jnp.concatenate([regroup(a_log.reshape(1, -1).astype(F32), zeros, 0.0),
                           regroup(dt_bias.reshape(1, -1).astype(F32), zeros, 0.0)], axis=0)
    aux = jnp.pad(aux, ((0, 0), (0, LANES - aux.shape[1])))

    moves, dst = [], 0
    for a, b in ((0, 3), (9, 10), (3, 4), (10, 12), (4, 7)):
        moves.append((offs[a], dst, offs[b] - offs[a]))
        dst += offs[b] - offs[a]
    w_all = _regroup(w_in, perm, tuple(moves), (offs[7], offs[9] - offs[7]))
    widths = (hk, hk, hk, gv + hk, 2 * d, 2 * gk, gv)
    q_h, lf, k_h, v_h, act_gates, merge_gates, q_g, k_g, v_g, gab_t = _proj(
        u, w_all, lb_logits.astype(F32), cw, aux, widths, layer=layer, seq=seq)

    o_h = _hgrn(q_h, lf, k_h, v_h, act_gates, gv // HEAD_DIM, hgrn_norm, batch=batch, seq=seq)
    o_g = _gdn(q_g, k_g, v_g, act_gates, gab_t, gdn_norm, batch=batch, seq=seq)
    return _merge(o_h, o_g, merge_gates, h, wbh, wbg, wo)


def kernel(x, ffn1_norm, ffn1_w_in, ffn1_w_out, mix_norm, w_in, hgrn_lb_logits, hgrn_out_norm, gdn_conv_w,
           gdn_a_log, gdn_dt_bias, gdn_out_norm, w_branch_hgrn, w_branch_gdn, w_out, ffn2_norm, ffn2_w_in,
           ffn2_w_out, final_norm):
    batch, seq, d = x.shape
    depth = ffn1_norm.shape[0]
    h = x.reshape(batch * seq, d)
    for l in range(depth):
        h, u = _ffn(h, ffn1_norm[l], ffn1_w_in[l], ffn1_w_out[l], mix_norm[l], final=False)
        h = _mixer(u, h, w_in[l], hgrn_lb_logits, hgrn_out_norm[l], gdn_conv_w[l], gdn_a_log[l],
                   gdn_dt_bias[l], gdn_out_norm[l], w_branch_hgrn[l], w_branch_gdn[l], w_out[l],
                   layer=l, batch=batch, seq=seq)
        last = l == depth - 1
        nxt = final_norm if last else ffn1_norm[l + 1]
        out = _ffn(h, ffn2_norm[l], ffn2_w_in[l], ffn2_w_out[l], nxt, final=last)
        h = out if last else out[0]
    return h.reshape(batch, seq, d)
```

```python
import functools

import jax
import jax.numpy as jnp
from jax import lax
from jax.experimental import pallas as pl
from jax.experimental.pallas import tpu as pltpu

F32 = jnp.float32
BF16 = jnp.bfloat16

EPS = 1e-6
CHUNK = 64
SUB = 8
HEAD_DIM = 128
HG_HEADS = 8
GDN_QK_HEADS = 8
GDN_V_HEADS = 16
CONV_K = 4
LANES = 128
SUBLANES = 8
NEG_BIG = -1e30
LOG2E = 1.4426950408889634
VMEM_LIMIT = 56 * 1024 * 1024


def _dot(a, b):
    return jnp.dot(a, b, preferred_element_type=F32)


def _dot_nt(a, b):
    return lax.dot_general(a, b, (((1,), (1,)), ((), ())), preferred_element_type=F32)


def _dot_tn(a, b):
    return lax.dot_general(a, b, (((0,), (0,)), ((), ())), preferred_element_type=F32)


def _sigmoid(x):
    return 1.0 / (1.0 + jnp.exp2(x * -LOG2E))


def _rms(x, w):
    ms = jnp.mean(x * x, axis=-1, keepdims=True)
    return x * lax.rsqrt(ms + EPS) * w


def _params(*sem):
    return pltpu.CompilerParams(dimension_semantics=sem, vmem_limit_bytes=VMEM_LIMIT)


def _resident(shape):
    return pl.BlockSpec(shape, lambda *_: (0,) * len(shape), pipeline_mode=pl.Buffered(1))


def _ffn_body(h_ref, nw_ref, win_ref, wo_ref, onw_ref, *out_refs, ff_tile, final):
    h = h_ref[...]
    xn = _rms(h, nw_ref[...]).astype(BF16)
    d_ff = wo_ref.shape[0]
    acc = jnp.zeros(h.shape, F32)
    for j in range(0, d_ff, ff_tile):
        w = min(ff_tile, d_ff - j)
        a = _dot(xn, win_ref[:, j:j + w].astype(BF16))
        b = _dot(xn, win_ref[:, d_ff + j:d_ff + j + w].astype(BF16))
        g = (a * _sigmoid(a) * b).astype(BF16)
        acc = acc + _dot(g, wo_ref[j:j + w, :].astype(BF16))
    hn = h + 0.5 * acc
    if final:
        out_refs[0][...] = _rms(hn, onw_ref[...])
    else:
        out_refs[0][...] = hn
        out_refs[1][...] = _rms(hn, onw_ref[...]).astype(BF16)


def _ffn(h, norm_w, w_in, w_out, next_norm_w, *, final, tm=512, ff_tile=512):
    n, d = h.shape
    d_ff = w_out.shape[0]
    row = pl.BlockSpec((tm, d), lambda i: (i, 0))
    if final:
        out_shape = jax.ShapeDtypeStruct((n, d), F32)
        out_specs = row
    else:
        out_shape = (jax.ShapeDtypeStruct((n, d), F32), jax.ShapeDtypeStruct((n, d), BF16))
        out_specs = (row, row)
    return pl.pallas_call(
        functools.partial(_ffn_body, ff_tile=ff_tile, final=final),
        out_shape=out_shape,
        grid=(n // tm,),
        in_specs=[row, _resident((1, d)), _resident((d, 2 * d_ff)), _resident((d_ff, d)), _resident((1, d))],
        out_specs=out_specs,
        compiler_params=_params("arbitrary"),
        name="ffn_final" if final else "ffn",
    )(h, norm_w.reshape(1, d), w_in, w_out, next_norm_w.reshape(1, d))


def _regroup_body(w_ref, perm_ref, o_ref, *, moves, small):
    w = w_ref[...]
    for src_col, dst_col, width in moves:
        o_ref[:, dst_col:dst_col + width] = w[:, src_col:src_col + width].astype(BF16)
    narrow = w[:, small[0]:small[0] + small[1]].astype(BF16)
    o_ref[:, o_ref.shape[1] - LANES:] = _dot(narrow, perm_ref[...]).astype(BF16)


def _regroup(w_in, perm, moves, small, *, layer, rows=128):
    _, d, cols = w_in.shape
    out_cols = sum(m[2] for m in moves) + LANES
    return pl.pallas_call(
        functools.partial(_regroup_body, moves=moves, small=small),
        out_shape=jax.ShapeDtypeStruct((d, out_cols), BF16), grid=(d // rows,),
        in_specs=[pl.BlockSpec((None, rows, cols), lambda i: (layer, i, 0)), _resident(perm.shape)],
        out_specs=pl.BlockSpec((rows, out_cols), lambda i: (i, 0)),
        compiler_params=_params("arbitrary"), name="regroup",
    )(w_in, perm)


def _proj_body(u_ref, w_ref, logit_ref, cw_ref, gaux_ref,
               qh_ref, lf_ref, kh_ref, vh_ref, act_ref, mg_ref, qg_ref, kg_ref, vg_ref, gab_ref, ext_ref,
               *, layer, tiles_per_seq, tn, widths):
    w_hq, w_hf, w_hi, w_act, w_mg, w_qk, w_gv = widths
    u = u_ref[...]
    tm = u.shape[0]
    proj = lambda c0, width=tn: _dot(u, w_ref[:, c0:c0 + width])
    bases = [0]
    for w in widths:
        bases.append(bases[-1] + w)

    def hq_tile(o):
        p = proj(bases[0] + o)
        qh_ref[:, o:o + tn] = (p * _sigmoid(p) * HEAD_DIM ** -0.5).astype(BF16)

    lg = logit_ref[...]
    e = jnp.exp(lg - jnp.max(lg, axis=0, keepdims=True))
    lb_all = jnp.sum(e[0:layer + 1, :], axis=0, keepdims=True) / jnp.sum(e, axis=0, keepdims=True)

    def hf_tile(o):
        s = _sigmoid(proj(bases[1] + o))
        lb = lb_all[:, o:o + tn]
        lf_ref[:, o:o + tn] = jnp.log2(lb + (1.0 - lb) * s)
        kh_ref[:, o:o + tn] = ((1.0 - lb) * (1.0 - s)).astype(BF16)

    def hi_tile(o):
        vh_ref[:, o:o + tn] = proj(bases[2] + o).astype(BF16)

    def act_tile(o):
        p = proj(bases[3] + o)
        act_ref[:, o:o + tn] = (p * _sigmoid(p)).astype(BF16)

    def mg_tile(o):
        mg_ref[:, o:o + tn] = _sigmoid(proj(bases[4] + o)).astype(BF16)

    @pl.when(pl.program_id(0) % tiles_per_seq == 0)
    def _():
        ext_ref[0:SUBLANES, :] = jnp.zeros((SUBLANES, ext_ref.shape[1]), F32)

    def conv_tile(o):
        p = proj(bases[5] + o)
        ext_ref[SUBLANES:, o:o + tn] = p
        acc = p * cw_ref[CONV_K - 1:CONV_K, o:o + tn]
        for d in range(1, CONV_K):
            acc = acc + ext_ref[SUBLANES - d:SUBLANES - d + tm, o:o + tn] * cw_ref[CONV_K - 1 - d:CONV_K - d, o:o + tn]
        y = acc * _sigmoid(acc)
        if o < w_qk:
            is_q = o < w_qk // 2
            out_ref, o_out = (qg_ref, o) if is_q else (kg_ref, o - w_qk // 2)
            for s in range(0, tn, HEAD_DIM):
                ys = y[:, s:s + HEAD_DIM]
                inv = lax.rsqrt(jnp.sum(ys * ys, axis=-1, keepdims=True) + EPS) * (HEAD_DIM ** -0.5 if is_q else 1.0)
                out_ref[:, o_out + s:o_out + s + HEAD_DIM] = (ys * inv).astype(BF16)
        else:
            vg_ref[:, o - w_qk:o - w_qk + tn] = y.astype(BF16)

    light = [(f, o) for f, w in ((hq_tile, w_hq), (hf_tile, w_hf), (hi_tile, w_hi), (act_tile, w_act), (mg_tile, w_mg))
             for o in range(0, w, tn)]
    heavy = [(conv_tile, o) for o in range(0, w_qk + w_gv, tn)]
    per_heavy = -(-len(light) // len(heavy))
    while light or heavy:
        for f, o in light[:per_heavy]:
            f(o)
        light = light[per_heavy:]
        if heavy:
            f, o = heavy.pop(0)
            f(o)
    ext_ref[0:SUBLANES, :] = ext_ref[tm:tm + SUBLANES, :]

    p = proj(bases[7], LANES)
    x = p + gaux_ref[1:2, :]
    softplus = jnp.maximum(x, 0.0) + jnp.log1p(jnp.exp(-jnp.abs(x)))
    g = -jnp.exp(gaux_ref[0:1, :]) * softplus
    lane = lax.broadcasted_iota(jnp.int32, p.shape, 1)
    gab_ref[...] = jnp.where(lane % SUBLANES < 2, g, _sigmoid(p)).T


def _proj(u, w_all, lb_logits, cw, gaux, widths, *, layer, seq, tm=256, tn=256):
    n, d = u.shape
    w_hq, w_hf, w_hi, w_act, w_mg, w_qk, w_gv = widths
    row = lambda width: pl.BlockSpec((tm, width), lambda i: (i, 0))
    out_widths = (w_hq, w_hf, w_hf, w_hi, w_act, w_mg, w_qk // 2, w_qk // 2, w_gv)
    out_dtypes = (BF16, F32, BF16, BF16, BF16, BF16, BF16, BF16, BF16)
    out_shape = tuple(jax.ShapeDtypeStruct((n, w), dt) for w, dt in zip(out_widths, out_dtypes))
    out_shape += (jax.ShapeDtypeStruct((LANES, n), F32),)
    out_specs = tuple(row(w) for w in out_widths) + (pl.BlockSpec((LANES, tm), lambda i: (0, i)),)
    return pl.pallas_call(
        functools.partial(_proj_body, layer=layer, tiles_per_seq=seq // tm, tn=tn, widths=widths),
        out_shape=out_shape, grid=(n // tm,),
        in_specs=[row(d), _resident(w_all.shape), _resident(lb_logits.shape), _resident(cw.shape),
                  _resident(gaux.shape)],
        out_specs=out_specs,
        scratch_shapes=[pltpu.VMEM((tm + SUBLANES, w_qk + w_gv), F32)],
        compiler_params=_params("arbitrary"), name="proj",
    )(u, w_all, lb_logits, cw, gaux)


def _hgrn_body(q_ref, lf_ref, k_ref, v_ref, gate_ref, nw_ref, tri_ref, o_ref, st_ref, scores_ref, inter_ref, vlag_ref,
               bk_ref, *, blocks_per_seq):
    step = pl.program_id(0)

    @pl.when(step == 0)
    def _():
        for ref in (scores_ref, inter_ref, vlag_ref):
            ref[...] = jnp.zeros_like(ref)

    @pl.when(step % blocks_per_seq == 0)
    def _():
        st_ref[...] = jnp.zeros_like(st_ref)

    ts = q_ref.shape[0]
    n_chunks = ts // CHUNK

    for c in range(n_chunks):
        r0 = c * CHUNK
        o = inter_ref[c] + _dot(scores_ref[c], vlag_ref[r0:r0 + CHUNK, :])
        o_ref[r0:r0 + CHUNK, :] = (_rms(o, nw_ref[...]) * gate_ref[r0:r0 + CHUNK, :].astype(F32)).astype(BF16)

    lf = lf_ref[...]
    hi = lf.astype(BF16)
    r1 = lf - hi.astype(F32)
    mid = r1.astype(BF16)
    lo = (r1 - mid.astype(F32)).astype(BF16)
    by_chunk = lambda x: jnp.concatenate([x[c * CHUNK:(c + 1) * CHUNK] for c in range(n_chunks)], axis=1)
    b_all = _dot(tri_ref[...], jnp.concatenate([by_chunk(hi), by_chunk(mid), by_chunk(lo)], axis=0))

    row = lax.broadcasted_iota(jnp.int32, (CHUNK, HEAD_DIM), 0)
    t_idx = lax.broadcasted_iota(jnp.int32, (CHUNK, CHUNK), 0)
    s_idx = lax.broadcasted_iota(jnp.int32, (CHUNK, CHUNK), 1)
    levels = []
    half = CHUNK // 2
    while half >= SUB:
        size = 2 * half
        levels.append((half, row % size >= half,
                       (t_idx // size == s_idx // size) & (t_idx % size >= half) & (s_idx % size < half)))
        half //= 2
    causal_bias = [jnp.where(row % SUB >= j, 0.0, NEG_BIG) for j in range(SUB)]
    place = [s_idx == (t_idx // SUB) * SUB + j for j in range(SUB)]
    vlag_ref[...] = v_ref[...]
    for c in range(n_chunks):
        r0 = c * CHUNK
        b = b_all[:, c * HEAD_DIM:(c + 1) * HEAD_DIM]
        q = q_ref[r0:r0 + CHUNK, :].astype(F32)
        k = k_ref[r0:r0 + CHUNK, :].astype(F32)
        v = v_ref[r0:r0 + CHUNK, :]
        b_end = b[CHUNK - 1:CHUNK, :]
        st = st_ref[...]
        inter_ref[c] = _dot_nt((q * jnp.exp2(b)).astype(BF16), st.astype(BF16))

        scores = jnp.zeros((CHUNK, CHUNK), F32)
        for half, past_mid, pair in levels:
            size = 2 * half
            b_mid = jnp.concatenate(
                [jnp.broadcast_to(b[m + half - 1:m + half, :], (size, HEAD_DIM)) for m in range(0, CHUNK, size)], axis=0)
            x = (jnp.where(past_mid, q, k) * jnp.exp2(-jnp.abs(b - b_mid))).astype(BF16)
            scores = jnp.where(pair, _dot_nt(x, x), scores)

        bk_ref[0] = b
        bk_ref[1] = k
        for j in range(SUB):
            pick = lambda a: jnp.concatenate(
                [jnp.broadcast_to(bk_ref[a, m * SUB + j:m * SUB + j + 1, :], (SUB, HEAD_DIM))
                 for m in range(CHUNK // SUB)], axis=0)
            dec = jnp.exp2(b - pick(0) + causal_bias[j])
            col = jnp.sum(q * pick(1) * dec, axis=-1, keepdims=True)
            scores = jnp.where(place[j], col, scores)
        scores_ref[c] = scores.astype(BF16)

        k_end = (k * jnp.exp2(b_end - b)).astype(BF16)
        st_ref[...] = st * jnp.exp2(b_end) + _dot_tn(v, k_end)


def _hgrn(q, lf, k, v, gates, gate_col0, norm_w, *, batch, seq, ts=1024):
    n = q.shape[0]
    nt = seq // ts
    n_chunks = ts // CHUNK
    total = batch * HG_HEADS * nt

    def block(step, col0):
        step = jnp.clip(step, 0, total - 1)
        return (step // (HG_HEADS * nt)) * nt + step % nt, (step // nt) % HG_HEADS + col0

    fresh = pl.BlockSpec((ts, HEAD_DIM), lambda s: block(s, 0))
    lagged = lambda col0: pl.BlockSpec((ts, HEAD_DIM), lambda s: block(s - 1, col0))
    tri = (jnp.arange(CHUNK)[:, None] >= jnp.arange(CHUNK)[None, :]).astype(BF16)
    tri3 = jnp.concatenate([tri, tri, tri], axis=1)
    return pl.pallas_call(
        functools.partial(_hgrn_body, blocks_per_seq=nt),
        out_shape=jax.ShapeDtypeStruct((n, HG_HEADS * HEAD_DIM), BF16),
        grid=(total + 1,),
        in_specs=[fresh, fresh, fresh, fresh, lagged(gate_col0), _resident((1, HEAD_DIM)),
                  _resident((CHUNK, 3 * CHUNK))],
        out_specs=lagged(0),
        scratch_shapes=[pltpu.VMEM((HEAD_DIM, HEAD_DIM), F32),
                        pltpu.VMEM((n_chunks, CHUNK, CHUNK), BF16),
                        pltpu.VMEM((n_chunks, CHUNK, HEAD_DIM), F32),
                        pltpu.VMEM((ts, HEAD_DIM), BF16),
                        pltpu.VMEM((2, CHUNK, HEAD_DIM), F32)],
        compiler_params=_params("arbitrary"), name="hgrn2",
    )(q, lf, k, v, gates, norm_w.reshape(1, HEAD_DIM), tri3)


def _split3_bf16(x):
    hi = x.astype(BF16)
    r1 = x - hi.astype(F32)
    mid = r1.astype(BF16)
    lo = (r1 - mid.astype(F32)).astype(BF16)
    return hi, mid, lo


def _unit_lower_inverse_pairs(a_pairs):
    shape = a_pairs[0].shape
    row = lax.broadcasted_iota(jnp.int32, shape, 0)
    lane = lax.broadcasted_iota(jnp.int32, shape, 1)
    first = lane < CHUNK
    eye = jnp.where(row == lane % CHUNK, 1.0, 0.0)

    def level(power, partial):
        top = jnp.concatenate([jnp.where(first, power, 0.0), jnp.where(first, partial, 0.0)], axis=1)
        bottom = jnp.concatenate([jnp.where(first, 0.0, power), jnp.where(first, 0.0, partial)], axis=1)
        res = _dot(power.astype(BF16), jnp.concatenate([top, bottom], axis=0).astype(BF16))
        return res[:, :2 * CHUNK], partial + res[:, 2 * CHUNK:]

    state = [level(-a, eye) for a in a_pairs]
    span = 2
    while span < CHUNK:
        state = [level(power, partial) for power, partial in state]
        span *= 2
    return [partial for _, partial in state]


def _gdn_body(q_ref, k_ref, v_ref, z_ref, gb_ref, nw_ref, triu_ref, tril_ref, o_ref,
              s_ref, uw_ref, xt_ref, qg_ref, dec_ab_ref, lhs_ref, ktu_ref, qku_ref, dec_bc_ref,
              *, blocks_per_seq, lag):
    ts = q_ref.shape[0]
    n_chunks = ts // CHUNK
    heads = q_ref.shape[1] // HEAD_DIM
    rep = s_ref.shape[0] // heads
    dup = 2 * CHUNK
    pairs = [(h, c) for h in range(heads) for c in range(n_chunks)]
    prob = lambda h, c, i: (h * n_chunks + c) * rep + i

    step = pl.program_id(0)

    @pl.when(step == 0)
    def _():
        for ref in (uw_ref, xt_ref, qg_ref, dec_ab_ref, lhs_ref, ktu_ref, qku_ref, dec_bc_ref):
            ref[...] = jnp.zeros_like(ref)

    @pl.when((step == 0) | (step % blocks_per_seq == lag % blocks_per_seq))
    def _():
        s_ref[...] = jnp.zeros_like(s_ref)

    g_stack = jnp.concatenate([gb_ref[SUBLANES * h:SUBLANES * (h + 1), c * CHUNK:(c + 1) * CHUNK]
                               for h, c in pairs], axis=0)
    g_cols = jnp.concatenate([g_stack, jnp.zeros_like(g_stack)], axis=1).T[:CHUNK]
    gam_rows = _dot(jnp.concatenate(_split3_bf16(g_stack), axis=1), triu_ref[...])
    gam_cols = _dot(tril_ref[...], jnp.concatenate(_split3_bf16(g_cols), axis=0))
    t_dup = lax.broadcasted_iota(jnp.int32, (CHUNK, dup), 0)
    s_dup_raw = lax.broadcasted_iota(jnp.int32, (CHUNK, dup), 1)
    s_dup = s_dup_raw % CHUNK
    s_idx = lax.broadcasted_iota(jnp.int32, (CHUNK, CHUNK), 0)
    t_idx = lax.broadcasted_iota(jnp.int32, (CHUNK, CHUNK), 1)
    first = s_dup_raw < CHUNK
    a_pairs, rhs_all, xt_all, qg_all, dec_all = [], {}, {}, {}, {}
    for h, c in pairs:
        r0 = c * CHUNK
        q = q_ref[r0:r0 + CHUNK, h * HEAD_DIM:(h + 1) * HEAD_DIM]
        k = k_ref[r0:r0 + CHUNK, h * HEAD_DIM:(h + 1) * HEAD_DIM]
        qf = q.astype(F32)
        kf = k.astype(F32)
        kkq = _dot_nt(k, jnp.concatenate([k, k, q, q], axis=0))
        kk_dup = kkq[:, :dup]
        kq = kkq[:, dup:dup + CHUNK]
        col0 = SUBLANES * (h * n_chunks + c)
        gam_c = [gam_cols[:, col0 + i:col0 + i + 1] for i in range(rep)]
        beta = [g_cols[:, col0 + rep + i:col0 + rep + i + 1] for i in range(rep)]
        gam_r_pair = jnp.where(first[0:1], gam_rows[col0:col0 + 1, :], gam_rows[col0 + 1:col0 + 2, :])
        lmat_pair = jnp.exp(jnp.where(t_dup >= s_dup, jnp.where(first, gam_c[0], gam_c[1]) - gam_r_pair, NEG_BIG))
        a_pairs.append(jnp.where(t_dup > s_dup, jnp.where(first, beta[0], beta[1]) * kk_dup * lmat_pair, 0.0))
        for i in range(rep):
            p = prob(h, c, i)
            vcol = (h * rep + i) * HEAD_DIM
            gam_r = gam_rows[col0 + i:col0 + i + 1, :CHUNK]
            g_end = gam_c[i][CHUNK - 1:CHUNK, :]
            lmat_t = jnp.exp(jnp.where(t_idx >= s_idx, gam_r - gam_c[i], NEG_BIG))
            e_gam = jnp.exp(gam_c[i])
            v = v_ref[r0:r0 + CHUNK, vcol:vcol + HEAD_DIM].astype(F32)
            rhs_all[p] = jnp.concatenate([(v * beta[i]).astype(BF16), (kf * (beta[i] * e_gam)).astype(BF16)], axis=1)
            k_end = kf * jnp.exp(g_end - gam_c[i])
            xt_all[p] = jnp.concatenate([k_end, kq * lmat_t], axis=1).T.astype(BF16)
            qg_all[p] = qf * e_gam
            dec_all[p] = jnp.broadcast_to(jnp.exp(g_end), dec_ab_ref.shape[1:])

    zero_state = jnp.zeros((HEAD_DIM, HEAD_DIM), BF16)
    states = [s_ref[j] for j in range(heads * rep)]
    for c in range(n_chunks):
        r0 = c * CHUNK
        for h in range(heads):
            diag = jnp.concatenate(
                [jnp.concatenate([states[h * rep + i].astype(BF16) if j == i else zero_state for j in range(rep)],
                                 axis=1) for i in range(rep)], axis=0)
            m_all = _dot(lhs_ref[h * n_chunks + c], diag)
            for i in range(rep):
                p = prob(h, c, i)
                vcol = (h * rep + i) * HEAD_DIM
                m = m_all[:, i * HEAD_DIM:(i + 1) * HEAD_DIM]
                states[h * rep + i] = states[h * rep + i] * dec_bc_ref[p][0:1, 0:1] + ktu_ref[p] - m[:HEAD_DIM]
                o = m[HEAD_DIM:] + qku_ref[p]
                z = z_ref[r0:r0 + CHUNK, vcol:vcol + HEAD_DIM].astype(F32)
                o_ref[r0:r0 + CHUNK, vcol:vcol + HEAD_DIM] = (_rms(o, nw_ref[...]) * z).astype(BF16)
    for j in range(heads * rep):
        s_ref[j] = states[j]

    for h, c in pairs:
        lhs = []
        for i in range(rep):
            p = prob(h, c, i)
            r = _dot(xt_ref[p], uw_ref[p])
            ktu_ref[p] = r[:HEAD_DIM, :HEAD_DIM]
            qku_ref[p] = r[HEAD_DIM:, :HEAD_DIM]
            dec_bc_ref[p] = dec_ab_ref[p]
            lhs.append(jnp.concatenate([r[:HEAD_DIM, HEAD_DIM:], qg_ref[p] - r[HEAD_DIM:, HEAD_DIM:]], axis=0))
        lhs_ref[h * n_chunks + c] = jnp.concatenate(lhs, axis=1).astype(BF16)

    for p in range(len(rhs_all)):
        xt_ref[p] = xt_all[p]
        qg_ref[p] = qg_all[p]
        dec_ab_ref[p] = dec_all[p]
    inverses = _unit_lower_inverse_pairs(a_pairs)
    for n, (h, c) in enumerate(pairs):
        inv_pair = inverses[n].astype(BF16)
        for i in range(rep):
            p = prob(h, c, i)
            pad = jnp.zeros_like(rhs_all[p])
            stacked = jnp.concatenate([rhs_all[p], pad] if i == 0 else [pad, rhs_all[p]], axis=0)
            uw_ref[p] = _dot(inv_pair, stacked).astype(BF16)


def _gdn(q, k, v, gates, gab_t, norm_w, *, batch, seq, ts=128, heads=8):
    n = q.shape[0]
    nt = seq // ts
    groups = GDN_QK_HEADS // heads
    rep = GDN_V_HEADS // GDN_QK_HEADS
    n_chunks = ts // CHUNK
    n_pairs = heads * n_chunks
    n_probs = n_pairs * rep
    lag = 2
    total = batch * groups * nt

    def block(step):
        step = jnp.clip(step, 0, total - 1)
        return (step // (groups * nt)) * nt + step % nt, (step // nt) % groups

    fresh = lambda width: pl.BlockSpec((ts, width * HEAD_DIM), lambda s: block(s))
    lagged = lambda width: pl.BlockSpec((ts, width * HEAD_DIM), lambda s: block(s - lag))
    r = jnp.arange(CHUNK)[:, None]
    c = jnp.arange(CHUNK)[None, :]
    upper = (r <= c).astype(BF16)
    upper_dup = jnp.concatenate([upper, upper], axis=1)
    triu3 = jnp.concatenate([upper_dup] * 3, axis=0)
    tril3 = jnp.concatenate([(r >= c).astype(BF16)] * 3, axis=1)
    return pl.pallas_call(
        functools.partial(_gdn_body, blocks_per_seq=nt, lag=lag),
        out_shape=jax.ShapeDtypeStruct((n, GDN_V_HEADS * HEAD_DIM), BF16),
        grid=(total + lag,),
        in_specs=[fresh(heads), fresh(heads), fresh(heads * rep), lagged(heads * rep),
                  pl.BlockSpec((SUBLANES * heads, ts), lambda s: block(s)[::-1]),
                  _resident((1, HEAD_DIM)), _resident(triu3.shape), _resident(tril3.shape)],
        out_specs=lagged(heads * rep),
        scratch_shapes=[pltpu.VMEM((heads * rep, HEAD_DIM, HEAD_DIM), F32),
                        pltpu.VMEM((n_probs, CHUNK, 2 * HEAD_DIM), BF16),
                        pltpu.VMEM((n_probs, HEAD_DIM + CHUNK, CHUNK), BF16),
                        pltpu.VMEM((n_probs, CHUNK, HEAD_DIM), F32),
                        pltpu.VMEM((n_probs, SUBLANES, LANES), F32),
                        pltpu.VMEM((n_pairs, HEAD_DIM + CHUNK, rep * HEAD_DIM), BF16),
                        pltpu.VMEM((n_probs, HEAD_DIM, HEAD_DIM), F32),
                        pltpu.VMEM((n_probs, CHUNK, HEAD_DIM), F32),
                        pltpu.VMEM((n_probs, SUBLANES, LANES), F32)],
        compiler_params=_params("arbitrary"), name="gated_delta",
    )(q, k, v, gates, gab_t, norm_w.reshape(1, HEAD_DIM), triu3, tril3)


def _merge_body(oh_ref, og_ref, gate_ref, h_ref, wbh_ref, wbg_ref, wo_ref, o_ref):
    d = h_ref.shape[1]
    yh = _dot(oh_ref[...], wbh_ref[...].astype(BF16))
    yg = _dot(og_ref[...], wbg_ref[...].astype(BF16))
    y = gate_ref[:, :d].astype(F32) * yh + gate_ref[:, d:].astype(F32) * yg
    o_ref[...] = h_ref[...] + _dot(y.astype(BF16), wo_ref[...].astype(BF16))


def _merge(o_h, o_g, merge_gates, h, wbh, wbg, wo, *, tm=512):
    n, d = h.shape
    row = lambda width: pl.BlockSpec((tm, width), lambda i: (i, 0))
    return pl.pallas_call(
        _merge_body, out_shape=jax.ShapeDtypeStruct((n, d), F32), grid=(n // tm,),
        in_specs=[row(o_h.shape[1]), row(o_g.shape[1]), row(2 * d), row(d),
                  _resident(wbh.shape), _resident(wbg.shape), _resident(wo.shape)],
        out_specs=row(d),
        compiler_params=_params("arbitrary"), name="merge",
    )(o_h, o_g, merge_gates, h, wbh, wbg, wo)


def _mixer(u, h, w_in, lb_logits, hgrn_norm, conv_w, a_log, dt_bias, gdn_norm, wbh, wbg, wo, *, layer, batch, seq):
    n, d = h.shape
    hk = HG_HEADS * HEAD_DIM
    gk = GDN_QK_HEADS * HEAD_DIM
    gv = GDN_V_HEADS * HEAD_DIM
    sizes = (hk, hk, hk, hk, gk, gk, gv, GDN_V_HEADS, GDN_V_HEADS, gv, d, d)
    offs = [0]
    for s in sizes:
        offs.append(offs[-1] + s)
    conv_pad = jnp.zeros((SUBLANES - CONV_K, conv_w.shape[1]), F32)
    cw = jnp.concatenate([conv_w, conv_pad], axis=0)

    rep = GDN_V_HEADS // GDN_QK_HEADS
    pad = SUBLANES - 2 * rep
    regroup = lambda a, b, fill: jnp.concatenate(
        [a.reshape(-1, GDN_QK_HEADS, rep), b.reshape(-1, GDN_QK_HEADS, rep),
         jnp.full((a.shape[0], GDN_QK_HEADS, pad), fill, a.dtype)], axis=2).reshape(a.shape[0], -1)
    eye = jnp.eye(GDN_V_HEADS, dtype=F32)
    zero = jnp.zeros_like(eye)
    perm = jnp.concatenate([regroup(eye, zero, 0.0), regroup(zero, eye, 0.0)], axis=0)
    perm = jnp.pad(perm, ((0, 0), (0, LANES - perm.shape[1]))).astype(BF16)
    zeros = jnp.zeros((1, GDN_V_HEADS), F32)
    aux = jnp.concatenate([regroup(a_log.reshape(1, -1).astype(F32), zeros, 0.0),
                           regroup(dt_bias.reshape(1, -1).astype(F32), zeros, 0.0)], axis=0)
    aux = jnp.pad(aux, ((0, 0), (0, LANES - aux.shape[1])))

    moves, dst = [], 0
    for a, b in ((0, 3), (9, 10), (3, 4), (10, 12), (4, 7)):
        moves.append((offs[a], dst, offs[b] - offs[a]))
        dst += offs[b] - offs[a]
    w_all = _regroup(w_in, perm, tuple(moves), (offs[7], offs[9] - offs[7]), layer=layer)
    widths = (hk, hk, hk, gv + hk, 2 * d, 2 * gk, gv)
    q_h, lf, k_h, v_h, act_gates, merge_gates, q_g, k_g, v_g, gab_t = _proj(
        u, w_all, lb_logits.astype(F32), cw, aux, widths, layer=layer, seq=seq)

    o_h = _hgrn(q_h, lf, k_h, v_h, act_gates, gv // HEAD_DIM, hgrn_norm, batch=batch, seq=seq)
    o_g = _gdn(q_g, k_g, v_g, act_gates, gab_t, gdn_norm, batch=batch, seq=seq)
    return _merge(o_h, o_g, merge_gates, h, wbh, wbg, wo)


def kernel(x, ffn1_norm, ffn1_w_in, ffn1_w_out, mix_norm, w_in, hgrn_lb_logits, hgrn_out_norm, gdn_conv_w,
           gdn_a_log, gdn_dt_bias, gdn_out_norm, w_branch_hgrn, w_branch_gdn, w_out, ffn2_norm, ffn2_w_in,
           ffn2_w_out, final_norm):
    batch, seq, d = x.shape
    depth = ffn1_norm.shape[0]
    h = x.reshape(batch * seq, d)
    for l in range(depth):
        h, u = _ffn(h, ffn1_norm[l], ffn1_w_in[l], ffn1_w_out[l], mix_norm[l], final=False)
        h = _mixer(u, h, w_in, hgrn_lb_logits, hgrn_out_norm[l], gdn_conv_w[l], gdn_a_log[l],
                   gdn_dt_bias[l], gdn_out_norm[l], w_branch_hgrn[l], w_branch_gdn[l], w_out[l],
                   layer=l, batch=batch, seq=seq)
        last = l == depth - 1
        nxt = final_norm if last else ffn1_norm[l + 1]
        out = _ffn(h, ffn2_norm[l], ffn2_w_in[l], ffn2_w_out[l], nxt, final=last)
        h = out if last else out[0]
    return h.reshape(batch, seq, d)
```
